```python
import jax
import jax.numpy as jnp
from jax import lax
import numpy as np

D_MODEL = 1024
BATCH = 16
SEQ = 2048
DEPTH = 2

N_AB = (DEPTH + 1) // 2
N_CD = DEPTH // 2
D_FF = 4 * D_MODEL
EPS = 1e-5
SHORT_CONV = 3

RW_HEADS = 8
RW_HEAD = 64
RW_DIM = RW_HEADS * RW_HEAD
RW_DECAY_RANK = 64
RW_AAA_RANK = 64
RW_GATE_RANK = 128
RW_COLS = 3 * RW_DIM + RW_DECAY_RANK + RW_AAA_RANK + RW_GATE_RANK
RW_GN_EPS = 64e-5

MB_HEADDIM = 64
MB_HEADS = 16
MB_DIM = MB_HEADS * MB_HEADDIM
MB_GROUPS = 2
MB_HPG = MB_HEADS // MB_GROUPS
MB_STATE = 128
MB_CHUNK = 128
MB_XBC = MB_DIM + 2 * MB_GROUPS * MB_STATE
MB_COLS = MB_DIM + MB_XBC + 2 * MB_HEADS

AB_IN = RW_COLS + MB_COLS
AB_OUT = RW_DIM + MB_DIM

S5_GROUP = 16
S5_GROUPS = 32
S5_DIM = S5_GROUP * S5_GROUPS
S5_STATE = 64

ML_HEADS = 8
ML_HEAD = 128
ML_DIM = ML_HEADS * ML_HEAD
ML_BLOCK = 4
ML_CHUNK = 64
ML_COLS = 2 * ML_DIM + 4 * ML_HEADS

CD_IN = S5_DIM + ML_COLS
CD_OUT = S5_DIM + ML_DIM

kernel_name = "hybrid_bidir_rwkv7_mamba2_s5_mlstm"


def _split(t, sizes):
    return jnp.split(t, np.cumsum(sizes)[:-1].tolist(), axis=-1)


def rmsnorm(x, w):
    xf = x.astype(jnp.float32)
    y = xf * lax.rsqrt(jnp.mean(xf * xf, axis=-1, keepdims=True) + EPS)
    return (y * w.astype(jnp.float32)).astype(x.dtype)


def head_norm(x, w, eps):
    xf = x.astype(jnp.float32)
    xc = xf - jnp.mean(xf, axis=-1, keepdims=True)
    y = xc * lax.rsqrt(jnp.mean(xc * xc, axis=-1, keepdims=True) + eps)
    return y.reshape(x.shape[:-2] + (-1,)) * w.astype(jnp.float32)


def to_dirs(t):
    return jnp.stack([t, jnp.flip(t, axis=1)])


def flip_dir1(t):
    return jnp.stack([t[0], jnp.flip(t[1], axis=1)])


def merge_dirs(t):
    return t.reshape((-1,) + t.shape[2:])


def centred_shift(y, mu):
    prev = jnp.pad(y[:, :-1], ((0, 0), (1, 0), (0, 0)))
    nxt = jnp.pad(y[:, 1:], ((0, 0), (0, 1), (0, 0)))
    return y + mu[0] * (prev - y) + mu[1] * (nxt - y)


def centred_dwconv(x, w, b):
    K, C = w.shape
    out = lax.conv_general_dilated(
        x, w[:, None, :], window_strides=(1,), padding=[((K - 1) // 2, (K - 1) // 2)],
        dimension_numbers=("NWC", "WIO", "NWC"), feature_group_count=C)
    return out + b


def rwkv7_scan(r, w, k, v, a, b):
    n, H, K = r.shape[1], r.shape[2], r.shape[3]

    def step(state, inp):
        r_t, w_t, k_t, v_t, a_t, b_t = inp
        sa = jnp.einsum("nhvk,nhk->nhv", state, a_t)
        state = (state * w_t[:, :, None, :] + sa[..., None] * b_t[:, :, None, :]
                 + v_t[..., None] * k_t[:, :, None, :])
        return state, jnp.einsum("nhvk,nhk->nhv", state, r_t)

    _, y = lax.scan(step, jnp.zeros((n, H, K, K), r.dtype), (r, w, k, v, a, b))
    return y


def rwkv7_mixer(cols, mu, w0, w2, a0, a2, g2, k_k, k_a, r_k, ln_w):
    bsz, seq, _ = cols.shape
    cols = centred_shift(cols, mu).astype(jnp.float32)
    r, k, v, w_lr, a_lr, g_lr = _split(
        cols, [RW_DIM, RW_DIM, RW_DIM, RW_DECAY_RANK, RW_AAA_RANK, RW_GATE_RANK])
    heads = lambda t: t.reshape(t.shape[:-1] + (RW_HEADS, RW_HEAD))
    w_log = -jax.nn.softplus(-(w0[:, None, None, :]
                               + jnp.einsum("bsr,drc->dbsc", jnp.tanh(w_lr), w2))) - 0.5
    decay = jnp.exp(-jnp.exp(w_log))
    a = jax.nn.sigmoid(a0 + a_lr @ a2)
    g = jax.nn.sigmoid(g_lr) @ g2
    kk = heads(k * k_k)
    kk = kk * lax.rsqrt(jnp.maximum(jnp.sum(kk * kk, axis=-1, keepdims=True), 1e-12))
    k = k * (1.0 + (a - 1.0) * k_a)
    r_h, k_h, v_h, a_h = heads(r), heads(k), heads(v), heads(a)
    tm = lambda t: jnp.swapaxes(merge_dirs(t), 0, 1)
    y = rwkv7_scan(tm(to_dirs(r_h)), tm(flip_dir1(heads(decay))), tm(to_dirs(k_h)),
                   tm(to_dirs(v_h)), tm(to_dirs(-kk)), tm(to_dirs(kk * a_h)))
    y = flip_dir1(jnp.swapaxes(y, 0, 1).reshape(2, bsz, seq, RW_HEADS, RW_HEAD)).sum(0)
    y = head_norm(y, ln_w, RW_GN_EPS)
    bonus = (jnp.sum(r_h * k_h * r_k, axis=-1, keepdims=True) * v_h).reshape(bsz, seq, RW_DIM)
    return (y + bonus) * g


def segsum_exp(a):
    T = a.shape[-1]
    cs = jnp.cumsum(a, axis=-1)
    tri = jnp.tril(jnp.ones((T, T), bool))
    return jnp.exp(jnp.where(tri, cs[..., :, None] - cs[..., None, :], -jnp.inf))


def ssd_chunked(xdt, la, bm, cm, chunk):
    n, S, G, E, P = xdt.shape
    N = bm.shape[-1]
    nc = S // chunk
    xdt = xdt.reshape(n, nc, chunk, G, E, P)
    bm = bm.reshape(n, nc, chunk, G, N)
    cm = cm.reshape(n, nc, chunk, G, N)
    la = jnp.moveaxis(la.reshape(n, nc, chunk, G, E), (3, 4), (1, 2))
    a_cs = jnp.cumsum(la, axis=-1)
    scores = jnp.einsum("nclgd,ncsgd->ngcls", cm, bm)
    m = scores[:, :, None] * segsum_exp(la)
    y_diag = jnp.einsum("ngecls,ncsgep->nclgep", m, xdt)
    decay_states = jnp.exp(a_cs[..., -1:] - a_cs)
    states = jnp.einsum("nclgd,ngecl,nclgep->ncgepd", bm, decay_states, xdt)
    states = jnp.concatenate([jnp.zeros_like(states[:, :1]), states], axis=1)
    chunk_decay = segsum_exp(jnp.pad(a_cs[..., -1], ((0, 0), (0, 0), (0, 0), (1, 0))))
    states = jnp.einsum("ngezc,ncgepd->nzgepd", chunk_decay, states)[:, :-1]
    y_off = jnp.einsum("nclgd,ncgepd,ngecl->nclgep", cm, states, jnp.exp(a_cs))
    return (y_diag + y_off).reshape(n, S, G, E, P)


def mamba2_mixer(cols, conv_w, conv_b, dt_bias, A_log, Dskip, norm_w):
    bsz, seq, _ = cols.shape
    f32 = jnp.float32
    z, xbc, dt = _split(cols, [MB_DIM, MB_XBC, 2 * MB_HEADS])
    xbc = jax.nn.silu(centred_dwconv(xbc, conv_w, conv_b))
    xs, bm, cm = _split(xbc, [MB_DIM, MB_GROUPS * MB_STATE, MB_GROUPS * MB_STATE])
    xs = xs.reshape(bsz, seq, MB_GROUPS, MB_HPG, MB_HEADDIM).astype(f32)
    bm = bm.reshape(bsz, seq, MB_GROUPS, MB_STATE).astype(f32)
    cm = cm.reshape(bsz, seq, MB_GROUPS, MB_STATE).astype(f32)
    dt = jax.nn.softplus(dt.reshape(bsz, seq, 2, MB_HEADS).astype(f32) + dt_bias.astype(f32))
    dt = flip_dir1(jnp.moveaxis(dt, 2, 0)).reshape(2, bsz, seq, MB_GROUPS, MB_HPG)
    A = -jnp.exp(A_log.astype(f32)).reshape(2, 1, 1, MB_GROUPS, MB_HPG)
    xdt = to_dirs(xs) * dt[..., None]
    y = ssd_chunked(merge_dirs(xdt), merge_dirs(dt * A), merge_dirs(to_dirs(bm)),
                    merge_dirs(to_dirs(cm)), MB_CHUNK)
    y = flip_dir1(y.reshape((2, bsz) + y.shape[1:])).sum(0)
    y = y + Dskip.reshape(MB_GROUPS, MB_HPG, 1) * xs
    y = (y.reshape(bsz, seq, MB_DIM) * jax.nn.silu(z.astype(f32)))
    y = y.reshape(bsz, seq, MB_GROUPS, MB_DIM // MB_GROUPS)
    y = y * lax.rsqrt(jnp.mean(y * y, axis=-1, keepdims=True) + EPS)
    return y.reshape(bsz, seq, MB_DIM) * norm_w


def _s5_combine(e1, e2):
    a1r, a1i, b1r, b1i = e1
    a2r, a2i, b2r, b2i = e2
    return (a2r * a1r - a2i * a1i, a2r * a1i + a2i * a1r,
            a2r * b1r - a2i * b1i + b2r, a2r * b1i + a2i * b1r + b2i)


def s5_mixer(u, A_re, A_im, log_dt, B_re, B_im, C_re, C_im, Dskip, glu_w, glu_b):
    bsz, seq, _ = u.shape
    f32 = jnp.float32
    uf = u.astype(f32)
    ug = uf.reshape(bsz, seq, S5_GROUPS, S5_GROUP)
    Bre, Bim = B_re.astype(f32), B_im.astype(f32)
    y = Dskip * uf
    for d in range(2):
        dt = jnp.exp(log_dt[d].astype(f32))[:, None]
        ar = jnp.minimum(A_re[d].astype(f32), -1e-4)
        ai = A_im[d].astype(f32)
        mag = jnp.exp(dt * ar)
        abr, abi = mag * jnp.cos(dt * ai), mag * jnp.sin(dt * ai)
        den = ar * ar + ai * ai
        fr = ((abr - 1.0) * ar + abi * ai) / den
        fi = (abi * ar - (abr - 1.0) * ai) / den
        bbr = fr[..., None] * Bre - fi[..., None] * Bim
        bbi = fr[..., None] * Bim + fi[..., None] * Bre
        ud = ug if d == 0 else jnp.flip(ug, axis=1)
        bur = jnp.einsum("bsgm,gpm->bsgp", ud, bbr)
        bui = jnp.einsum("bsgm,gpm->bsgp", ud, bbi)
        a_shape = (1, seq, S5_GROUPS, S5_STATE)
        _, _, xr, xi = lax.associative_scan(
            _s5_combine, (jnp.broadcast_to(abr, a_shape), jnp.broadcast_to(abi, a_shape), bur, bui),
            axis=1)
        yd = (jnp.einsum("bsgp,gmp->bsgm", xr, C_re[d].astype(f32))
              - jnp.einsum("bsgp,gmp->bsgm", xi, C_im[d].astype(f32)))
        if d == 1:
            yd = jnp.flip(yd, axis=1)
        y = y + yd.reshape(bsz, seq, S5_DIM)
    y = jax.nn.gelu(y)
    return y * jax.nn.sigmoid(y @ glu_w + glu_b)


def mlstm_chunkwise(q, k, v, log_i, log_f, chunk):
    n, S, H, dh = q.shape
    nc = S // chunk

    def to_chunks(t):
        t = t.reshape((n, nc, chunk, H) + t.shape[3:])
        return jnp.moveaxis(t, (1, 3), (0, 2))

    tri = jnp.tril(jnp.ones((chunk, chunk), bool))

    def step(carry, inp):
        C, nv, m = carry
        qc, kc, vc, li, lf = inp
        b = jnp.cumsum(lf, axis=-1)
        log_d = jnp.where(tri, b[..., :, None] - b[..., None, :] + li[..., None, :], -jnp.inf)
        inter = b + m[..., None]
        m_t = jnp.maximum(jnp.max(log_d, axis=-1), inter)
        s = jnp.einsum("nhtd,nhsd->nhts", qc, kc) * jnp.exp(log_d - m_t[..., None])
        w_in = jnp.exp(inter - m_t)
        num = (jnp.einsum("nhts,nhsv->nhtv", s, vc)
               + w_in[..., None] * jnp.einsum("nhtd,nhdv->nhtv", qc, C))
        den = s.sum(-1) + w_in * jnp.einsum("nhtd,nhd->nht", qc, nv)
        h = num / jnp.maximum(jnp.abs(den), jnp.exp(-m_t))[..., None]
        b_last = b[..., -1]
        log_w = b_last[..., None] - b + li
        m_new = jnp.maximum(b_last + m, jnp.max(log_w, axis=-1))
        w = jnp.exp(log_w - m_new[..., None])
        dec = jnp.exp(b_last + m - m_new)
        C = dec[..., None, None] * C + jnp.einsum("nhs,nhsd,nhsv->nhdv", w, kc, vc)
        nv = dec[..., None] * nv + jnp.einsum("nhs,nhsd->nhd", w, kc)
        return (C, nv, m_new), h

    init = (jnp.zeros((n, H, dh, dh), q.dtype), jnp.zeros((n, H, dh), q.dtype),
            jnp.zeros((n, H), q.dtype))
    _, h = lax.scan(step, init, (to_chunks(q), to_chunks(k), to_chunks(v),
                                 to_chunks(log_i), to_chunks(log_f)))
    return jnp.moveaxis(h, (0, 2), (1, 3)).reshape(n, S, H, dh)


def mlstm_mixer(cols, conv_w, conv_b, wq, wk, wv, i_b, f_b, norm_w, skip):
    bsz, seq, _ = cols.shape
    f32 = jnp.float32
    xm, o_pre, i_pre, f_pre = _split(cols, [ML_DIM, ML_DIM, 2 * ML_HEADS, 2 * ML_HEADS])
    xc = jax.nn.silu(centred_dwconv(xm, conv_w, conv_b))

    def headwise(t, w):
        t = t.reshape(bsz, seq, ML_DIM // ML_BLOCK, ML_BLOCK)
        return jnp.einsum("bsjc,jcd->bsjd", t, w).reshape(bsz, seq, ML_HEADS, ML_HEAD).astype(f32)

    q = headwise(xc, wq)
    k = headwise(xc, wk) * (ML_HEAD ** -0.5)
    v = headwise(xm, wv)

    def gate_dirs(pre, bias):
        g = pre.reshape(bsz, seq, 2, ML_HEADS).astype(f32) + bias.astype(f32)
        return merge_dirs(flip_dir1(jnp.moveaxis(g, 2, 0)))

    log_i = gate_dirs(i_pre, i_b)
    log_f = jax.nn.log_sigmoid(gate_dirs(f_pre, f_b))
    h = mlstm_chunkwise(merge_dirs(to_dirs(q)), merge_dirs(to_dirs(k)),
                        merge_dirs(to_dirs(v)), log_i, log_f, ML_CHUNK)
    h = flip_dir1(h.reshape(2, bsz, seq, ML_HEADS, ML_HEAD)).sum(0)
    h = head_norm(h, norm_w, EPS)
    return jax.nn.sigmoid(o_pre.astype(f32)) * h + skip * xc


def sq_relu_mlp(x, w1, w2):
    return jnp.square(jax.nn.relu(x @ w1)) @ w2


def setup_inputs(seed: int = 0) -> dict:
    key = jax.random.key(seed)
    ks = iter(jax.random.split(key, 64))
    f32 = jnp.float32
    nrm = lambda shape, scale: scale * jax.random.normal(next(ks), shape, f32)
    uni = lambda shape, lo, hi: jax.random.uniform(next(ks), shape, f32, lo, hi)
    gain = lambda shape: 1.0 + nrm(shape, 0.02)

    x = nrm((BATCH, SEQ, D_MODEL), 1.0)
    norm_mix = gain((DEPTH, D_MODEL))
    norm_mlp = gain((DEPTH, D_MODEL))
    norm_final = gain((D_MODEL,))
    mlp_w1 = nrm((DEPTH, D_MODEL, D_FF), D_MODEL ** -0.5)
    mlp_w2 = nrm((DEPTH, D_FF, D_MODEL), 0.5 * D_FF ** -0.5)

    ab_w_in = nrm((N_AB, D_MODEL, AB_IN), D_MODEL ** -0.5)
    ab_w_out = nrm((N_AB, AB_OUT, D_MODEL), AB_OUT ** -0.5)
    rw_mu = uni((N_AB, 2, RW_COLS), 0.0, 0.4)
    rw_w0 = uni((N_AB, 2, RW_DIM), -6.0, 1.0)
    rw_w2 = nrm((N_AB, 2, RW_DECAY_RANK, RW_DIM), 0.1)
    rw_a0 = nrm((N_AB, RW_DIM), 0.1)
    rw_a2 = nrm((N_AB, RW_AAA_RANK, RW_DIM), 0.1)
    rw_g2 = nrm((N_AB, RW_GATE_RANK, RW_DIM), RW_GATE_RANK ** -0.5)
    rw_k_k = 0.85 + nrm((N_AB, RW_DIM), 0.02)
    rw_k_a = gain((N_AB, RW_DIM))
    rw_r_k = nrm((N_AB, RW_HEADS, RW_HEAD), 0.1)
    rw_ln_w = gain((N_AB, RW_DIM))
    mb_conv_w = nrm((N_AB, SHORT_CONV, MB_XBC), SHORT_CONV ** -0.5)
    mb_conv_b = nrm((N_AB, MB_XBC), 0.02)
    dt0 = jnp.exp(uni((N_AB, 2, MB_HEADS), float(np.log(1e-3)), float(np.log(1e-1))))
    mb_dt_bias = dt0 + jnp.log(-jnp.expm1(-dt0))
    mb_A_log = jnp.log(uni((N_AB, 2, MB_HEADS), 1.0, 16.0))
    mb_D = gain((N_AB, MB_HEADS))
    mb_norm_w = gain((N_AB, MB_DIM))

    cd_w_in = nrm((N_CD, D_MODEL, CD_IN), D_MODEL ** -0.5)
    cd_w_out = nrm((N_CD, CD_OUT, D_MODEL), CD_OUT ** -0.5)
    s5_A_re = -0.5 + nrm((N_CD, 2, S5_GROUPS, S5_STATE), 0.01)
    s5_A_im = (jnp.pi * jnp.arange(S5_STATE, dtype=f32)
               + nrm((N_CD, 2, S5_GROUPS, S5_STATE), 0.01))
    s5_log_dt = uni((N_CD, 2, S5_GROUPS), float(np.log(1e-3)), float(np.log(1e-1)))
    s5_B_re = nrm((N_CD, S5_GROUPS, S5_STATE, S5_GROUP), (2 * S5_GROUP) ** -0.5)
    s5_B_im = nrm((N_CD, S5_GROUPS, S5_STATE, S5_GROUP), (2 * S5_GROUP) ** -0.5)
    s5_C_re = nrm((N_CD, 2, S5_GROUPS, S5_GROUP, S5_STATE), S5_STATE ** -0.5)
    s5_C_im = nrm((N_CD, 2, S5_GROUPS, S5_GROUP, S5_STATE), S5_STATE ** -0.5)
    s5_D = nrm((N_CD, S5_DIM), 1.0)
    s5_glu_w = nrm((N_CD, S5_DIM, S5_DIM), S5_DIM ** -0.5)
    s5_glu_b = nrm((N_CD, S5_DIM), 0.02)
    ml_conv_w = nrm((N_CD, SHORT_CONV, ML_DIM), SHORT_CONV ** -0.5)
    ml_conv_b = nrm((N_CD, ML_DIM), 0.02)
    ml_wq = nrm((N_CD, ML_DIM // ML_BLOCK, ML_BLOCK, ML_BLOCK), ML_BLOCK ** -0.5)
    ml_wk = nrm((N_CD, ML_DIM // ML_BLOCK, ML_BLOCK, ML_BLOCK), ML_BLOCK ** -0.5)
    ml_wv = nrm((N_CD, ML_DIM // ML_BLOCK, ML_BLOCK, ML_BLOCK), ML_BLOCK ** -0.5)
    ml_i_b = nrm((N_CD, 2, ML_HEADS), 0.1)
    ml_f_b = jnp.linspace(3.0, 6.0, ML_HEADS, dtype=f32) + nrm((N_CD, 2, ML_HEADS), 0.02)
    ml_norm_w = gain((N_CD, ML_DIM))
    ml_skip = gain((N_CD, ML_DIM))

    return {
        "x": x, "norm_mix": norm_mix, "norm_mlp": norm_mlp, "norm_final": norm_final,
        "mlp_w1": mlp_w1, "mlp_w2": mlp_w2,
        "ab_w_in": ab_w_in, "ab_w_out": ab_w_out,
        "rw_mu": rw_mu, "rw_w0": rw_w0, "rw_w2": rw_w2, "rw_a0": rw_a0, "rw_a2": rw_a2,
        "rw_g2": rw_g2, "rw_k_k": rw_k_k, "rw_k_a": rw_k_a, "rw_r_k": rw_r_k,
        "rw_ln_w": rw_ln_w,
        "mb_conv_w": mb_conv_w, "mb_conv_b": mb_conv_b, "mb_dt_bias": mb_dt_bias,
        "mb_A_log": mb_A_log, "mb_D": mb_D, "mb_norm_w": mb_norm_w,
        "cd_w_in": cd_w_in, "cd_w_out": cd_w_out,
        "s5_A_re": s5_A_re, "s5_A_im": s5_A_im, "s5_log_dt": s5_log_dt,
        "s5_B_re": s5_B_re, "s5_B_im": s5_B_im, "s5_C_re": s5_C_re, "s5_C_im": s5_C_im,
        "s5_D": s5_D, "s5_glu_w": s5_glu_w, "s5_glu_b": s5_glu_b,
        "ml_conv_w": ml_conv_w, "ml_conv_b": ml_conv_b, "ml_wq": ml_wq, "ml_wk": ml_wk,
        "ml_wv": ml_wv, "ml_i_b": ml_i_b, "ml_f_b": ml_f_b, "ml_norm_w": ml_norm_w,
        "ml_skip": ml_skip,
    }


def reference(x, norm_mix, norm_mlp, norm_final, mlp_w1, mlp_w2,
              ab_w_in, ab_w_out, rw_mu, rw_w0, rw_w2, rw_a0, rw_a2, rw_g2, rw_k_k, rw_k_a,
              rw_r_k, rw_ln_w, mb_conv_w, mb_conv_b, mb_dt_bias, mb_A_log, mb_D, mb_norm_w,
              cd_w_in, cd_w_out, s5_A_re, s5_A_im, s5_log_dt, s5_B_re, s5_B_im, s5_C_re,
              s5_C_im, s5_D, s5_glu_w, s5_glu_b, ml_conv_w, ml_conv_b, ml_wq, ml_wk, ml_wv,
              ml_i_b, ml_f_b, ml_norm_w, ml_skip):
    h = x
    for layer in range(DEPTH):
        xn = rmsnorm(h, norm_mix[layer])
        i = layer // 2
        if layer % 2 == 0:
            cols = xn @ ab_w_in[i]
            rw_cols, mb_cols = _split(cols, [RW_COLS, MB_COLS])
            y_rw = rwkv7_mixer(rw_cols, rw_mu[i], rw_w0[i], rw_w2[i], rw_a0[i], rw_a2[i],
                               rw_g2[i], rw_k_k[i], rw_k_a[i], rw_r_k[i], rw_ln_w[i])
            y_mb = mamba2_mixer(mb_cols, mb_conv_w[i], mb_conv_b[i], mb_dt_bias[i],
                                mb_A_log[i], mb_D[i], mb_norm_w[i])
            mixed = jnp.concatenate([y_rw, y_mb], axis=-1).astype(xn.dtype) @ ab_w_out[i]
        else:
            cols = xn @ cd_w_in[i]
            s5_cols, ml_cols = _split(cols, [S5_DIM, ML_COLS])
            y_s5 = s5_mixer(s5_cols, s5_A_re[i], s5_A_im[i], s5_log_dt[i], s5_B_re[i],
                            s5_B_im[i], s5_C_re[i], s5_C_im[i], s5_D[i], s5_glu_w[i], s5_glu_b[i])
            y_ml = mlstm_mixer(ml_cols, ml_conv_w[i], ml_conv_b[i], ml_wq[i], ml_wk[i], ml_wv[i],
                               ml_i_b[i], ml_f_b[i], ml_norm_w[i], ml_skip[i])
            mixed = jnp.concatenate([y_s5, y_ml], axis=-1).astype(xn.dtype) @ cd_w_out[i]
        h = h + mixed
        h = h + sq_relu_mlp(rmsnorm(h, norm_mlp[layer]), mlp_w1[layer], mlp_w2[layer])
    return rmsnorm(h, norm_final)
```

```python
import functools
import math

import jax
import jax.numpy as jnp
import numpy as np
from jax import lax
from jax.experimental import pallas as pl
from jax.experimental.pallas import tpu as pltpu

F32 = jnp.float32
BF16 = jnp.bfloat16

EPS = 1e-5
LANES = 128
SUBLANES = 8
VMEM_LIMIT_BYTES = 48 * 1024 * 1024

D_MODEL = 1024
D_FF = 4 * D_MODEL

RW_HEADS, RW_HEAD = 8, 64
RW_DIM = RW_HEADS * RW_HEAD
RW_LR = 64 + 64 + 128
RW_COLS = 3 * RW_DIM + RW_LR
RW_GN_EPS = 64e-5
RW_CHUNK = 64

MB_HEADS, MB_HEADDIM, MB_GROUPS, MB_STATE = 16, 64, 2, 128
MB_HPG = MB_HEADS // MB_GROUPS
MB_DIM = MB_HEADS * MB_HEADDIM
MB_BC = 2 * MB_GROUPS * MB_STATE
MB_COLS_PAD = 2 * MB_DIM + MB_BC + LANES
MB_CHUNK = 128


def _cparams(*sem):
    return pltpu.CompilerParams(dimension_semantics=sem, vmem_limit_bytes=VMEM_LIMIT_BYTES)


def _bdot(a, b):
    return jnp.dot(a.astype(BF16), b.astype(BF16), preferred_element_type=F32)


def _bdot_nt(a, b):
    return lax.dot_general(a.astype(BF16), b.astype(BF16), (((1,), (1,)), ((), ())),
                           preferred_element_type=F32)


def _bdot_tn(a, b):
    return lax.dot_general(a.astype(BF16), b.astype(BF16), (((0,), (0,)), ((), ())),
                           preferred_element_type=F32)


def _split3(x):
    hi = x.astype(BF16)
    r = x - hi.astype(F32)
    mid = r.astype(BF16)
    lo = (r - mid.astype(F32)).astype(BF16)
    return hi, mid, lo


def _xdot_l(x, m):
    hi, mid, lo = _split3(x)
    dot = functools.partial(jnp.dot, preferred_element_type=F32)
    return dot(lo, m) + dot(mid, m) + dot(hi, m)


def _xdot_r(m, x):
    hi, mid, lo = _split3(x)
    dot = functools.partial(jnp.dot, preferred_element_type=F32)
    return dot(m, lo) + dot(m, mid) + dot(m, hi)


def _softplus(u):
    return jnp.maximum(u, 0.0) + jnp.log1p(jnp.exp(-jnp.abs(u)))


def _sigmoid(u):
    return 1.0 / (1.0 + jnp.exp(-u))


def _rms(x, gain):
    return x * lax.rsqrt(jnp.mean(x * x, axis=-1, keepdims=True) + EPS) * gain


def _shifted(x, prev_row, next_row):
    n = x.shape[0]
    rows = lax.broadcasted_iota(jnp.int32, x.shape, 0)
    x_prev = jnp.where(rows == 0, prev_row, pltpu.roll(x, 1, 0))
    x_next = jnp.where(rows == n - 1, next_row, pltpu.roll(x, n - 1, 0))
    return x_prev, x_next


def _halo_rows(xp_ref, xn_ref, tiles_per_seq):
    si = pl.program_id(0) % tiles_per_seq
    prev_row = jnp.where(si == 0, 0.0, xp_ref[SUBLANES - 1:SUBLANES, :])
    next_row = jnp.where(si == tiles_per_seq - 1, 0.0, xn_ref[0:1, :])
    return prev_row, next_row


def _halo_specs(ts, width, col_block, n_rows):
    per = ts // SUBLANES
    last = n_rows // SUBLANES - 1
    prev = pl.BlockSpec((SUBLANES, width), lambda i: (jnp.maximum(i * per - 1, 0), col_block))
    nxt = pl.BlockSpec((SUBLANES, width), lambda i: (jnp.minimum((i + 1) * per, last), col_block))
    return prev, nxt


def _scan_masks(n, period, sgn):
    ri = lax.broadcasted_iota(jnp.int32, (n, n), 0)
    ci = lax.broadcasted_iota(jnp.int32, (n, n), 1)
    delta = ((ci & (period - 1)) - (ri & (period - 1))) * sgn
    return delta < 0, delta <= 0, ri == ci


def _norm_mm_kernel(x_ref, g_ref, w_ref, o_ref):
    xn = _rms(x_ref[...], g_ref[...])
    o_ref[...] = jnp.dot(xn.astype(BF16), w_ref[...], preferred_element_type=F32)


def _norm_matmul(h, gain, w, *, tm, out_spec=None, out_shape=None):
    m, dm = h.shape
    n = w.shape[1]
    if out_spec is None:
        out_spec = pl.BlockSpec((tm, n), lambda i: (i, 0))
        out_shape = (m, n)
    return pl.pallas_call(
        _norm_mm_kernel,
        grid=(m // tm,),
        in_specs=[pl.BlockSpec((tm, dm), lambda i: (i, 0)),
                  pl.BlockSpec((1, dm), lambda i: (0, 0)),
                  pl.BlockSpec((dm, n), lambda i: (0, 0))],
        out_specs=out_spec,
        out_shape=jax.ShapeDtypeStruct(out_shape, F32),
        compiler_params=_cparams("parallel"),
        name="norm_matmul",
    )(h, gain.reshape(1, dm), w)


def _mix_mlp_kernel(h_ref, y1_ref, y2_ref, wo1_ref, wo2_ref, gm_ref, w1_ref, w2_ref, gf_ref,
                    o_ref, h1_s, xn_s, acc_s, *, final_norm):
    kf = pl.program_id(1)

    @pl.when(kf == 0)
    def _():
        h1 = h_ref[...] + _bdot(y1_ref[...], wo1_ref[...]) + _bdot(y2_ref[...], wo2_ref[...])
        h1_s[...] = h1
        xn_s[...] = _rms(h1, gm_ref[...]).astype(BF16)
        acc_s[...] = jnp.zeros_like(acc_s)

    hid = jnp.dot(xn_s[...], w1_ref[...], preferred_element_type=F32)
    hid = jnp.square(jnp.maximum(hid, 0.0))
    acc_s[...] += jnp.dot(hid.astype(BF16), w2_ref[...], preferred_element_type=F32)

    @pl.when(kf == pl.num_programs(1) - 1)
    def _():
        out = h1_s[...] + acc_s[...]
        if final_norm:
            out = _rms(out, gf_ref[...])
        o_ref[...] = out


def _mix_mlp(h, y1, y1_spec, y2, wo1, wo2, g_mlp, w1, w2, g_final, *, tm, tf, final_norm):
    m, dm = h.shape
    ff = w1.shape[1]
    k2 = y2.shape[1]
    row = lambda i, k: (i, 0)
    fixed = lambda i, k: (0, 0)
    return pl.pallas_call(
        functools.partial(_mix_mlp_kernel, final_norm=final_norm),
        grid=(m // tm, ff // tf),
        in_specs=[pl.BlockSpec((tm, dm), row),
                  y1_spec,
                  pl.BlockSpec((tm, k2), row),
                  pl.BlockSpec(wo1.shape, fixed),
                  pl.BlockSpec(wo2.shape, fixed),
                  pl.BlockSpec((1, dm), fixed),
                  pl.BlockSpec((dm, tf), lambda i, k: (0, k)),
                  pl.BlockSpec((tf, dm), lambda i, k: (k, 0)),
                  pl.BlockSpec((1, dm), fixed)],
        out_specs=pl.BlockSpec((tm, dm), row),
        out_shape=jax.ShapeDtypeStruct((m, dm), F32),
        scratch_shapes=[pltpu.VMEM((tm, dm), F32), pltpu.VMEM((tm, dm), BF16), pltpu.VMEM((tm, dm), F32)],
        compiler_params=_cparams("parallel", "arbitrary"),
        name="mix_mlp",
    )(h, y1, y2, wo1, wo2, g_mlp.reshape(1, dm), w1, w2, g_final.reshape(1, dm))


def _dwconv_silu_kernel(x_ref, xp_ref, xn_ref, w_ref, b_ref, o_ref, *, tiles_per_seq):
    x = x_ref[...]
    prev_row, next_row = _halo_rows(xp_ref, xn_ref, tiles_per_seq)
    x_prev, x_next = _shifted(x, prev_row, next_row)
    w = w_ref[...]
    y = w[0:1] * x_prev + w[1:2] * x + w[2:3] * x_next + b_ref[...]
    o_ref[...] = y * _sigmoid(y)


def _dwconv_silu(cols, col_block, width, w, b, *, seq, ts):
    n_rows = cols.shape[0]
    prev, nxt = _halo_specs(ts, width, col_block, n_rows)
    return pl.pallas_call(
        functools.partial(_dwconv_silu_kernel, tiles_per_seq=seq // ts),
        grid=(n_rows // ts,),
        in_specs=[pl.BlockSpec((ts, width), lambda i: (i, col_block)), prev, nxt,
                  pl.BlockSpec((3, width), lambda i: (0, 0)),
                  pl.BlockSpec((1, width), lambda i: (0, 0))],
        out_specs=pl.BlockSpec((ts, width), lambda i: (i, 0)),
        out_shape=jax.ShapeDtypeStruct((n_rows, width), F32),
        compiler_params=_cparams("parallel"),
        name="dwconv_silu",
    )(cols, cols, cols, w, b.reshape(1, width))


def _rw_prep_kernel(x_ref, xp_ref, xn_ref, mu_ref, w0_ref, w2_ref, a0_ref, a2_ref, g2_ref,
                    kk_ref, ka_ref, rk_ref, ones_ref,
                    r_o, k_o, v_o, kk_o, b_o, lw_o, bonus_o, g_o, *, tiles_per_seq):
    x = x_ref[...]
    prev_row, next_row = _halo_rows(xp_ref, xn_ref, tiles_per_seq)
    x_prev, x_next = _shifted(x, prev_row, next_row)
    mu = mu_ref[...]
    xs = x + mu[0:1] * (x_prev - x) + mu[1:2] * (x_next - x)
    r = xs[:, 0:RW_DIM]
    k = xs[:, RW_DIM:2 * RW_DIM]
    v = xs[:, 2 * RW_DIM:3 * RW_DIM]
    lr = xs[:, 3 * RW_DIM:3 * RW_DIM + LANES]
    g_lr = xs[:, 3 * RW_DIM + LANES:3 * RW_DIM + 2 * LANES]
    th = jnp.tanh(lr)
    for d in range(2):
        z = w0_ref[d:d + 1, :] + _bdot(th, w2_ref[d])
        lw_o[d] = -jnp.exp(-_softplus(-z) - 0.5)
    a_gate = _sigmoid(a0_ref[...] + _bdot(lr, a2_ref[...]))
    g_o[...] = _bdot(_sigmoid(g_lr), g2_ref[...])
    ones = ones_ref[...]
    kk = k * kk_ref[...]
    kk = kk * lax.rsqrt(jnp.maximum(_xdot_l(kk * kk, ones), 1e-12))
    k2 = k * (1.0 + (a_gate - 1.0) * ka_ref[...])
    r_o[...] = r
    k_o[...] = k2
    v_o[...] = v
    kk_o[...] = kk
    b_o[...] = kk * a_gate
    bonus_o[...] = _xdot_l(r * k2 * rk_ref[...], ones) * v


def _rw_chunk_kernel(r_ref, k_ref, v_ref, kk_ref, b_ref, lw_ref, y_ref, s_ref, *, chunk):
    L = chunk
    d = pl.program_id(1)

    @pl.when(pl.program_id(2) == 0)
    def _():
        s_ref[...] = jnp.zeros_like(s_ref)

    sgn = 1 - 2 * d
    _, incl_l, _ = _scan_masks(L, L, sgn)
    tri = jnp.where(incl_l, 1.0, 0.0).astype(BF16)
    lw = lw_ref[0]
    c_incl = _xdot_r(tri, lw)
    c_tot = jnp.sum(lw, axis=0, keepdims=True)
    e_in = jnp.exp(c_incl)
    e_neg = jnp.exp(-c_incl)
    e_tot = jnp.exp(c_tot)
    e_rem = e_tot * e_neg
    kk = kk_ref[...]
    bb = b_ref[...]
    k = k_ref[...]
    v = v_ref[...]
    rt = r_ref[...] * e_in
    at = -kk * jnp.exp(c_incl - lw)
    bt = bb * e_neg
    kt = k * e_neg
    bh = bb * e_rem
    kh = k * e_rem

    strict, incl, eye = _scan_masks(2 * L, L, sgn)
    first = lax.broadcasted_iota(jnp.int32, (L, LANES), 1) < RW_HEAD

    def stack(t):
        return jnp.concatenate([jnp.where(first, t, 0.0), jnp.where(first, 0.0, t)], axis=0)

    for p in range(RW_DIM // LANES):
        sl = slice(p * LANES, (p + 1) * LANES)
        at_s, rt_s, bt_s, kt_s = stack(at[:, sl]), stack(rt[:, sl]), stack(bt[:, sl]), stack(kt[:, sl])
        bh_s, kh_s, v_s = stack(bh[:, sl]), stack(kh[:, sl]), stack(v[:, sl])
        a_ab = jnp.where(strict, _bdot_nt(at_s, bt_s), 0.0)
        a_ak = jnp.where(strict, _bdot_nt(at_s, kt_s), 0.0)
        r_b = jnp.where(incl, _bdot_nt(rt_s, bt_s), 0.0)
        r_k = jnp.where(incl, _bdot_nt(rt_s, kt_s), 0.0)
        inv = jnp.where(eye, 1.0, 0.0) + a_ab
        pw = a_ab
        for _ in range(int(math.log2(L)) - 1):
            pw = _bdot(pw, pw)
            inv = inv + _bdot(inv, pw)
        s_bd = s_ref[p]
        w_s = _bdot_nt(at_s, s_bd) + _bdot(a_ak, v_s)
        u_s = _bdot(inv, w_s)
        y_s = _bdot_nt(rt_s, s_bd) + _bdot(r_b, u_s) + _bdot(r_k, v_s)
        y_ref[0, :, sl] = y_s[:L] + y_s[L:]
        s_ref[p] = s_bd * e_tot[:, sl] + _bdot_tn(u_s, bh_s) + _bdot_tn(v_s, kh_s)


def _rw_post_kernel(y_ref, bonus_ref, g_ref, lnw_ref, ones_ref, o_ref):
    y = y_ref[0] + y_ref[1]
    ones = ones_ref[...]
    yc = y - _xdot_l(y, ones) * (1.0 / RW_HEAD)
    var = _xdot_l(yc * yc, ones) * (1.0 / RW_HEAD)
    yn = yc * lax.rsqrt(var + RW_GN_EPS) * lnw_ref[...]
    o_ref[...] = (yn + bonus_ref[...]) * g_ref[...]


def _rwkv7(cols, p, *, batch, seq, ts):
    n_rows = cols.shape[0]
    dim = RW_DIM
    row = lambda i: (i, 0)
    fixed2 = lambda i: (0, 0)
    fixed3 = lambda i: (0, 0, 0)
    prev, nxt = _halo_specs(ts, RW_COLS, 0, n_rows)
    ones = jnp.asarray(np.kron(np.eye(RW_HEADS), np.ones((RW_HEAD, RW_HEAD))), BF16)
    zeros = jnp.zeros((64, dim), F32)
    w2 = jnp.concatenate([p["w2"], jnp.broadcast_to(zeros, (2, 64, dim))], axis=1).astype(BF16)
    a2 = jnp.concatenate([zeros, p["a2"]], axis=0).astype(BF16)
    vec = lambda t: t.reshape(1, dim)
    tile = jax.ShapeDtypeStruct((n_rows, dim), F32)
    r, k, v, kk, b, lw, bonus, g = pl.pallas_call(
        functools.partial(_rw_prep_kernel, tiles_per_seq=seq // ts),
        grid=(n_rows // ts,),
        in_specs=[pl.BlockSpec((ts, RW_COLS), row), prev, nxt,
                  pl.BlockSpec((2, RW_COLS), fixed2),
                  pl.BlockSpec((2, dim), fixed2),
                  pl.BlockSpec((2, LANES, dim), fixed3),
                  pl.BlockSpec((1, dim), fixed2),
                  pl.BlockSpec((LANES, dim), fixed2),
                  pl.BlockSpec((LANES, dim), fixed2),
                  pl.BlockSpec((1, dim), fixed2),
                  pl.BlockSpec((1, dim), fixed2),
                  pl.BlockSpec((1, dim), fixed2),
                  pl.BlockSpec((dim, dim), fixed2)],
        out_specs=[pl.BlockSpec((ts, dim), row)] * 5
                  + [pl.BlockSpec((2, ts, dim), lambda i: (0, i, 0))]
                  + [pl.BlockSpec((ts, dim), row)] * 2,
        out_shape=[tile] * 5 + [jax.ShapeDtypeStruct((2, n_rows, dim), F32)] + [tile] * 2,
        compiler_params=_cparams("parallel"),
        name="rw_prep",
    )(cols, cols, cols, p["mu"], p["w0"], w2, vec(p["a0"]), a2, p["g2"].astype(BF16),
      vec(p["k_k"]), vec(p["k_a"]), vec(p["r_k"]), ones)

    L = RW_CHUNK
    nc = seq // L
    rows = lambda bi, d, c: (bi * nc + c + d * (nc - 1 - 2 * c), 0)
    rows_d = lambda bi, d, c: (d, bi * nc + c + d * (nc - 1 - 2 * c), 0)
    y = pl.pallas_call(
        functools.partial(_rw_chunk_kernel, chunk=L),
        grid=(batch, 2, nc),
        in_specs=[pl.BlockSpec((L, dim), rows)] * 5 + [pl.BlockSpec((1, L, dim), rows_d)],
        out_specs=pl.BlockSpec((1, L, dim), rows_d),
        out_shape=jax.ShapeDtypeStruct((2, n_rows, dim), F32),
        scratch_shapes=[pltpu.VMEM((dim // LANES, LANES, LANES), F32)],
        compiler_params=_cparams("parallel", "parallel", "arbitrary"),
        name="rw_chunk",
    )(r, k, v, kk, b, lw)

    return pl.pallas_call(
        _rw_post_kernel,
        grid=(n_rows // ts,),
        in_specs=[pl.BlockSpec((2, ts, dim), lambda i: (0, i, 0)),
                  pl.BlockSpec((ts, dim), row), pl.BlockSpec((ts, dim), row),
                  pl.BlockSpec((1, dim), fixed2), pl.BlockSpec((dim, dim), fixed2)],
        out_specs=pl.BlockSpec((ts, dim), row),
        out_shape=tile,
        compiler_params=_cparams("parallel"),
        name="rw_post",
    )(y, bonus, g, vec(p["ln_w"]), ones)


def _mb_ssd_kernel(xs_ref, bc_ref, dtc_ref, dtr_ref, bias_c_ref, alog_c_ref, bias_r_ref, alog_r_ref,
                   e1_ref, e2_ref, y_ref, st_ref, *, chunk):
    L = chunk
    d = pl.program_id(1)

    @pl.when(pl.program_id(2) == 0)
    def _():
        st_ref[...] = jnp.zeros_like(st_ref)

    sgn = 1 - 2 * d
    _, incl, _ = _scan_masks(L, L, sgn)
    tri = jnp.where(incl, 1.0, 0.0).astype(BF16)
    ri = lax.broadcasted_iota(jnp.int32, (L, L), 0)
    ci = lax.broadcasted_iota(jnp.int32, (L, L), 1)
    tri_t = jnp.where((ri - ci) * sgn <= 0, 1.0, 0.0).astype(BF16)

    dt_c = _softplus(dtc_ref[...] + bias_c_ref[...])
    la_c = dt_c * (-jnp.exp(alog_c_ref[...]))
    cs_c = _xdot_r(tri, la_c)
    e1 = e1_ref[0]
    dt_full = _xdot_l(dt_c, e1)
    cs_full = _xdot_l(cs_c, e1)
    tot_full = jnp.where(d == 0, cs_full[L - 1:L, :], cs_full[0:1, :])
    cs_b = _xdot_l(cs_c, e2_ref[0])
    dt_r = _softplus(dtr_ref[0] + bias_r_ref[0])
    la_r = dt_r * (-jnp.exp(alog_r_ref[0]))
    cs_r = _xdot_l(la_r, tri_t)

    xdt = xs_ref[...] * dt_full
    ecs = jnp.exp(cs_full)
    xw = xdt * jnp.exp(tot_full - cs_full)
    etot = jnp.exp(tot_full)
    first = lax.broadcasted_iota(jnp.int32, (L, LANES), 1) < MB_HEADDIM
    gw = MB_HPG * MB_HEADDIM
    for g in range(MB_GROUPS):
        gs = slice(g * gw, (g + 1) * gw)
        bg = bc_ref[:, g * MB_STATE:(g + 1) * MB_STATE]
        cg = bc_ref[:, (MB_GROUPS + g) * MB_STATE:(MB_GROUPS + g + 1) * MB_STATE]
        scores = _bdot_nt(cg, bg)
        st = st_ref[:, gs]
        y_off = _bdot(cg, st) * ecs[:, gs]
        pairs = []
        for j in range(MB_HPG // 2):
            xp = xdt[:, g * gw + j * LANES:g * gw + (j + 1) * LANES]
            halves = []
            for hh in range(2):
                h = g * MB_HPG + 2 * j + hh
                expo = cs_b[:, h * L:(h + 1) * L] - cs_r[h:h + 1, :]
                m = jnp.where(incl, scores * jnp.exp(jnp.minimum(expo, 0.0)), 0.0)
                halves.append(_bdot(m, xp))
            pairs.append(jnp.where(first, halves[0], halves[1]))
        y_ref[0, :, gs] = y_off + jnp.concatenate(pairs, axis=1)
        st_ref[:, gs] = st * etot[:, gs] + _bdot_tn(bg, xw[:, gs])


def _mb_post_kernel(y_ref, xs_ref, z_ref, d_ref, nw_ref, o_ref):
    y = y_ref[0] + y_ref[1] + d_ref[...] * xs_ref[...]
    z = z_ref[...]
    y = y * (z * _sigmoid(z))
    gw = MB_DIM // MB_GROUPS
    for g in range(MB_GROUPS):
        yg = y[:, g * gw:(g + 1) * gw]
        o_ref[:, g * gw:(g + 1) * gw] = (yg * lax.rsqrt(jnp.mean(yg * yg, axis=-1, keepdims=True) + EPS)
                                          * nw_ref[:, g * gw:(g + 1) * gw])


def _mamba2(cols, p, *, batch, seq, ts):
    n_rows = cols.shape[0]
    xs = _dwconv_silu(cols, 1, MB_DIM, p["conv_w"][:, :MB_DIM], p["conv_b"][:MB_DIM], seq=seq, ts=ts)
    bc = _dwconv_silu(cols, 2 * MB_DIM // MB_BC, MB_BC, p["conv_w"][:, MB_DIM:], p["conv_b"][MB_DIM:],
                      seq=seq, ts=ts)
    L = MB_CHUNK
    nc = seq // L
    dt_col_block = (2 * MB_DIM + MB_BC) // LANES
    dt_raw = cols[:, 2 * MB_DIM + MB_BC:2 * MB_DIM + MB_BC + 2 * MB_HEADS]
    dt_rows = jnp.swapaxes(dt_raw.reshape(batch, seq, 2 * MB_HEADS), 1, 2)
    pad = jnp.zeros((LANES - 2 * MB_HEADS,), F32)
    bias_c = jnp.concatenate([p["dt_bias"].reshape(-1), pad]).reshape(1, LANES)
    alog_c = jnp.concatenate([p["A_log"].reshape(-1), pad]).reshape(1, LANES)
    bias_r = p["dt_bias"].reshape(2, MB_HEADS, 1)
    alog_r = p["A_log"].reshape(2, MB_HEADS, 1)
    e1 = np.zeros((2, LANES, MB_DIM), np.float32)
    e2 = np.zeros((2, LANES, MB_HEADS * L), np.float32)
    for d in range(2):
        for h in range(MB_HEADS):
            e1[d, MB_HEADS * d + h, h * MB_HEADDIM:(h + 1) * MB_HEADDIM] = 1.0
            e2[d, MB_HEADS * d + h, h * L:(h + 1) * L] = 1.0
    rows = lambda bi, d, c: (bi * nc + c + d * (nc - 1 - 2 * c), 0)
    fixed = lambda bi, d, c: (0, 0)
    by_dir = lambda bi, d, c: (d, 0, 0)
    y = pl.pallas_call(
        functools.partial(_mb_ssd_kernel, chunk=L),
        grid=(batch, 2, nc),
        in_specs=[pl.BlockSpec((L, MB_DIM), rows),
                  pl.BlockSpec((L, MB_BC), rows),
                  pl.BlockSpec((L, LANES), lambda bi, d, c: (bi * nc + c + d * (nc - 1 - 2 * c), dt_col_block)),
                  pl.BlockSpec((1, MB_HEADS, L), lambda bi, d, c: (bi, d, c + d * (nc - 1 - 2 * c))),
                  pl.BlockSpec((1, LANES), fixed), pl.BlockSpec((1, LANES), fixed),
                  pl.BlockSpec((1, MB_HEADS, 1), by_dir), pl.BlockSpec((1, MB_HEADS, 1), by_dir),
                  pl.BlockSpec((1, LANES, MB_DIM), by_dir),
                  pl.BlockSpec((1, LANES, MB_HEADS * L), by_dir)],
        out_specs=pl.BlockSpec((1, L, MB_DIM), lambda bi, d, c: (d, bi * nc + c + d * (nc - 1 - 2 * c), 0)),
        out_shape=jax.ShapeDtypeStruct((2, n_rows, MB_DIM), F32),
        scratch_shapes=[pltpu.VMEM((MB_STATE, MB_DIM), F32)],
        compiler_params=_cparams("parallel", "parallel", "arbitrary"),
        name="mb_ssd",
    )(xs, bc, cols, dt_rows, bias_c, alog_c, bias_r, alog_r, jnp.asarray(e1, BF16), jnp.asarray(e2, BF16))

    row = lambda i: (i, 0)
    fixed2 = lambda i: (0, 0)
    return pl.pallas_call(
        _mb_post_kernel,
        grid=(n_rows // ts,),
        in_specs=[pl.BlockSpec((2, ts, MB_DIM), lambda i: (0, i, 0)),
                  pl.BlockSpec((ts, MB_DIM), row),
                  pl.BlockSpec((ts, MB_DIM), row),
                  pl.BlockSpec((1, MB_DIM), fixed2), pl.BlockSpec((1, MB_DIM), fixed2)],
        out_specs=pl.BlockSpec((ts, MB_DIM), row),
        out_shape=jax.ShapeDtypeStruct((n_rows, MB_DIM), F32),
        compiler_params=_cparams("parallel"),
        name="mb_post",
    )(y, xs, cols, jnp.repeat(p["D"], MB_HEADDIM).reshape(1, MB_DIM), p["norm_w"].reshape(1, MB_DIM))


S5_GROUP, S5_GROUPS, S5_STATE = 16, 32, 64
S5_DIM = S5_GROUP * S5_GROUPS
S5_GPB = LANES // S5_GROUP
S5_BLOCKS = S5_GROUPS // S5_GPB
S5_HALF = S5_GPB * S5_STATE
S5_STEPS = 64


def _s5_disc_kernel(ar_ref, ai_ref, ldt_ref, bre_ref, bim_ref, abr_o, abi_o, bbr_o, bbi_o):
    dt = jnp.exp(ldt_ref[0])
    ar = jnp.minimum(ar_ref[0], -1e-4)
    ai = ai_ref[0]
    mag = jnp.exp(dt * ar)
    abr = mag * jnp.cos(dt * ai)
    abi = mag * jnp.sin(dt * ai)
    den = ar * ar + ai * ai
    fr = ((abr - 1.0) * ar + abi * ai) / den
    fi = (abi * ar - (abr - 1.0) * ai) / den
    bre = bre_ref[...]
    bim = bim_ref[...]
    abr_o[0] = abr
    abi_o[0] = abi
    bbr_o[0] = fr * bre - fi * bim
    bbi_o[0] = fr * bim + fi * bre


def _s5_scan_kernel(u_ref, bw_ref, cw_ref, lr_ref, li_ref, y_ref, x_s, st_s, *, batch, steps):
    d = pl.program_id(0)

    @pl.when(pl.program_id(1) == 0)
    def _():
        st_s[...] = jnp.zeros_like(st_s)

    u = u_ref[...].astype(BF16)
    width = 2 * S5_HALF
    for j in range(S5_BLOCKS):
        x_s[:, j * width:(j + 1) * width] = jnp.dot(u[:, j * LANES:(j + 1) * LANES], bw_ref[0, j],
                                                     preferred_element_type=F32)
    for j in range(S5_BLOCKS):
        re = slice(j * width, j * width + S5_HALF)
        im = slice(j * width + S5_HALF, (j + 1) * width)
        lam_r = jnp.broadcast_to(lr_ref[0, :, j * S5_HALF:(j + 1) * S5_HALF], (batch, S5_HALF))
        lam_i = jnp.broadcast_to(li_ref[0, :, j * S5_HALF:(j + 1) * S5_HALF], (batch, S5_HALF))

        def body(i, carry):
            xr, xi = carry
            t = jnp.where(d == 0, i, steps - 1 - i)
            rows = pl.ds(pl.multiple_of(t * batch, batch), batch)
            nr = lam_r * xr - lam_i * xi + x_s[rows, re]
            ni = lam_r * xi + lam_i * xr + x_s[rows, im]
            x_s[rows, re] = nr
            x_s[rows, im] = ni
            return nr, ni

        xr, xi = lax.fori_loop(0, steps, body, (st_s[:, re], st_s[:, im]))
        st_s[:, re] = xr
        st_s[:, im] = xi
    for j in range(S5_BLOCKS):
        y_ref[0, :, j * LANES:(j + 1) * LANES] = jnp.dot(
            x_s[:, j * width:(j + 1) * width].astype(BF16), cw_ref[0, j], preferred_element_type=F32)


def _s5_post_kernel(y_ref, u_ref, d_ref, gw_ref, gb_ref, o_ref):
    y = d_ref[...] * u_ref[...] + y_ref[0] + y_ref[1]
    y = 0.5 * y * (1.0 + jnp.tanh(math.sqrt(2.0 / math.pi) * (y + 0.044715 * (y * y * y))))
    o_ref[...] = y * _sigmoid(_bdot(y, gw_ref[...]) + gb_ref[...])


def _s5(u, p, *, batch, seq):
    n_rows = u.shape[0]
    gp = S5_GROUPS * S5_STATE
    bc = lambda t: jnp.broadcast_to(t.reshape(2, gp, 1), (2, gp, S5_GROUP))
    ldt = jnp.broadcast_to(p["log_dt"][:, :, None, None], (2, S5_GROUPS, S5_STATE, S5_GROUP)).reshape(2, gp, S5_GROUP)
    per_dir = pl.BlockSpec((1, gp, S5_GROUP), lambda d: (d, 0, 0))
    shared = pl.BlockSpec((gp, S5_GROUP), lambda d: (0, 0))
    disc = jax.ShapeDtypeStruct((2, gp, S5_GROUP), F32)
    abr, abi, bbr, bbi = pl.pallas_call(
        _s5_disc_kernel,
        grid=(2,),
        in_specs=[per_dir, per_dir, per_dir, shared, shared],
        out_specs=[per_dir] * 4,
        out_shape=[disc] * 4,
        compiler_params=_cparams("parallel"),
        name="s5_disc",
    )(bc(p["A_re"]), bc(p["A_im"]), ldt, p["B_re"].reshape(gp, S5_GROUP), p["B_im"].reshape(gp, S5_GROUP))

    eye = jnp.eye(S5_GPB, dtype=F32)
    shp = (2, S5_BLOCKS, S5_GPB, S5_STATE, S5_GROUP)
    b_blk = lambda t: jnp.einsum("djgpm,gh->djgmhp", t.reshape(shp), eye).reshape(2, S5_BLOCKS, LANES, S5_HALF)
    bw = jnp.concatenate([b_blk(bbr), b_blk(bbi)], axis=-1).astype(BF16)
    cshp = (2, S5_BLOCKS, S5_GPB, S5_GROUP, S5_STATE)
    c_blk = lambda t: jnp.einsum("djgmp,gh->djgphm", t.reshape(cshp), eye).reshape(2, S5_BLOCKS, S5_HALF, LANES)
    cw = jnp.concatenate([c_blk(p["C_re"]), c_blk(-p["C_im"])], axis=2).astype(BF16)
    lam_r = abr[:, :, 0].reshape(2, 1, gp)
    lam_i = abi[:, :, 0].reshape(2, 1, gp)

    steps = min(S5_STEPS, seq)
    nc = seq // steps
    tr = steps * batch
    width = 2 * S5_HALF
    rows = lambda d, c: (c + d * (nc - 1 - 2 * c), 0)
    by_dir4 = lambda d, c: (d, 0, 0, 0)
    by_dir3 = lambda d, c: (d, 0, 0)
    y = pl.pallas_call(
        functools.partial(_s5_scan_kernel, batch=batch, steps=steps),
        grid=(2, nc),
        in_specs=[pl.BlockSpec((tr, S5_DIM), rows),
                  pl.BlockSpec((1, S5_BLOCKS, LANES, width), by_dir4),
                  pl.BlockSpec((1, S5_BLOCKS, width, LANES), by_dir4),
                  pl.BlockSpec((1, 1, gp), by_dir3), pl.BlockSpec((1, 1, gp), by_dir3)],
        out_specs=pl.BlockSpec((1, tr, S5_DIM), lambda d, c: (d, c + d * (nc - 1 - 2 * c), 0)),
        out_shape=jax.ShapeDtypeStruct((2, n_rows, S5_DIM), F32),
        scratch_shapes=[pltpu.VMEM((tr, S5_BLOCKS * width), F32), pltpu.VMEM((batch, S5_BLOCKS * width), F32)],
        compiler_params=_cparams("parallel", "arbitrary"),
        name="s5_scan",
    )(u, bw, cw, lam_r, lam_i)

    tp = min(512, n_rows)
    row = lambda i: (i, 0)
    fixed = lambda i: (0, 0)
    return pl.pallas_call(
        _s5_post_kernel,
        grid=(n_rows // tp,),
        in_specs=[pl.BlockSpec((2, tp, S5_DIM), lambda i: (0, i, 0)), pl.BlockSpec((tp, S5_DIM), row),
                  pl.BlockSpec((1, S5_DIM), fixed), pl.BlockSpec((S5_DIM, S5_DIM), fixed),
                  pl.BlockSpec((1, S5_DIM), fixed)],
        out_specs=pl.BlockSpec((tp, S5_DIM), row),
        out_shape=jax.ShapeDtypeStruct((n_rows, S5_DIM), F32),
        compiler_params=_cparams("parallel"),
        name="s5_post",
    )(y, u, p["D"].reshape(1, S5_DIM), p["glu_w"].astype(BF16), p["glu_b"].reshape(1, S5_DIM))


ML_HEADS, ML_HEAD, ML_BLOCK = 8, 128, 4
ML_DIM = ML_HEADS * ML_HEAD
ML_COLS_PAD = 2 * ML_DIM + LANES
ML_CHUNK = 128
ML_PROJ = 256
NEG_BIG = -1e30


def _log_sigmoid(u):
    return -_softplus(-u)


def _ml_prep_kernel(x_ref, xp_ref, xn_ref, cw_ref, cb_ref, wq_ref, wk_ref, wv_ref,
                    xc_o, q_o, k_o, v_o, *, tiles_per_seq):
    x = x_ref[...]
    prev_row, next_row = _halo_rows(xp_ref, xn_ref, tiles_per_seq)
    x_prev, x_next = _shifted(x, prev_row, next_row)
    w = cw_ref[...]
    y = w[0:1] * x_prev + w[1:2] * x + w[2:3] * x_next + cb_ref[...]
    xc = y * _sigmoid(y)
    xc_o[...] = xc
    xcb = xc.astype(BF16)
    xb = x.astype(BF16)
    for j in range(ML_DIM // ML_PROJ):
        sl = slice(j * ML_PROJ, (j + 1) * ML_PROJ)
        q_o[:, sl] = jnp.dot(xcb[:, sl], wq_ref[j], preferred_element_type=F32).astype(BF16)
        k_o[:, sl] = (jnp.dot(xcb[:, sl], wk_ref[j], preferred_element_type=F32) * (ML_HEAD ** -0.5)).astype(BF16)
        v_o[:, sl] = jnp.dot(xb[:, sl], wv_ref[j], preferred_element_type=F32).astype(BF16)


def _ml_chunk_kernel(q_ref, k_ref, v_ref, gc_ref, gi_ref, gf_ref, bias_c_ref, ib_ref, fb_ref, eb_ref, ew_ref,
                     h_ref, c_s, m_s, *, chunk):
    L = chunk
    d = pl.program_id(1)

    @pl.when(pl.program_id(2) == 0)
    def _():
        c_s[...] = jnp.zeros_like(c_s)
        m_s[...] = jnp.zeros_like(m_s)

    sgn = 1 - 2 * d
    _, incl, _ = _scan_masks(L, L, sgn)
    tri = jnp.where(incl, 1.0, 0.0).astype(BF16)
    ri = lax.broadcasted_iota(jnp.int32, (L, L), 0)
    ci = lax.broadcasted_iota(jnp.int32, (L, L), 1)
    tri_t = jnp.where((ri - ci) * sgn <= 0, 1.0, 0.0).astype(BF16)

    lane = lax.broadcasted_iota(jnp.int32, (L, LANES), 1)
    gpre = gc_ref[...] + bias_c_ref[...]
    gcol = jnp.where(lane < 2 * ML_HEADS, gpre, _log_sigmoid(gpre))
    cs_c = _xdot_r(tri, gcol)
    tot_c = jnp.sum(gcol, axis=0, keepdims=True)
    b_bc = _xdot_l(cs_c, eb_ref[0])
    lw_bc = _xdot_l(jnp.where(lane < 2 * ML_HEADS, gcol, tot_c - cs_c), ew_ref[0])
    li_r = gi_ref[0] + ib_ref[0]
    lf_r = _log_sigmoid(gf_ref[0] + fb_ref[0])
    b_r = _xdot_l(lf_r, tri_t)
    b_last = jnp.where(d == 0, b_r[:, L - 1:L], b_r[:, 0:1])
    lw_r = b_last - b_r + li_r
    lw_max = jnp.max(lw_r, axis=-1, keepdims=True)

    ones_col = jnp.where(lane == 0, 1.0, 0.0).astype(BF16)
    for h in range(ML_HEADS):
        sl = slice(h * ML_HEAD, (h + 1) * ML_HEAD)
        q = q_ref[:, sl]
        k = k_ref[:, sl]
        v_ext = jnp.concatenate([v_ref[:, sl], ones_col], axis=1)
        m_prev = m_s[h:h + 1, 0:1]
        b_t = b_bc[:, h * L:(h + 1) * L]
        log_d = jnp.where(incl, b_t - b_r[h:h + 1, :] + li_r[h:h + 1, :], NEG_BIG)
        inter = b_t[:, 0:1] + m_prev
        m_t = jnp.maximum(jnp.max(log_d, axis=-1, keepdims=True), inter)
        s = _bdot_nt(q, k) * jnp.exp(log_d - m_t)
        w_in = jnp.exp(inter - m_t)
        c_ext = c_s[h]
        nd = _bdot(s, v_ext) + w_in * _bdot(q, c_ext)
        den = nd[:, ML_HEAD:ML_HEAD + 1]
        h_ref[0, :, sl] = nd[:, :ML_HEAD] / jnp.maximum(jnp.abs(den), jnp.exp(-m_t))
        bl = b_last[h:h + 1, :]
        m_new = jnp.maximum(bl + m_prev, lw_max[h:h + 1, :])
        wk = k.astype(F32) * jnp.exp(lw_bc[:, sl] - m_new)
        c_s[h] = jnp.exp(bl + m_prev - m_new) * c_ext + _bdot_tn(wk, v_ext)
        m_s[h:h + 1, :] = jnp.broadcast_to(m_new, (1, LANES))


def _ml_post_kernel(h_ref, o_ref_in, xc_ref, nw_ref, skip_ref, out_ref):
    hsum = h_ref[0] + h_ref[1]
    for h in range(ML_HEADS):
        sl = slice(h * ML_HEAD, (h + 1) * ML_HEAD)
        x = hsum[:, sl]
        xc = x - jnp.mean(x, axis=-1, keepdims=True)
        hn = xc * lax.rsqrt(jnp.mean(xc * xc, axis=-1, keepdims=True) + EPS) * nw_ref[:, sl]
        out_ref[:, sl] = _sigmoid(o_ref_in[:, sl]) * hn + skip_ref[:, sl] * xc_ref[:, sl]


def _mlstm(cols, p, *, batch, seq, ts):
    n_rows = cols.shape[0]
    dim = ML_DIM
    row = lambda i: (i, 0)
    fixed2 = lambda i: (0, 0)
    fixed3 = lambda i: (0, 0, 0)
    prev, nxt = _halo_specs(ts, dim, 0, n_rows)
    nblk = dim // ML_PROJ
    per = ML_PROJ // ML_BLOCK
    eye = jnp.eye(per, dtype=F32)
    blockdiag = lambda w: jnp.einsum("bjcd,jk->bjckd", w.reshape(nblk, per, ML_BLOCK, ML_BLOCK),
                                     eye).reshape(nblk, ML_PROJ, ML_PROJ).astype(BF16)
    wspec = pl.BlockSpec((nblk, ML_PROJ, ML_PROJ), fixed3)
    xc, q, k, v = pl.pallas_call(
        functools.partial(_ml_prep_kernel, tiles_per_seq=seq // ts),
        grid=(n_rows // ts,),
        in_specs=[pl.BlockSpec((ts, dim), row), prev, nxt,
                  pl.BlockSpec((3, dim), fixed2), pl.BlockSpec((1, dim), fixed2), wspec, wspec, wspec],
        out_specs=[pl.BlockSpec((ts, dim), row)] * 4,
        out_shape=[jax.ShapeDtypeStruct((n_rows, dim), F32)] + [jax.ShapeDtypeStruct((n_rows, dim), BF16)] * 3,
        compiler_params=_cparams("parallel"),
        name="ml_prep",
    )(cols, cols, cols, p["conv_w"], p["conv_b"].reshape(1, dim),
      blockdiag(p["wq"]), blockdiag(p["wk"]), blockdiag(p["wv"]))

    L = min(ML_CHUNK, seq)
    nc = seq // L
    ng = 4 * ML_HEADS
    gate_col_block = 2 * dim // LANES
    g_rows = jnp.swapaxes(cols[:, 2 * dim:2 * dim + ng].reshape(batch, seq, ng), 1, 2)
    pad = jnp.zeros((LANES - ng,), F32)
    bias_c = jnp.concatenate([p["i_b"].reshape(-1), p["f_b"].reshape(-1), pad]).reshape(1, LANES)
    ib = p["i_b"].reshape(2, ML_HEADS, 1)
    fb = p["f_b"].reshape(2, ML_HEADS, 1)
    eb = np.zeros((2, LANES, ML_HEADS * L), np.float32)
    ew = np.zeros((2, LANES, dim), np.float32)
    for d in range(2):
        for h in range(ML_HEADS):
            eb[d, 2 * ML_HEADS + ML_HEADS * d + h, h * L:(h + 1) * L] = 1.0
            ew[d, 2 * ML_HEADS + ML_HEADS * d + h, h * ML_HEAD:(h + 1) * ML_HEAD] = 1.0
            ew[d, ML_HEADS * d + h, h * ML_HEAD:(h + 1) * ML_HEAD] = 1.0
    chunk_of = lambda d, c: c + d * (nc - 1 - 2 * c)
    rows = lambda bi, d, c: (bi * nc + chunk_of(d, c), 0)
    fixed = lambda bi, d, c: (0, 0)
    by_dir = lambda bi, d, c: (d, 0, 0)
    hdirs = pl.pallas_call(
        functools.partial(_ml_chunk_kernel, chunk=L),
        grid=(batch, 2, nc),
        in_specs=[pl.BlockSpec((L, dim), rows)] * 3
                 + [pl.BlockSpec((L, LANES), lambda bi, d, c: (bi * nc + chunk_of(d, c), gate_col_block)),
                    pl.BlockSpec((1, ML_HEADS, L), lambda bi, d, c: (bi, d, chunk_of(d, c))),
                    pl.BlockSpec((1, ML_HEADS, L), lambda bi, d, c: (bi, 2 + d, chunk_of(d, c))),
                    pl.BlockSpec((1, LANES), fixed),
                    pl.BlockSpec((1, ML_HEADS, 1), by_dir), pl.BlockSpec((1, ML_HEADS, 1), by_dir),
                    pl.BlockSpec((1, LANES, ML_HEADS * L), by_dir), pl.BlockSpec((1, LANES, dim), by_dir)],
        out_specs=pl.BlockSpec((1, L, dim), lambda bi, d, c: (d, bi * nc + chunk_of(d, c), 0)),
        out_shape=jax.ShapeDtypeStruct((2, n_rows, dim), F32),
        scratch_shapes=[pltpu.VMEM((ML_HEADS, ML_HEAD, 2 * ML_HEAD), F32), pltpu.VMEM((ML_HEADS, LANES), F32)],
        compiler_params=_cparams("parallel", "parallel", "arbitrary"),
        name="ml_chunk",
    )(q, k, v, cols, g_rows, g_rows, bias_c, ib, fb, jnp.asarray(eb, BF16), jnp.asarray(ew, BF16))

    return pl.pallas_call(
        _ml_post_kernel,
        grid=(n_rows // ts,),
        in_specs=[pl.BlockSpec((2, ts, dim), lambda i: (0, i, 0)),
                  pl.BlockSpec((ts, dim), lambda i: (i, 1)),
                  pl.BlockSpec((ts, dim), row),
                  pl.BlockSpec((1, dim), fixed2), pl.BlockSpec((1, dim), fixed2)],
        out_specs=pl.BlockSpec((ts, dim), row),
        out_shape=jax.ShapeDtypeStruct((n_rows, dim), F32),
        compiler_params=_cparams("parallel"),
        name="ml_post",
    )(hdirs, cols, xc, p["norm_w"].reshape(1, dim), p["skip"].reshape(1, dim))


ROW_TILE = 512
MIXER_TILE = 256
FF_TILE = 1024


def _pad_cols(w, n):
    return jnp.pad(w, ((0, 0), (0, n - w.shape[1])))


def kernel(x, norm_mix, norm_mlp, norm_final, mlp_w1, mlp_w2, ab_w_in, ab_w_out, rw_mu, rw_w0, rw_w2, rw_a0, rw_a2, rw_g2, rw_k_k, rw_k_a, rw_r_k, rw_ln_w, mb_conv_w, mb_conv_b, mb_dt_bias, mb_A_log, mb_D, mb_norm_w, cd_w_in, cd_w_out, s5_A_re, s5_A_im, s5_log_dt, s5_B_re, s5_B_im, s5_C_re, s5_C_im, s5_D, s5_glu_w, s5_glu_b, ml_conv_w, ml_conv_b, ml_wq, ml_wk, ml_wv, ml_i_b, ml_f_b, ml_norm_w, ml_skip):
    batch, seq, dm = x.shape
    n_rows = batch * seq
    tm = min(ROW_TILE, seq)
    ts = min(MIXER_TILE, seq)
    tiles_per_seq = seq // tm
    h = x.reshape(n_rows, dm)
    depth = norm_mix.shape[0]
    for layer in range(depth):
        i = layer // 2
        last = layer == depth - 1
        if layer % 2 == 0:
            w_in = ab_w_in[i]
            rw_cols = _norm_matmul(h, norm_mix[layer], w_in[:, :RW_COLS].astype(BF16), tm=tm)
            mb_cols = _norm_matmul(h, norm_mix[layer], _pad_cols(w_in[:, RW_COLS:], MB_COLS_PAD).astype(BF16), tm=tm)
            y1 = _rwkv7(rw_cols, dict(mu=rw_mu[i], w0=rw_w0[i], w2=rw_w2[i], a0=rw_a0[i], a2=rw_a2[i], g2=rw_g2[i],
                                      k_k=rw_k_k[i], k_a=rw_k_a[i], r_k=rw_r_k[i].reshape(-1), ln_w=rw_ln_w[i]),
                        batch=batch, seq=seq, ts=ts)
            y2 = _mamba2(mb_cols, dict(conv_w=mb_conv_w[i], conv_b=mb_conv_b[i], dt_bias=mb_dt_bias[i],
                                       A_log=mb_A_log[i], D=mb_D[i], norm_w=mb_norm_w[i]),
                         batch=batch, seq=seq, ts=ts)
            y1_spec = pl.BlockSpec((tm, RW_DIM), lambda r, k: (r, 0))
            w_out, k1 = ab_w_out[i], RW_DIM
        else:
            w_in = cd_w_in[i]
            tm_spec = pl.BlockSpec((tm, S5_DIM), lambda r: (r % tiles_per_seq, r // tiles_per_seq))
            s5_cols = _norm_matmul(h, norm_mix[layer], w_in[:, :S5_DIM].astype(BF16), tm=tm,
                                   out_spec=tm_spec, out_shape=(seq, batch * S5_DIM))
            ml_cols = _norm_matmul(h, norm_mix[layer], _pad_cols(w_in[:, S5_DIM:], ML_COLS_PAD).astype(BF16), tm=tm)
            y1 = _s5(s5_cols.reshape(seq * batch, S5_DIM),
                     dict(A_re=s5_A_re[i], A_im=s5_A_im[i], log_dt=s5_log_dt[i], B_re=s5_B_re[i], B_im=s5_B_im[i],
                          C_re=s5_C_re[i], C_im=s5_C_im[i], D=s5_D[i], glu_w=s5_glu_w[i], glu_b=s5_glu_b[i]),
                     batch=batch, seq=seq).reshape(seq, batch * S5_DIM)
            y2 = _mlstm(ml_cols, dict(conv_w=ml_conv_w[i], conv_b=ml_conv_b[i], wq=ml_wq[i], wk=ml_wk[i], wv=ml_wv[i],
                                      i_b=ml_i_b[i], f_b=ml_f_b[i], norm_w=ml_norm_w[i], skip=ml_skip[i]),
                        batch=batch, seq=seq, ts=ts)
            y1_spec = pl.BlockSpec((tm, S5_DIM), lambda r, k: (r % tiles_per_seq, r // tiles_per_seq))
            w_out, k1 = cd_w_out[i], S5_DIM
        h = _mix_mlp(h, y1, y1_spec, y2, w_out[:k1].astype(BF16), w_out[k1:].astype(BF16), norm_mlp[layer],
                     mlp_w1[layer].astype(BF16), mlp_w2[layer].astype(BF16), norm_final,
                     tm=tm, tf=FF_TILE, final_norm=last)
    return h.reshape(batch, seq, dm)
```

```python
import functools
import math

import jax
import jax.numpy as jnp
import numpy as np
from jax import lax
from jax.experimental import pallas as pl
from jax.experimental.pallas import tpu as pltpu

F32 = jnp.float32
BF16 = jnp.bfloat16

EPS = 1e-5
LANES = 128
SUBLANES = 8
VMEM_LIMIT_BYTES = 48 * 1024 * 1024

D_MODEL = 1024
D_FF = 4 * D_MODEL

RW_HEADS, RW_HEAD = 8, 64
RW_DIM = RW_HEADS * RW_HEAD
RW_LR = 64 + 64 + 128
RW_COLS = 3 * RW_DIM + RW_LR
RW_GN_EPS = 64e-5
RW_CHUNK = 64

MB_HEADS, MB_HEADDIM, MB_GROUPS, MB_STATE = 16, 64, 2, 128
MB_HPG = MB_HEADS // MB_GROUPS
MB_DIM = MB_HEADS * MB_HEADDIM
MB_BC = 2 * MB_GROUPS * MB_STATE
MB_COLS_PAD = 2 * MB_DIM + MB_BC + LANES
MB_CHUNK = 128


def _cparams(*sem):
    return pltpu.CompilerParams(dimension_semantics=sem, vmem_limit_bytes=VMEM_LIMIT_BYTES)


def _bdot(a, b):
    return jnp.dot(a.astype(BF16), b.astype(BF16), preferred_element_type=F32)


def _bdot_nt(a, b):
    return lax.dot_general(a.astype(BF16), b.astype(BF16), (((1,), (1,)), ((), ())),
                           preferred_element_type=F32)


def _bdot_tn(a, b):
    return lax.dot_general(a.astype(BF16), b.astype(BF16), (((0,), (0,)), ((), ())),
                           preferred_element_type=F32)


def _split3(x):
    hi = x.astype(BF16)
    r = x - hi.astype(F32)
    mid = r.astype(BF16)
    lo = (r - mid.astype(F32)).astype(BF16)
    return hi, mid, lo


def _xdot_l(x, m):
    hi, mid, lo = _split3(x)
    dot = functools.partial(jnp.dot, preferred_element_type=F32)
    return dot(lo, m) + dot(mid, m) + dot(hi, m)


def _xdot_r(m, x):
    hi, mid, lo = _split3(x)
    dot = functools.partial(jnp.dot, preferred_element_type=F32)
    return dot(m, lo) + dot(m, mid) + dot(m, hi)


def _softplus(u):
    return jnp.maximum(u, 0.0) + jnp.log1p(jnp.exp(-jnp.abs(u)))


def _sigmoid(u):
    return 1.0 / (1.0 + jnp.exp(-u))


def _rms(x, gain):
    return x * lax.rsqrt(jnp.mean(x * x, axis=-1, keepdims=True) + EPS) * gain


def _shifted(x, prev_row, next_row):
    n = x.shape[0]
    rows = lax.broadcasted_iota(jnp.int32, x.shape, 0)
    x_prev = jnp.where(rows == 0, prev_row, pltpu.roll(x, 1, 0))
    x_next = jnp.where(rows == n - 1, next_row, pltpu.roll(x, n - 1, 0))
    return x_prev, x_next


def _halo_rows(xp_ref, xn_ref, tiles_per_seq):
    si = pl.program_id(0) % tiles_per_seq
    prev_row = jnp.where(si == 0, 0.0, xp_ref[SUBLANES - 1:SUBLANES, :])
    next_row = jnp.where(si == tiles_per_seq - 1, 0.0, xn_ref[0:1, :])
    return prev_row, next_row


def _halo_specs(ts, width, col_block, n_rows):
    per = ts // SUBLANES
    last = n_rows // SUBLANES - 1
    prev = pl.BlockSpec((SUBLANES, width), lambda i: (jnp.maximum(i * per - 1, 0), col_block))
    nxt = pl.BlockSpec((SUBLANES, width), lambda i: (jnp.minimum((i + 1) * per, last), col_block))
    return prev, nxt


def _scan_masks(n, period, sgn):
    ri = lax.broadcasted_iota(jnp.int32, (n, n), 0)
    ci = lax.broadcasted_iota(jnp.int32, (n, n), 1)
    delta = ((ci & (period - 1)) - (ri & (period - 1))) * sgn
    return delta < 0, delta <= 0, ri == ci


def _norm_mm_kernel(x_ref, g_ref, w_ref, o_ref):
    xn = _rms(x_ref[...], g_ref[...])
    o_ref[...] = jnp.dot(xn.astype(BF16), w_ref[...], preferred_element_type=F32)


def _norm_matmul(h, gain, w, *, tm, out_spec=None, out_shape=None):
    m, dm = h.shape
    n = w.shape[1]
    if out_spec is None:
        out_spec = pl.BlockSpec((tm, n), lambda i: (i, 0))
        out_shape = (m, n)
    return pl.pallas_call(
        _norm_mm_kernel,
        grid=(m // tm,),
        in_specs=[pl.BlockSpec((tm, dm), lambda i: (i, 0)),
                  pl.BlockSpec((1, dm), lambda i: (0, 0)),
                  pl.BlockSpec((dm, n), lambda i: (0, 0))],
        out_specs=out_spec,
        out_shape=jax.ShapeDtypeStruct(out_shape, F32),
        compiler_params=_cparams("parallel"),
        name="norm_matmul",
    )(h, gain.reshape(1, dm), w)


def _mix_mlp_kernel(h_ref, y1_ref, y2_ref, wo1_ref, wo2_ref, gm_ref, w1_ref, w2_ref, gf_ref,
                    o_ref, h1_s, xn_s, acc_s, *, final_norm):
    kf = pl.program_id(1)

    @pl.when(kf == 0)
    def _():
        h1 = h_ref[...] + _bdot(y1_ref[...], wo1_ref[...]) + _bdot(y2_ref[...], wo2_ref[...])
        h1_s[...] = h1
        xn_s[...] = _rms(h1, gm_ref[...]).astype(BF16)
        acc_s[...] = jnp.zeros_like(acc_s)

    hid = jnp.dot(xn_s[...], w1_ref[...], preferred_element_type=F32)
    hid = jnp.square(jnp.maximum(hid, 0.0))
    acc_s[...] += jnp.dot(hid.astype(BF16), w2_ref[...], preferred_element_type=F32)

    @pl.when(kf == pl.num_programs(1) - 1)
    def _():
        out = h1_s[...] + acc_s[...]
        if final_norm:
            out = _rms(out, gf_ref[...])
        o_ref[...] = out


def _mix_mlp(h, y1, y1_spec, y2, wo1, wo2, g_mlp, w1, w2, g_final, *, tm, tf, final_norm):
    m, dm = h.shape
    ff = w1.shape[1]
    k2 = y2.shape[1]
    row = lambda i, k: (i, 0)
    fixed = lambda i, k: (0, 0)
    return pl.pallas_call(
        functools.partial(_mix_mlp_kernel, final_norm=final_norm),
        grid=(m // tm, ff // tf),
        in_specs=[pl.BlockSpec((tm, dm), row),
                  y1_spec,
                  pl.BlockSpec((tm, k2), row),
                  pl.BlockSpec(wo1.shape, fixed),
                  pl.BlockSpec(wo2.shape, fixed),
                  pl.BlockSpec((1, dm), fixed),
                  pl.BlockSpec((dm, tf), lambda i, k: (0, k)),
                  pl.BlockSpec((tf, dm), lambda i, k: (k, 0)),
                  pl.BlockSpec((1, dm), fixed)],
        out_specs=pl.BlockSpec((tm, dm), row),
        out_shape=jax.ShapeDtypeStruct((m, dm), F32),
        scratch_shapes=[pltpu.VMEM((tm, dm), F32), pltpu.VMEM((tm, dm), BF16), pltpu.VMEM((tm, dm), F32)],
        compiler_params=_cparams("parallel", "arbitrary"),
        name="mix_mlp",
    )(h, y1, y2, wo1, wo2, g_mlp.reshape(1, dm), w1, w2, g_final.reshape(1, dm))


def _dwconv_silu_kernel(x_ref, xp_ref, xn_ref, w_ref, b_ref, o_ref, *, tiles_per_seq):
    x = x_ref[...]
    prev_row, next_row = _halo_rows(xp_ref, xn_ref, tiles_per_seq)
    x_prev, x_next = _shifted(x, prev_row, next_row)
    w = w_ref[...]
    y = w[0:1] * x_prev + w[1:2] * x + w[2:3] * x_next + b_ref[...]
    o_ref[...] = y * _sigmoid(y)


def _dwconv_silu(cols, col_block, width, w, b, *, seq, ts):
    n_rows = cols.shape[0]
    prev, nxt = _halo_specs(ts, width, col_block, n_rows)
    return pl.pallas_call(
        functools.partial(_dwconv_silu_kernel, tiles_per_seq=seq // ts),
        grid=(n_rows // ts,),
        in_specs=[pl.BlockSpec((ts, width), lambda i: (i, col_block)), prev, nxt,
                  pl.BlockSpec((3, width), lambda i: (0, 0)),
                  pl.BlockSpec((1, width), lambda i: (0, 0))],
        out_specs=pl.BlockSpec((ts, width), lambda i: (i, 0)),
        out_shape=jax.ShapeDtypeStruct((n_rows, width), F32),
        compiler_params=_cparams("parallel"),
        name="dwconv_silu",
    )(cols, cols, cols, w, b.reshape(1, width))


def _rw_prep_kernel(x_ref, xp_ref, xn_ref, mu_ref, w0_ref, w2_ref, a0_ref, a2_ref, g2_ref,
                    kk_ref, ka_ref, rk_ref, ones_ref,
                    r_o, k_o, v_o, kk_o, b_o, lw_o, bonus_o, g_o, *, tiles_per_seq):
    x = x_ref[...]
    prev_row, next_row = _halo_rows(xp_ref, xn_ref, tiles_per_seq)
    x_prev, x_next = _shifted(x, prev_row, next_row)
    mu = mu_ref[...]
    xs = x + mu[0:1] * (x_prev - x) + mu[1:2] * (x_next - x)
    r = xs[:, 0:RW_DIM]
    k = xs[:, RW_DIM:2 * RW_DIM]
    v = xs[:, 2 * RW_DIM:3 * RW_DIM]
    lr = xs[:, 3 * RW_DIM:3 * RW_DIM + LANES]
    g_lr = xs[:, 3 * RW_DIM + LANES:3 * RW_DIM + 2 * LANES]
    th = jnp.tanh(lr)
    for d in range(2):
        z = w0_ref[d:d + 1, :] + _bdot(th, w2_ref[d])
        lw_o[d] = -jnp.exp(-_softplus(-z) - 0.5)
    a_gate = _sigmoid(a0_ref[...] + _bdot(lr, a2_ref[...]))
    g_o[...] = _bdot(_sigmoid(g_lr), g2_ref[...])
    ones = ones_ref[...]
    kk = k * kk_ref[...]
    kk = kk * lax.rsqrt(jnp.maximum(_xdot_l(kk * kk, ones), 1e-12))
    k2 = k * (1.0 + (a_gate - 1.0) * ka_ref[...])
    r_o[...] = r
    k_o[...] = k2
    v_o[...] = v
    kk_o[...] = kk
    b_o[...] = kk * a_gate
    bonus_o[...] = _xdot_l(r * k2 * rk_ref[...], ones) * v


def _rw_chunk_kernel(*refs, chunk):
    L = chunk
    ins = (refs[0:6], refs[6:12])
    y_refs = refs[12:14]
    s_ref = refs[14]

    @pl.when(pl.program_id(1) == 0)
    def _():
        s_ref[...] = jnp.zeros_like(s_ref)

    first = lax.broadcasted_iota(jnp.int32, (L, LANES), 1) < RW_HEAD

    def stack(t):
        return jnp.concatenate([jnp.where(first, t, 0.0), jnp.where(first, 0.0, t)], axis=0).astype(BF16)

    chains = []
    for dd in range(2):
        r_ref, k_ref, v_ref, kk_ref, b_ref, lw_ref = ins[dd]
        sgn = 1 - 2 * dd
        _, incl_l, _ = _scan_masks(L, L, sgn)
        tri = jnp.where(incl_l, 1.0, 0.0).astype(BF16)
        lw = lw_ref[0]
        c_incl = _xdot_r(tri, lw)
        c_tot = jnp.sum(lw, axis=0, keepdims=True)
        e_in = jnp.exp(c_incl)
        e_neg = jnp.exp(-c_incl)
        e_tot = jnp.exp(c_tot)
        e_rem = e_tot * e_neg
        kk = kk_ref[...]
        bb = b_ref[...]
        k = k_ref[...]
        v = v_ref[...]
        rt = r_ref[...] * e_in
        at = -kk * jnp.exp(c_incl - lw)
        bt = bb * e_neg
        kt = k * e_neg
        bh = bb * e_rem
        kh = k * e_rem
        strict, incl, eye = _scan_masks(2 * L, L, sgn)
        for p in range(RW_DIM // LANES):
            sl = slice(p * LANES, (p + 1) * LANES)
            chains.append(dict(
                dd=dd, p=p, sl=sl, strict=strict, incl=incl, eye=eye, e_tot=e_tot[:, sl],
                at=stack(at[:, sl]), rt=stack(rt[:, sl]), bt=stack(bt[:, sl]), kt=stack(kt[:, sl]),
                bh=stack(bh[:, sl]), kh=stack(kh[:, sl]), v=stack(v[:, sl]),
                s=s_ref[dd, p]))

    for c in chains:
        c["a_ab"] = jnp.where(c["strict"], _bdot_nt(c["at"], c["bt"]), 0.0)
        c["a_ak"] = jnp.where(c["strict"], _bdot_nt(c["at"], c["kt"]), 0.0)
        c["r_b"] = jnp.where(c["incl"], _bdot_nt(c["rt"], c["bt"]), 0.0)
        c["r_k"] = jnp.where(c["incl"], _bdot_nt(c["rt"], c["kt"]), 0.0)
        c["w"] = _bdot_nt(c["at"], c["s"])
        c["y"] = _bdot_nt(c["rt"], c["s"])
    n_sq = int(math.log2(L)) - 1
    for c in chains:
        c["inv"] = jnp.where(c["eye"], 1.0, 0.0) + c["a_ab"]
        c["pw"] = _bdot(c["a_ab"], c["a_ab"])
        c["w"] = c["w"] + _bdot(c["a_ak"], c["v"])
        c["y"] = c["y"] + _bdot(c["r_k"], c["v"])
        c["s_new"] = c["s"] * c["e_tot"] + _bdot_tn(c["v"], c["kh"])
    for i in range(n_sq):
        for c in chains:
            c["inv"] = c["inv"] + _bdot(c["inv"], c["pw"])
            if i + 1 < n_sq:
                c["pw"] = _bdot(c["pw"], c["pw"])
    for c in chains:
        c["u"] = _bdot(c["inv"], c["w"])
    for c in chains:
        y_s = c["y"] + _bdot(c["r_b"], c["u"])
        y_refs[c["dd"]][:, c["sl"]] = y_s[:L] + y_s[L:]
        s_ref[c["dd"], c["p"]] = c["s_new"] + _bdot_tn(c["u"], c["bh"])


def _rw_post_kernel(yf_ref, yb_ref, bonus_ref, g_ref, lnw_ref, ones_ref, o_ref):
    y = yf_ref[...] + yb_ref[...]
    ones = ones_ref[...]
    yc = y - _xdot_l(y, ones) * (1.0 / RW_HEAD)
    var = _xdot_l(yc * yc, ones) * (1.0 / RW_HEAD)
    yn = yc * lax.rsqrt(var + RW_GN_EPS) * lnw_ref[...]
    o_ref[...] = (yn + bonus_ref[...]) * g_ref[...]


def _rwkv7(cols, p, *, batch, seq, ts):
    n_rows = cols.shape[0]
    dim = RW_DIM
    row = lambda i: (i, 0)
    fixed2 = lambda i: (0, 0)
    fixed3 = lambda i: (0, 0, 0)
    prev, nxt = _halo_specs(ts, RW_COLS, 0, n_rows)
    ones = jnp.asarray(np.kron(np.eye(RW_HEADS), np.ones((RW_HEAD, RW_HEAD))), BF16)
    zeros = jnp.zeros((64, dim), F32)
    w2 = jnp.concatenate([p["w2"], jnp.broadcast_to(zeros, (2, 64, dim))], axis=1).astype(BF16)
    a2 = jnp.concatenate([zeros, p["a2"]], axis=0).astype(BF16)
    vec = lambda t: t.reshape(1, dim)
    tile = jax.ShapeDtypeStruct((n_rows, dim), F32)
    r, k, v, kk, b, lw, bonus, g = pl.pallas_call(
        functools.partial(_rw_prep_kernel, tiles_per_seq=seq // ts),
        grid=(n_rows // ts,),
        in_specs=[pl.BlockSpec((ts, RW_COLS), row), prev, nxt,
                  pl.BlockSpec((2, RW_COLS), fixed2),
                  pl.BlockSpec((2, dim), fixed2),
                  pl.BlockSpec((2, LANES, dim), fixed3),
                  pl.BlockSpec((1, dim), fixed2),
                  pl.BlockSpec((LANES, dim), fixed2),
                  pl.BlockSpec((LANES, dim), fixed2),
                  pl.BlockSpec((1, dim), fixed2),
                  pl.BlockSpec((1, dim), fixed2),
                  pl.BlockSpec((1, dim), fixed2),
                  pl.BlockSpec((dim, dim), fixed2)],
        out_specs=[pl.BlockSpec((ts, dim), row)] * 5
                  + [pl.BlockSpec((2, ts, dim), lambda i: (0, i, 0))]
                  + [pl.BlockSpec((ts, dim), row)] * 2,
        out_shape=[tile] * 5 + [jax.ShapeDtypeStruct((2, n_rows, dim), F32)] + [tile] * 2,
        compiler_params=_cparams("parallel"),
        name="rw_prep",
    )(cols, cols, cols, p["mu"], p["w0"], w2, vec(p["a0"]), a2, p["g2"].astype(BF16),
      vec(p["k_k"]), vec(p["k_a"]), vec(p["r_k"]), ones)

    L = RW_CHUNK
    nc = seq // L
    fwd = lambda bi, c: (bi * nc + c, 0)
    bwd = lambda bi, c: (bi * nc + nc - 1 - c, 0)
    y_fwd, y_bwd = pl.pallas_call(
        functools.partial(_rw_chunk_kernel, chunk=L),
        grid=(batch, nc),
        in_specs=[pl.BlockSpec((L, dim), fwd)] * 5 + [pl.BlockSpec((1, L, dim), lambda bi, c: (0, bi * nc + c, 0))]
                 + [pl.BlockSpec((L, dim), bwd)] * 5
                 + [pl.BlockSpec((1, L, dim), lambda bi, c: (1, bi * nc + nc - 1 - c, 0))],
        out_specs=[pl.BlockSpec((L, dim), fwd), pl.BlockSpec((L, dim), bwd)],
        out_shape=[tile, tile],
        scratch_shapes=[pltpu.VMEM((2, dim // LANES, LANES, LANES), F32)],
        compiler_params=_cparams("parallel", "arbitrary"),
        name="rw_chunk",
    )(r, k, v, kk, b, lw, r, k, v, kk, b, lw)

    return pl.pallas_call(
        _rw_post_kernel,
        grid=(n_rows // ts,),
        in_specs=[pl.BlockSpec((ts, dim), row), pl.BlockSpec((ts, dim), row),
                  pl.BlockSpec((ts, dim), row), pl.BlockSpec((ts, dim), row),
                  pl.BlockSpec((1, dim), fixed2), pl.BlockSpec((dim, dim), fixed2)],
        out_specs=pl.BlockSpec((ts, dim), row),
        out_shape=tile,
        compiler_params=_cparams("parallel"),
        name="rw_post",
    )(y_fwd, y_bwd, bonus, g, vec(p["ln_w"]), ones)


def _mb_ssd_kernel(*refs, chunk):
    L = chunk
    ins = (refs[0:5], refs[5:10])
    bias_c_ref, alog_c_ref, bias_r_ref, alog_r_ref, e1_ref = refs[10:15]
    y_refs = refs[15:17]
    st_ref = refs[17]

    @pl.when(pl.program_id(1) == 0)
    def _():
        st_ref[...] = jnp.zeros_like(st_ref)

    ri = lax.broadcasted_iota(jnp.int32, (L, L), 0)
    ci = lax.broadcasted_iota(jnp.int32, (L, L), 1)
    first = lax.broadcasted_iota(jnp.int32, (L, LANES), 1) < MB_HEADDIM
    gw = MB_HPG * MB_HEADDIM
    groups = []
    for dd in range(2):
        xs_ref, bc_ref, bt_ref, dtc_ref, dtr_ref = ins[dd]
        sgn = 1 - 2 * dd
        incl = (ci - ri) * sgn <= 0
        tri = jnp.where(incl, 1.0, 0.0).astype(BF16)
        tri_t = jnp.where((ri - ci) * sgn <= 0, 1.0, 0.0).astype(BF16)
        last = L - 1 if dd == 0 else 0
        dt_c = _softplus(dtc_ref[...] + bias_c_ref[...])
        cs_c = _xdot_r(tri, dt_c * (-jnp.exp(alog_c_ref[...])))
        cols2 = jnp.concatenate([jnp.exp(cs_c), dt_c * jnp.exp(cs_c[last:last + 1, :] - cs_c)], axis=0)
        hi = cols2.astype(BF16)
        mid = (cols2 - hi.astype(F32)).astype(BF16)
        full2 = jnp.dot(jnp.concatenate([hi, mid], axis=1), e1_ref[dd], preferred_element_type=F32)
        ecs = full2[:L]
        xw = (xs_ref[...] * full2[L:]).astype(BF16)
        etot = ecs[last:last + 1, :]
        xb = xs_ref[...].astype(BF16)
        dt_r = _softplus(dtr_ref[0] + bias_r_ref[dd])
        cs_r = _xdot_l(dt_r * (-jnp.exp(alog_r_ref[dd])), tri_t)
        for g in range(MB_GROUPS):
            gs = slice(g * gw, (g + 1) * gw)
            groups.append(dict(
                dd=dd, g=g, gs=gs, incl=incl, ecs=ecs[:, gs], etot=etot[:, gs], xw=xw[:, gs], xb=xb[:, gs],
                cs_c=cs_c, cs_r=cs_r, dt_r=dt_r,
                bg=bc_ref[:, g * MB_STATE:(g + 1) * MB_STATE].astype(BF16),
                cg=bc_ref[:, (MB_GROUPS + g) * MB_STATE:(MB_GROUPS + g + 1) * MB_STATE].astype(BF16),
                bt=bt_ref[0, g * MB_STATE:(g + 1) * MB_STATE, :],
                st=st_ref[dd, :, gs]))

    for c in groups:
        c["scores"] = _bdot_nt(c["cg"], c["bg"])
        c["y_off"] = _bdot(c["cg"], c["st"]) * c["ecs"]
        st_ref[c["dd"], :, c["gs"]] = c["st"] * c["etot"] + _bdot(c["bt"], c["xw"])
    for c in groups:
        pairs = []
        for j in range(MB_HPG // 2):
            xp = c["xb"][:, j * LANES:(j + 1) * LANES]
            halves = []
            for hh in range(2):
                h = c["g"] * MB_HPG + 2 * j + hh
                lane = MB_HEADS * c["dd"] + h
                b_t = jnp.broadcast_to(c["cs_c"][:, lane:lane + 1], (L, L))
                expo = jnp.minimum(b_t - c["cs_r"][h:h + 1, :], 0.0)
                m = jnp.where(c["incl"], c["scores"] * jnp.exp(expo) * c["dt_r"][h:h + 1, :], 0.0)
                halves.append(_bdot(m, xp))
            pairs.append(jnp.where(first, halves[0], halves[1]))
        y_refs[c["dd"]][:, c["gs"]] = c["y_off"] + jnp.concatenate(pairs, axis=1)


def _mb_post_kernel(yf_ref, yb_ref, xs_ref, z_ref, d_ref, nw_ref, o_ref):
    y = yf_ref[...] + yb_ref[...] + d_ref[...] * xs_ref[...]
    z = z_ref[...]
    y = y * (z * _sigmoid(z))
    gw = MB_DIM // MB_GROUPS
    for g in range(MB_GROUPS):
        yg = y[:, g * gw:(g + 1) * gw]
        o_ref[:, g * gw:(g + 1) * gw] = (yg * lax.rsqrt(jnp.mean(yg * yg, axis=-1, keepdims=True) + EPS)
                                          * nw_ref[:, g * gw:(g + 1) * gw])


def _mamba2(cols, p, *, batch, seq, ts):
    n_rows = cols.shape[0]
    xs = _dwconv_silu(cols, 1, MB_DIM, p["conv_w"][:, :MB_DIM], p["conv_b"][:MB_DIM], seq=seq, ts=ts)
    bc = _dwconv_silu(cols, 2 * MB_DIM // MB_BC, MB_BC, p["conv_w"][:, MB_DIM:], p["conv_b"][MB_DIM:],
                      seq=seq, ts=ts)
    L = MB_CHUNK
    nc = seq // L
    dt_col_block = (2 * MB_DIM + MB_BC) // LANES
    dt_raw = cols[:, 2 * MB_DIM + MB_BC:2 * MB_DIM + MB_BC + 2 * MB_HEADS]
    dt_rows = jnp.swapaxes(dt_raw.reshape(batch, seq, 2 * MB_HEADS), 1, 2)
    pad = jnp.zeros((LANES - 2 * MB_HEADS,), F32)
    bias_c = jnp.concatenate([p["dt_bias"].reshape(-1), pad]).reshape(1, LANES)
    alog_c = jnp.concatenate([p["A_log"].reshape(-1), pad]).reshape(1, LANES)
    bias_r = p["dt_bias"].reshape(2, MB_HEADS, 1)
    alog_r = p["A_log"].reshape(2, MB_HEADS, 1)
    e1 = np.zeros((2, 2 * LANES, MB_DIM), np.float32)
    for d in range(2):
        for h in range(MB_HEADS):
            e1[d, MB_HEADS * d + h, h * MB_HEADDIM:(h + 1) * MB_HEADDIM] = 1.0
            e1[d, LANES + MB_HEADS * d + h, h * MB_HEADDIM:(h + 1) * MB_HEADDIM] = 1.0
    b_t = jnp.swapaxes(bc[:, :MB_GROUPS * MB_STATE].reshape(batch, seq, MB_GROUPS * MB_STATE), 1, 2).astype(BF16)
    fixed = lambda bi, c: (0, 0)
    fixed3c = lambda bi, c: (0, 0, 0)

    def dir_specs(d):
        chunk_of = (lambda c: c) if d == 0 else (lambda c: nc - 1 - c)
        rows = lambda bi, c: (bi * nc + chunk_of(c), 0)
        return ([pl.BlockSpec((L, MB_DIM), rows),
                 pl.BlockSpec((L, MB_BC), rows),
                 pl.BlockSpec((1, MB_GROUPS * MB_STATE, L), lambda bi, c: (bi, 0, chunk_of(c))),
                 pl.BlockSpec((L, LANES), lambda bi, c: (bi * nc + chunk_of(c), dt_col_block)),
                 pl.BlockSpec((1, MB_HEADS, L), lambda bi, c: (bi, d, chunk_of(c)))],
                pl.BlockSpec((L, MB_DIM), rows))

    (in_f, out_f), (in_b, out_b) = dir_specs(0), dir_specs(1)
    y_tile = jax.ShapeDtypeStruct((n_rows, MB_DIM), F32)
    y_fwd, y_bwd = pl.pallas_call(
        functools.partial(_mb_ssd_kernel, chunk=L),
        grid=(batch, nc),
        in_specs=in_f + in_b + [pl.BlockSpec((1, LANES), fixed), pl.BlockSpec((1, LANES), fixed),
                                pl.BlockSpec((2, MB_HEADS, 1), fixed3c), pl.BlockSpec((2, MB_HEADS, 1), fixed3c),
                                pl.BlockSpec((2, 2 * LANES, MB_DIM), fixed3c)],
        out_specs=[out_f, out_b],
        out_shape=[y_tile, y_tile],
        scratch_shapes=[pltpu.VMEM((2, MB_STATE, MB_DIM), F32)],
        compiler_params=_cparams("parallel", "arbitrary"),
        name="mb_ssd",
    )(xs, bc, b_t, cols, dt_rows, xs, bc, b_t, cols, dt_rows, bias_c, alog_c, bias_r, alog_r, jnp.asarray(e1, BF16))

    row = lambda i: (i, 0)
    fixed2 = lambda i: (0, 0)
    return pl.pallas_call(
        _mb_post_kernel,
        grid=(n_rows // ts,),
        in_specs=[pl.BlockSpec((ts, MB_DIM), row), pl.BlockSpec((ts, MB_DIM), row),
                  pl.BlockSpec((ts, MB_DIM), row),
                  pl.BlockSpec((ts, MB_DIM), row),
                  pl.BlockSpec((1, MB_DIM), fixed2), pl.BlockSpec((1, MB_DIM), fixed2)],
        out_specs=pl.BlockSpec((ts, MB_DIM), row),
        out_shape=jax.ShapeDtypeStruct((n_rows, MB_DIM), F32),
        compiler_params=_cparams("parallel"),
        name="mb_post",
    )(y_fwd, y_bwd, xs, cols, jnp.repeat(p["D"], MB_HEADDIM).reshape(1, MB_DIM), p["norm_w"].reshape(1, MB_DIM))


S5_GROUP, S5_GROUPS, S5_STATE = 16, 32, 64
S5_DIM = S5_GROUP * S5_GROUPS
S5_GPB = LANES // S5_GROUP
S5_BLOCKS = S5_GROUPS // S5_GPB
S5_HALF = S5_GPB * S5_STATE
S5_STEPS = 64


def _s5_disc_kernel(ar_ref, ai_ref, ldt_ref, bre_ref, bim_ref, abr_o, abi_o, bbr_o, bbi_o):
    dt = jnp.exp(ldt_ref[0])
    ar = jnp.minimum(ar_ref[0], -1e-4)
    ai = ai_ref[0]
    mag = jnp.exp(dt * ar)
    abr = mag * jnp.cos(dt * ai)
    abi = mag * jnp.sin(dt * ai)
    den = ar * ar + ai * ai
    fr = ((abr - 1.0) * ar + abi * ai) / den
    fi = (abi * ar - (abr - 1.0) * ai) / den
    bre = bre_ref[...]
    bim = bim_ref[...]
    abr_o[0] = abr
    abi_o[0] = abi
    bbr_o[0] = fr * bre - fi * bim
    bbi_o[0] = fr * bim + fi * bre


def _s5_scan_kernel(u_ref, bw_ref, cw_ref, lr_ref, li_ref, y_ref, x_s, st_s, *, batch, steps):
    d = pl.program_id(0)

    @pl.when(pl.program_id(1) == 0)
    def _():
        st_s[...] = jnp.zeros_like(st_s)

    u = u_ref[...].astype(BF16)
    width = 2 * S5_HALF
    for j in range(S5_BLOCKS):
        x_s[:, j * width:(j + 1) * width] = jnp.dot(u[:, j * LANES:(j + 1) * LANES], bw_ref[0, j],
                                                     preferred_element_type=F32)
    for j in range(S5_BLOCKS):
        re = slice(j * width, j * width + S5_HALF)
        im = slice(j * width + S5_HALF, (j + 1) * width)
        lam_r = jnp.broadcast_to(lr_ref[0, :, j * S5_HALF:(j + 1) * S5_HALF], (batch, S5_HALF))
        lam_i = jnp.broadcast_to(li_ref[0, :, j * S5_HALF:(j + 1) * S5_HALF], (batch, S5_HALF))

        def body(i, carry):
            xr, xi = carry
            t = jnp.where(d == 0, i, steps - 1 - i)
            rows = pl.ds(pl.multiple_of(t * batch, batch), batch)
            nr = lam_r * xr - lam_i * xi + x_s[rows, re]
            ni = lam_r * xi + lam_i * xr + x_s[rows, im]
            x_s[rows, re] = nr
            x_s[rows, im] = ni
            return nr, ni

        xr, xi = lax.fori_loop(0, steps, body, (st_s[:, re], st_s[:, im]))
        st_s[:, re] = xr
        st_s[:, im] = xi
    for j in range(S5_BLOCKS):
        y_ref[0, :, j * LANES:(j + 1) * LANES] = jnp.dot(
            x_s[:, j * width:(j + 1) * width].astype(BF16), cw_ref[0, j], preferred_element_type=F32)


def _s5_post_kernel(y_ref, u_ref, d_ref, gw_ref, gb_ref, o_ref):
    y = d_ref[...] * u_ref[...] + y_ref[0] + y_ref[1]
    y = 0.5 * y * (1.0 + jnp.tanh(math.sqrt(2.0 / math.pi) * (y + 0.044715 * (y * y * y))))
    o_ref[...] = y * _sigmoid(_bdot(y, gw_ref[...]) + gb_ref[...])


def _s5(u, p, *, batch, seq):
    n_rows = u.shape[0]
    gp = S5_GROUPS * S5_STATE
    bc = lambda t: jnp.broadcast_to(t.reshape(2, gp, 1), (2, gp, S5_GROUP))
    ldt = jnp.broadcast_to(p["log_dt"][:, :, None, None], (2, S5_GROUPS, S5_STATE, S5_GROUP)).reshape(2, gp, S5_GROUP)
    per_dir = pl.BlockSpec((1, gp, S5_GROUP), lambda d: (d, 0, 0))
    shared = pl.BlockSpec((gp, S5_GROUP), lambda d: (0, 0))
    disc = jax.ShapeDtypeStruct((2, gp, S5_GROUP), F32)
    abr, abi, bbr, bbi = pl.pallas_call(
        _s5_disc_kernel,
        grid=(2,),
        in_specs=[per_dir, per_dir, per_dir, shared, shared],
        out_specs=[per_dir] * 4,
        out_shape=[disc] * 4,
        compiler_params=_cparams("parallel"),
        name="s5_disc",
    )(bc(p["A_re"]), bc(p["A_im"]), ldt, p["B_re"].reshape(gp, S5_GROUP), p["B_im"].reshape(gp, S5_GROUP))

    eye = jnp.eye(S5_GPB, dtype=F32)
    shp = (2, S5_BLOCKS, S5_GPB, S5_STATE, S5_GROUP)
    b_blk = lambda t: jnp.einsum("djgpm,gh->djgmhp", t.reshape(shp), eye).reshape(2, S5_BLOCKS, LANES, S5_HALF)
    bw = jnp.concatenate([b_blk(bbr), b_blk(bbi)], axis=-1).astype(BF16)
    cshp = (2, S5_BLOCKS, S5_GPB, S5_GROUP, S5_STATE)
    c_blk = lambda t: jnp.einsum("djgmp,gh->djgphm", t.reshape(cshp), eye).reshape(2, S5_BLOCKS, S5_HALF, LANES)
    cw = jnp.concatenate([c_blk(p["C_re"]), c_blk(-p["C_im"])], axis=2).astype(BF16)
    lam_r = abr[:, :, 0].reshape(2, 1, gp)
    lam_i = abi[:, :, 0].reshape(2, 1, gp)

    steps = min(S5_STEPS, seq)
    nc = seq // steps
    tr = steps * batch
    width = 2 * S5_HALF
    rows = lambda d, c: (c + d * (nc - 1 - 2 * c), 0)
    by_dir4 = lambda d, c: (d, 0, 0, 0)
    by_dir3 = lambda d, c: (d, 0, 0)
    y = pl.pallas_call(
        functools.partial(_s5_scan_kernel, batch=batch, steps=steps),
        grid=(2, nc),
        in_specs=[pl.BlockSpec((tr, S5_DIM), rows),
                  pl.BlockSpec((1, S5_BLOCKS, LANES, width), by_dir4),
                  pl.BlockSpec((1, S5_BLOCKS, width, LANES), by_dir4),
                  pl.BlockSpec((1, 1, gp), by_dir3), pl.BlockSpec((1, 1, gp), by_dir3)],
        out_specs=pl.BlockSpec((1, tr, S5_DIM), lambda d, c: (d, c + d * (nc - 1 - 2 * c), 0)),
        out_shape=jax.ShapeDtypeStruct((2, n_rows, S5_DIM), F32),
        scratch_shapes=[pltpu.VMEM((tr, S5_BLOCKS * width), F32), pltpu.VMEM((batch, S5_BLOCKS * width), F32)],
        compiler_params=_cparams("parallel", "arbitrary"),
        name="s5_scan",
    )(u, bw, cw, lam_r, lam_i)

    tp = min(512, n_rows)
    row = lambda i: (i, 0)
    fixed = lambda i: (0, 0)
    return pl.pallas_call(
        _s5_post_kernel,
        grid=(n_rows // tp,),
        in_specs=[pl.BlockSpec((2, tp, S5_DIM), lambda i: (0, i, 0)), pl.BlockSpec((tp, S5_DIM), row),
                  pl.BlockSpec((1, S5_DIM), fixed), pl.BlockSpec((S5_DIM, S5_DIM), fixed),
                  pl.BlockSpec((1, S5_DIM), fixed)],
        out_specs=pl.BlockSpec((tp, S5_DIM), row),
        out_shape=jax.ShapeDtypeStruct((n_rows, S5_DIM), F32),
        compiler_params=_cparams("parallel"),
        name="s5_post",
    )(y, u, p["D"].reshape(1, S5_DIM), p["glu_w"].astype(BF16), p["glu_b"].reshape(1, S5_DIM))


ML_HEADS, ML_HEAD, ML_BLOCK = 8, 128, 4
ML_DIM = ML_HEADS * ML_HEAD
ML_COLS_PAD = 2 * ML_DIM + LANES
ML_CHUNK = 128
ML_PROJ = 256
NEG_BIG = -1e30


def _log_sigmoid(u):
    return -_softplus(-u)


def _ml_prep_kernel(x_ref, xp_ref, xn_ref, cw_ref, cb_ref, wq_ref, wk_ref, wv_ref,
                    xc_o, q_o, k_o, v_o, *, tiles_per_seq):
    x = x_ref[...]
    prev_row, next_row = _halo_rows(xp_ref, xn_ref, tiles_per_seq)
    x_prev, x_next = _shifted(x, prev_row, next_row)
    w = cw_ref[...]
    y = w[0:1] * x_prev + w[1:2] * x + w[2:3] * x_next + cb_ref[...]
    xc = y * _sigmoid(y)
    xc_o[...] = xc
    xcb = xc.astype(BF16)
    xb = x.astype(BF16)
    for j in range(ML_DIM // ML_PROJ):
        sl = slice(j * ML_PROJ, (j + 1) * ML_PROJ)
        q_o[:, sl] = jnp.dot(xcb[:, sl], wq_ref[j], preferred_element_type=F32).astype(BF16)
        k_o[:, sl] = (jnp.dot(xcb[:, sl], wk_ref[j], preferred_element_type=F32) * (ML_HEAD ** -0.5)).astype(BF16)
        v_o[:, sl] = jnp.dot(xb[:, sl], wv_ref[j], preferred_element_type=F32).astype(BF16)


def _ml_chunk_kernel(*refs, chunk):
    L = chunk
    ins = (refs[0:7], refs[7:14])
    bias_c_ref, ib_ref, fb_ref = refs[14:17]
    h_refs = refs[17:19]
    c_s, m_s = refs[19:21]

    @pl.when(pl.program_id(1) == 0)
    def _():
        c_s[...] = jnp.zeros_like(c_s)
        m_s[...] = jnp.zeros_like(m_s)

    lane = lax.broadcasted_iota(jnp.int32, (L, LANES), 1)
    ones_tile = jnp.ones((L, ML_HEAD), BF16)
    ri = lax.broadcasted_iota(jnp.int32, (L, L), 0)
    ci = lax.broadcasted_iota(jnp.int32, (L, L), 1)
    chains = []
    for dd in range(2):
        q_ref, k_ref, kt_ref, v_ref, gc_ref, gi_ref, gf_ref = ins[dd]
        sgn = 1 - 2 * dd
        incl = (ci - ri) * sgn <= 0
        tri = jnp.where(incl, 1.0, 0.0).astype(BF16)
        tri_t = jnp.where((ri - ci) * sgn <= 0, 1.0, 0.0).astype(BF16)
        gpre = gc_ref[...] + bias_c_ref[...]
        gcol = jnp.where(lane < 2 * ML_HEADS, gpre, _log_sigmoid(gpre))
        cs_c = _xdot_r(tri, gcol)
        li_r = gi_ref[0] + ib_ref[dd]
        lf_r = _log_sigmoid(gf_ref[0] + fb_ref[dd])
        b_r = _xdot_l(lf_r, tri_t)
        b_last = b_r[:, L - 1:L] if dd == 0 else b_r[:, 0:1]
        lw_r = b_last - b_r + li_r
        lw_max = jnp.max(lw_r, axis=-1, keepdims=True)
        for h in range(ML_HEADS):
            sl = slice(h * ML_HEAD, (h + 1) * ML_HEAD)
            jf = 2 * ML_HEADS + ML_HEADS * dd + h
            chains.append(dict(
                dd=dd, h=h, sl=sl, incl=incl, q=q_ref[:, sl], k=k_ref[:, sl], kt=kt_ref[0, sl, :],
                v_ext=jnp.concatenate([v_ref[:, sl], ones_tile], axis=1),
                b_t=jnp.broadcast_to(cs_c[:, jf:jf + 1], (L, LANES)),
                b_row=b_r[h:h + 1, :], li_row=li_r[h:h + 1, :], lw_row=lw_r[h:h + 1, :],
                bl=b_last[h:h + 1, :], lw_max=lw_max[h:h + 1, :],
                m_prev=m_s[dd, h:h + 1, 0:1], c_ext=c_s[dd, h]))

    for c in chains:
        log_d = jnp.where(c["incl"], c["b_t"] - c["b_row"] + c["li_row"], NEG_BIG)
        inter = c["b_t"] + c["m_prev"]
        m_t = jnp.maximum(jnp.broadcast_to(jnp.max(log_d, axis=-1, keepdims=True), (L, LANES)), inter)
        c["m_t"] = m_t
        c["dmat"] = jnp.exp(log_d - m_t)
        c["w_in"] = jnp.exp(inter - m_t)
        c["qk"] = _bdot_nt(c["q"], c["k"])
        c["qc"] = _bdot(c["q"], c["c_ext"])
        m_new = jnp.maximum(c["bl"] + c["m_prev"], c["lw_max"])
        wkt = c["kt"].astype(F32) * jnp.exp(c["lw_row"] - m_new)
        c["c_new"] = jnp.exp(c["bl"] + c["m_prev"] - m_new) * c["c_ext"] + _bdot(wkt, c["v_ext"])
        c["m_new"] = m_new
    for c in chains:
        w_in2 = jnp.concatenate([c["w_in"], c["w_in"]], axis=1)
        nd = _bdot(c["qk"] * c["dmat"], c["v_ext"]) + w_in2 * c["qc"]
        den = nd[:, ML_HEAD:]
        h_refs[c["dd"]][:, c["sl"]] = nd[:, :ML_HEAD] / jnp.maximum(jnp.abs(den), jnp.exp(-c["m_t"]))
        c_s[c["dd"], c["h"]] = c["c_new"]
        m_s[c["dd"], c["h"]:c["h"] + 1, :] = jnp.broadcast_to(c["m_new"], (1, LANES))


def _ml_post_kernel(hf_ref, hb_ref, o_ref_in, xc_ref, nw_ref, skip_ref, out_ref):
    hsum = hf_ref[...] + hb_ref[...]
    for h in range(ML_HEADS):
        sl = slice(h * ML_HEAD, (h + 1) * ML_HEAD)
        x = hsum[:, sl]
        xc = x - jnp.mean(x, axis=-1, keepdims=True)
        hn = xc * lax.rsqrt(jnp.mean(xc * xc, axis=-1, keepdims=True) + EPS) * nw_ref[:, sl]
        out_ref[:, sl] = _sigmoid(o_ref_in[:, sl]) * hn + skip_ref[:, sl] * xc_ref[:, sl]


def _mlstm(cols, p, *, batch, seq, ts):
    n_rows = cols.shape[0]
    dim = ML_DIM
    row = lambda i: (i, 0)
    fixed2 = lambda i: (0, 0)
    fixed3 = lambda i: (0, 0, 0)
    prev, nxt = _halo_specs(ts, dim, 0, n_rows)
    nblk = dim // ML_PROJ
    per = ML_PROJ // ML_BLOCK
    eye = jnp.eye(per, dtype=F32)
    blockdiag = lambda w: jnp.einsum("bjcd,jk->bjckd", w.reshape(nblk, per, ML_BLOCK, ML_BLOCK),
                                     eye).reshape(nblk, ML_PROJ, ML_PROJ).astype(BF16)
    wspec = pl.BlockSpec((nblk, ML_PROJ, ML_PROJ), fixed3)
    xc, q, k, v = pl.pallas_call(
        functools.partial(_ml_prep_kernel, tiles_per_seq=seq // ts),
        grid=(n_rows // ts,),
        in_specs=[pl.BlockSpec((ts, dim), row), prev, nxt,
                  pl.BlockSpec((3, dim), fixed2), pl.BlockSpec((1, dim), fixed2), wspec, wspec, wspec],
        out_specs=[pl.BlockSpec((ts, dim), row)] * 4,
        out_shape=[jax.ShapeDtypeStruct((n_rows, dim), F32)] + [jax.ShapeDtypeStruct((n_rows, dim), BF16)] * 3,
        compiler_params=_cparams("parallel"),
        name="ml_prep",
    )(cols, cols, cols, p["conv_w"], p["conv_b"].reshape(1, dim),
      blockdiag(p["wq"]), blockdiag(p["wk"]), blockdiag(p["wv"]))

    L = min(ML_CHUNK, seq)
    nc = seq // L
    ng = 4 * ML_HEADS
    gate_col_block = 2 * dim // LANES
    g_rows = jnp.swapaxes(cols[:, 2 * dim:2 * dim + ng].reshape(batch, seq, ng), 1, 2)
    k_t = jnp.swapaxes(k.reshape(batch, seq, dim), 1, 2)
    pad = jnp.zeros((LANES - ng,), F32)
    bias_c = jnp.concatenate([p["i_b"].reshape(-1), p["f_b"].reshape(-1), pad]).reshape(1, LANES)
    ib = p["i_b"].reshape(2, ML_HEADS, 1)
    fb = p["f_b"].reshape(2, ML_HEADS, 1)
    fixed = lambda bi, c: (0, 0)
    fixed3c = lambda bi, c: (0, 0, 0)

    def dir_specs(d):
        chunk_of = (lambda c: c) if d == 0 else (lambda c: nc - 1 - c)
        rows = lambda bi, c: (bi * nc + chunk_of(c), 0)
        return ([pl.BlockSpec((L, dim), rows)] * 2
                + [pl.BlockSpec((1, dim, L), lambda bi, c: (bi, 0, chunk_of(c))),
                   pl.BlockSpec((L, dim), rows),
                   pl.BlockSpec((L, LANES), lambda bi, c: (bi * nc + chunk_of(c), gate_col_block)),
                   pl.BlockSpec((1, ML_HEADS, L), lambda bi, c: (bi, d, chunk_of(c))),
                   pl.BlockSpec((1, ML_HEADS, L), lambda bi, c: (bi, 2 + d, chunk_of(c)))],
                pl.BlockSpec((L, dim), rows))

    (in_f, out_f), (in_b, out_b) = dir_specs(0), dir_specs(1)
    h_tile = jax.ShapeDtypeStruct((n_rows, dim), F32)
    h_fwd, h_bwd = pl.pallas_call(
        functools.partial(_ml_chunk_kernel, chunk=L),
        grid=(batch, nc),
        in_specs=in_f + in_b + [pl.BlockSpec((1, LANES), fixed),
                                pl.BlockSpec((2, ML_HEADS, 1), fixed3c), pl.BlockSpec((2, ML_HEADS, 1), fixed3c)],
        out_specs=[out_f, out_b],
        out_shape=[h_tile, h_tile],
        scratch_shapes=[pltpu.VMEM((2, ML_HEADS, ML_HEAD, 2 * ML_HEAD), F32),
                        pltpu.VMEM((2, ML_HEADS, LANES), F32)],
        compiler_params=_cparams("parallel", "arbitrary"),
        name="ml_chunk",
    )(q, k, k_t, v, cols, g_rows, g_rows, q, k, k_t, v, cols, g_rows, g_rows, bias_c, ib, fb)

    return pl.pallas_call(
        _ml_post_kernel,
        grid=(n_rows // ts,),
        in_specs=[pl.BlockSpec((ts, dim), row), pl.BlockSpec((ts, dim), row),
                  pl.BlockSpec((ts, dim), lambda i: (i, 1)),
                  pl.BlockSpec((ts, dim), row),
                  pl.BlockSpec((1, dim), fixed2), pl.BlockSpec((1, dim), fixed2)],
        out_specs=pl.BlockSpec((ts, dim), row),
        out_shape=jax.ShapeDtypeStruct((n_rows, dim), F32),
        compiler_params=_cparams("parallel"),
        name="ml_post",
    )(h_fwd, h_bwd, cols, xc, p["norm_w"].reshape(1, dim), p["skip"].reshape(1, dim))


ROW_TILE = 512
MIXER_TILE = 256
FF_TILE = 1024


def _pad_cols(w, n):
    return jnp.pad(w, ((0, 0), (0, n - w.shape[1])))


def kernel(x, norm_mix, norm_mlp, norm_final, mlp_w1, mlp_w2, ab_w_in, ab_w_out, rw_mu, rw_w0, rw_w2, rw_a0, rw_a2, rw_g2, rw_k_k, rw_k_a, rw_r_k, rw_ln_w, mb_conv_w, mb_conv_b, mb_dt_bias, mb_A_log, mb_D, mb_norm_w, cd_w_in, cd_w_out, s5_A_re, s5_A_im, s5_log_dt, s5_B_re, s5_B_im, s5_C_re, s5_C_im, s5_D, s5_glu_w, s5_glu_b, ml_conv_w, ml_conv_b, ml_wq, ml_wk, ml_wv, ml_i_b, ml_f_b, ml_norm_w, ml_skip):
    batch, seq, dm = x.shape
    n_rows = batch * seq
    tm = min(ROW_TILE, seq)
    ts = min(MIXER_TILE, seq)
    tiles_per_seq = seq // tm
    h = x.reshape(n_rows, dm)
    depth = norm_mix.shape[0]
    for layer in range(depth):
        i = layer // 2
        last = layer == depth - 1
        if layer % 2 == 0:
            w_in = ab_w_in[i]
            rw_cols = _norm_matmul(h, norm_mix[layer], w_in[:, :RW_COLS].astype(BF16), tm=tm)
            mb_cols = _norm_matmul(h, norm_mix[layer], _pad_cols(w_in[:, RW_COLS:], MB_COLS_PAD).astype(BF16), tm=tm)
            y1 = _rwkv7(rw_cols, dict(mu=rw_mu[i], w0=rw_w0[i], w2=rw_w2[i], a0=rw_a0[i], a2=rw_a2[i], g2=rw_g2[i],
                                      k_k=rw_k_k[i], k_a=rw_k_a[i], r_k=rw_r_k[i].reshape(-1), ln_w=rw_ln_w[i]),
                        batch=batch, seq=seq, ts=ts)
            y2 = _mamba2(mb_cols, dict(conv_w=mb_conv_w[i], conv_b=mb_conv_b[i], dt_bias=mb_dt_bias[i],
                                       A_log=mb_A_log[i], D=mb_D[i], norm_w=mb_norm_w[i]),
                         batch=batch, seq=seq, ts=ts)
            y1_spec = pl.BlockSpec((tm, RW_DIM), lambda r, k: (r, 0))
            w_out, k1 = ab_w_out[i], RW_DIM
        else:
            w_in = cd_w_in[i]
            tm_spec = pl.BlockSpec((tm, S5_DIM), lambda r: (r % tiles_per_seq, r // tiles_per_seq))
            s5_cols = _norm_matmul(h, norm_mix[layer], w_in[:, :S5_DIM].astype(BF16), tm=tm,
                                   out_spec=tm_spec, out_shape=(seq, batch * S5_DIM))
            ml_cols = _norm_matmul(h, norm_mix[layer], _pad_cols(w_in[:, S5_DIM:], ML_COLS_PAD).astype(BF16), tm=tm)
            y1 = _s5(s5_cols.reshape(seq * batch, S5_DIM),
                     dict(A_re=s5_A_re[i], A_im=s5_A_im[i], log_dt=s5_log_dt[i], B_re=s5_B_re[i], B_im=s5_B_im[i],
                          C_re=s5_C_re[i], C_im=s5_C_im[i], D=s5_D[i], glu_w=s5_glu_w[i], glu_b=s5_glu_b[i]),
                     batch=batch, seq=seq).reshape(seq, batch * S5_DIM)
            y2 = _mlstm(ml_cols, dict(conv_w=ml_conv_w[i], conv_b=ml_conv_b[i], wq=ml_wq[i], wk=ml_wk[i], wv=ml_wv[i],
                                      i_b=ml_i_b[i], f_b=ml_f_b[i], norm_w=ml_norm_w[i], skip=ml_skip[i]),
                        batch=batch, seq=seq, ts=ts)
            y1_spec = pl.BlockSpec((tm, S5_DIM), lambda r, k: (r % tiles_per_seq, r // tiles_per_seq))
            w_out, k1 = cd_w_out[i], S5_DIM
        h = _mix_mlp(h, y1, y1_spec, y2, w_out[:k1].astype(BF16), w_out[k1:].astype(BF16), norm_mlp[layer],
                     mlp_w1[layer].astype(BF16), mlp_w2[layer].astype(BF16), norm_final,
                     tm=tm, tf=FF_TILE, final_norm=last)
    return h.reshape(batch, seq, dm)
```

```python
import functools
import math

import jax
import jax.numpy as jnp
import numpy as np
from jax import lax
from jax.experimental import pallas as pl
from jax.experimental.pallas import tpu as pltpu

F32 = jnp.float32
BF16 = jnp.bfloat16

EPS = 1e-5
LANES = 128
SUBLANES = 8
VMEM_LIMIT_BYTES = 48 * 1024 * 1024

D_MODEL = 1024
D_FF = 4 * D_MODEL

RW_HEADS, RW_HEAD = 8, 64
RW_DIM = RW_HEADS * RW_HEAD
RW_LR = 64 + 64 + 128
RW_COLS = 3 * RW_DIM + RW_LR
RW_GN_EPS = 64e-5
RW_CHUNK = 64

MB_HEADS, MB_HEADDIM, MB_GROUPS, MB_STATE = 16, 64, 2, 128
MB_HPG = MB_HEADS // MB_GROUPS
MB_DIM = MB_HEADS * MB_HEADDIM
MB_BC = 2 * MB_GROUPS * MB_STATE
MB_COLS_PAD = 2 * MB_DIM + MB_BC + LANES
MB_CHUNK = 128


def _cparams(*sem):
    return pltpu.CompilerParams(dimension_semantics=sem, vmem_limit_bytes=VMEM_LIMIT_BYTES)


def _bdot(a, b):
    return jnp.dot(a.astype(BF16), b.astype(BF16), preferred_element_type=F32)


def _bdot_nt(a, b):
    return lax.dot_general(a.astype(BF16), b.astype(BF16), (((1,), (1,)), ((), ())),
                           preferred_element_type=F32)


def _bdot_tn(a, b):
    return lax.dot_general(a.astype(BF16), b.astype(BF16), (((0,), (0,)), ((), ())),
                           preferred_element_type=F32)


def _split3(x):
    hi = x.astype(BF16)
    r = x - hi.astype(F32)
    mid = r.astype(BF16)
    lo = (r - mid.astype(F32)).astype(BF16)
    return hi, mid, lo


def _xdot_l(x, m):
    hi, mid, lo = _split3(x)
    dot = functools.partial(jnp.dot, preferred_element_type=F32)
    return dot(lo, m) + dot(mid, m) + dot(hi, m)


def _xdot_r(m, x):
    hi, mid, lo = _split3(x)
    dot = functools.partial(jnp.dot, preferred_element_type=F32)
    return dot(m, lo) + dot(m, mid) + dot(m, hi)


def _softplus(u):
    return jnp.maximum(u, 0.0) + jnp.log1p(jnp.exp(-jnp.abs(u)))


def _sigmoid(u):
    return 1.0 / (1.0 + jnp.exp(-u))


def _rms(x, gain):
    return x * lax.rsqrt(jnp.mean(x * x, axis=-1, keepdims=True) + EPS) * gain


def _shifted(x, prev_row, next_row):
    n = x.shape[0]
    rows = lax.broadcasted_iota(jnp.int32, x.shape, 0)
    x_prev = jnp.where(rows == 0, prev_row, pltpu.roll(x, 1, 0))
    x_next = jnp.where(rows == n - 1, next_row, pltpu.roll(x, n - 1, 0))
    return x_prev, x_next


def _halo_rows(xp_ref, xn_ref, tiles_per_seq):
    si = pl.program_id(0) % tiles_per_seq
    prev_row = jnp.where(si == 0, 0.0, xp_ref[SUBLANES - 1:SUBLANES, :])
    next_row = jnp.where(si == tiles_per_seq - 1, 0.0, xn_ref[0:1, :])
    return prev_row, next_row


def _halo_specs(ts, width, col_block, n_rows):
    per = ts // SUBLANES
    last = n_rows // SUBLANES - 1
    prev = pl.BlockSpec((SUBLANES, width), lambda i: (jnp.maximum(i * per - 1, 0), col_block))
    nxt = pl.BlockSpec((SUBLANES, width), lambda i: (jnp.minimum((i + 1) * per, last), col_block))
    return prev, nxt


def _scan_masks(n, period, sgn):
    ri = lax.broadcasted_iota(jnp.int32, (n, n), 0)
    ci = lax.broadcasted_iota(jnp.int32, (n, n), 1)
    delta = ((ci & (period - 1)) - (ri & (period - 1))) * sgn
    return delta < 0, delta <= 0, ri == ci


def _norm_mm_kernel(x_ref, g_ref, w_ref, o_ref):
    xn = _rms(x_ref[...], g_ref[...])
    o_ref[...] = jnp.dot(xn.astype(BF16), w_ref[...], preferred_element_type=F32)


def _norm_matmul(h, gain, w, *, tm, out_spec=None, out_shape=None):
    m, dm = h.shape
    n = w.shape[1]
    if out_spec is None:
        out_spec = pl.BlockSpec((tm, n), lambda i: (i, 0))
        out_shape = (m, n)
    return pl.pallas_call(
        _norm_mm_kernel,
        grid=(m // tm,),
        in_specs=[pl.BlockSpec((tm, dm), lambda i: (i, 0)),
                  pl.BlockSpec((1, dm), lambda i: (0, 0)),
                  pl.BlockSpec((dm, n), lambda i: (0, 0))],
        out_specs=out_spec,
        out_shape=jax.ShapeDtypeStruct(out_shape, F32),
        compiler_params=_cparams("parallel"),
        name="norm_matmul",
    )(h, gain.reshape(1, dm), w)


def _mix_mlp_kernel(h_ref, y1_ref, y2_ref, wo1_ref, wo2_ref, gm_ref, w1_ref, w2_ref, gf_ref,
                    o_ref, h1_s, xn_s, acc_s, *, final_norm):
    kf = pl.program_id(1)

    @pl.when(kf == 0)
    def _():
        h1 = h_ref[...] + _bdot(y1_ref[...], wo1_ref[...]) + _bdot(y2_ref[...], wo2_ref[...])
        h1_s[...] = h1
        xn_s[...] = _rms(h1, gm_ref[...]).astype(BF16)
        acc_s[...] = jnp.zeros_like(acc_s)

    hid = jnp.dot(xn_s[...], w1_ref[...], preferred_element_type=F32)
    hid = jnp.square(jnp.maximum(hid, 0.0))
    acc_s[...] += jnp.dot(hid.astype(BF16), w2_ref[...], preferred_element_type=F32)

    @pl.when(kf == pl.num_programs(1) - 1)
    def _():
        out = h1_s[...] + acc_s[...]
        if final_norm:
            out = _rms(out, gf_ref[...])
        o_ref[...] = out


def _mix_mlp(h, y1, y1_spec, y2, wo1, wo2, g_mlp, w1, w2, g_final, *, tm, tf, final_norm):
    m, dm = h.shape
    ff = w1.shape[1]
    k2 = y2.shape[1]
    row = lambda i, k: (i, 0)
    fixed = lambda i, k: (0, 0)
    return pl.pallas_call(
        functools.partial(_mix_mlp_kernel, final_norm=final_norm),
        grid=(m // tm, ff // tf),
        in_specs=[pl.BlockSpec((tm, dm), row),
                  y1_spec,
                  pl.BlockSpec((tm, k2), row),
                  pl.BlockSpec(wo1.shape, fixed),
                  pl.BlockSpec(wo2.shape, fixed),
                  pl.BlockSpec((1, dm), fixed),
                  pl.BlockSpec((dm, tf), lambda i, k: (0, k)),
                  pl.BlockSpec((tf, dm), lambda i, k: (k, 0)),
                  pl.BlockSpec((1, dm), fixed)],
        out_specs=pl.BlockSpec((tm, dm), row),
        out_shape=jax.ShapeDtypeStruct((m, dm), F32),
        scratch_shapes=[pltpu.VMEM((tm, dm), F32), pltpu.VMEM((tm, dm), BF16), pltpu.VMEM((tm, dm), F32)],
        compiler_params=_cparams("parallel", "arbitrary"),
        name="mix_mlp",
    )(h, y1, y2, wo1, wo2, g_mlp.reshape(1, dm), w1, w2, g_final.reshape(1, dm))


def _dwconv_silu_kernel(x_ref, xp_ref, xn_ref, w_ref, b_ref, o_ref, *, tiles_per_seq):
    x = x_ref[...]
    prev_row, next_row = _halo_rows(xp_ref, xn_ref, tiles_per_seq)
    x_prev, x_next = _shifted(x, prev_row, next_row)
    w = w_ref[...]
    y = w[0:1] * x_prev + w[1:2] * x + w[2:3] * x_next + b_ref[...]
    o_ref[...] = (y * _sigmoid(y)).astype(o_ref.dtype)


def _dwconv_silu(cols, col_block, width, w, b, *, seq, ts):
    n_rows = cols.shape[0]
    prev, nxt = _halo_specs(ts, width, col_block, n_rows)
    return pl.pallas_call(
        functools.partial(_dwconv_silu_kernel, tiles_per_seq=seq // ts),
        grid=(n_rows // ts,),
        in_specs=[pl.BlockSpec((ts, width), lambda i: (i, col_block)), prev, nxt,
                  pl.BlockSpec((3, width), lambda i: (0, 0)),
                  pl.BlockSpec((1, width), lambda i: (0, 0))],
        out_specs=pl.BlockSpec((ts, width), lambda i: (i, 0)),
        out_shape=jax.ShapeDtypeStruct((n_rows, width), BF16),
        compiler_params=_cparams("parallel"),
        name="dwconv_silu",
    )(cols, cols, cols, w, b.reshape(1, width))


def _rw_prep_kernel(x_ref, xp_ref, xn_ref, mu_ref, w0_ref, w2_ref, a0_ref, a2_ref, g2_ref,
                    kk_ref, ka_ref, rk_ref, ones_ref,
                    r_o, k_o, v_o, kk_o, b_o, lw_o, bonus_o, g_o, *, tiles_per_seq):
    x = x_ref[...]
    prev_row, next_row = _halo_rows(xp_ref, xn_ref, tiles_per_seq)
    x_prev, x_next = _shifted(x, prev_row, next_row)
    mu = mu_ref[...]
    xs = x + mu[0:1] * (x_prev - x) + mu[1:2] * (x_next - x)
    r = xs[:, 0:RW_DIM]
    k = xs[:, RW_DIM:2 * RW_DIM]
    v = xs[:, 2 * RW_DIM:3 * RW_DIM]
    lr = xs[:, 3 * RW_DIM:3 * RW_DIM + LANES]
    g_lr = xs[:, 3 * RW_DIM + LANES:3 * RW_DIM + 2 * LANES]
    th = jnp.tanh(lr)
    for d in range(2):
        z = w0_ref[d:d + 1, :] + _bdot(th, w2_ref[d])
        lw_o[d] = -jnp.exp(-_softplus(-z) - 0.5)
    a_gate = _sigmoid(a0_ref[...] + _bdot(lr, a2_ref[...]))
    g_o[...] = _bdot(_sigmoid(g_lr), g2_ref[...])
    ones = ones_ref[...]
    kk = k * kk_ref[...]
    kk = kk * lax.rsqrt(jnp.maximum(_xdot_l(kk * kk, ones), 1e-12))
    k2 = k * (1.0 + (a_gate - 1.0) * ka_ref[...])
    r_o[...] = r.astype(BF16)
    k_o[...] = k2.astype(BF16)
    v_o[...] = v.astype(BF16)
    kk_o[...] = kk.astype(BF16)
    b_o[...] = (kk * a_gate).astype(BF16)
    bonus_o[...] = _xdot_l(r * k2 * rk_ref[...], ones) * v


def _rw_chunk_kernel(*refs, chunk):
    L = chunk
    ins = (refs[0:6], refs[6:12])
    y_refs = refs[12:14]
    s_ref = refs[14]

    @pl.when(pl.program_id(1) == 0)
    def _():
        s_ref[...] = jnp.zeros_like(s_ref)

    first = lax.broadcasted_iota(jnp.int32, (L, LANES), 1) < RW_HEAD

    def stack(t):
        return jnp.concatenate([jnp.where(first, t, 0.0), jnp.where(first, 0.0, t)], axis=0).astype(BF16)

    chains = []
    for dd in range(2):
        r_ref, k_ref, v_ref, kk_ref, b_ref, lw_ref = ins[dd]
        sgn = 1 - 2 * dd
        _, incl_l, _ = _scan_masks(L, L, sgn)
        tri = jnp.where(incl_l, 1.0, 0.0).astype(BF16)
        lw = lw_ref[0]
        c_incl = _xdot_r(tri, lw)
        c_tot = jnp.sum(lw, axis=0, keepdims=True)
        e_in = jnp.exp(c_incl)
        e_neg = jnp.exp(-c_incl)
        e_tot = jnp.exp(c_tot)
        e_rem = e_tot * e_neg
        kk = kk_ref[...]
        bb = b_ref[...]
        k = k_ref[...]
        v = v_ref[...]
        rt = r_ref[...] * e_in
        at = -kk * jnp.exp(c_incl - lw)
        bt = bb * e_neg
        kt = k * e_neg
        bh = bb * e_rem
        kh = k * e_rem
        strict, incl, eye = _scan_masks(2 * L, L, sgn)
        for p in range(RW_DIM // LANES):
            sl = slice(p * LANES, (p + 1) * LANES)
            chains.append(dict(
                dd=dd, p=p, sl=sl, strict=strict, incl=incl, eye=eye, e_tot=e_tot[:, sl],
                at=stack(at[:, sl]), rt=stack(rt[:, sl]), bt=stack(bt[:, sl]), kt=stack(kt[:, sl]),
                bh=stack(bh[:, sl]), kh=stack(kh[:, sl]), v=stack(v[:, sl]),
                s=s_ref[dd, p]))

    n2 = 2 * L
    for c in chains:
        ar = jnp.concatenate([c["at"], c["rt"]], axis=0)
        g = _bdot_nt(ar, jnp.concatenate([c["bt"], c["kt"]], axis=0))
        c["a_ab"] = jnp.where(c["strict"], g[:n2, :n2], 0.0)
        c["a_ak"] = jnp.where(c["strict"], g[:n2, n2:], 0.0)
        c["r_b"] = jnp.where(c["incl"], g[n2:, :n2], 0.0)
        c["r_k"] = jnp.where(c["incl"], g[n2:, n2:], 0.0)
        c["ar"] = ar
    n_sq = int(math.log2(L)) - 1
    for c in chains:
        c["inv"] = jnp.where(c["eye"], 1.0, 0.0) + c["a_ab"]
        c["pw"] = _bdot(c["a_ab"], c["a_ab"])
        lhs = jnp.concatenate([c["ar"], jnp.concatenate([c["a_ak"], c["r_k"]], axis=0).astype(BF16)], axis=1)
        c["wy"] = _bdot(lhs, jnp.concatenate([c["s"].T.astype(BF16), c["v"]], axis=0))
    for i in range(n_sq):
        for c in chains:
            if i + 1 < n_sq:
                both = _bdot(c["pw"], jnp.concatenate([c["inv"], c["pw"]], axis=1))
                c["inv"] = c["inv"] + both[:, :n2]
                c["pw"] = both[:, n2:]
            else:
                c["inv"] = c["inv"] + _bdot(c["pw"], c["inv"])
    for c in chains:
        c["u"] = _bdot(c["inv"], c["wy"][:n2])
    for c in chains:
        y_s = c["wy"][n2:] + _bdot(c["r_b"], c["u"])
        y_refs[c["dd"]][:, c["sl"]] = (y_s[:L] + y_s[L:]).astype(y_refs[c["dd"]].dtype)
        uv = jnp.concatenate([c["u"].astype(BF16), c["v"]], axis=0)
        s_ref[c["dd"], c["p"]] = c["s"] * c["e_tot"] + _bdot_tn(uv, jnp.concatenate([c["bh"], c["kh"]], axis=0))


def _rw_post_kernel(yf_ref, yb_ref, bonus_ref, g_ref, lnw_ref, ones_ref, o_ref):
    y = yf_ref[...].astype(F32) + yb_ref[...].astype(F32)
    ones = ones_ref[...]
    yc = y - _xdot_l(y, ones) * (1.0 / RW_HEAD)
    var = _xdot_l(yc * yc, ones) * (1.0 / RW_HEAD)
    yn = yc * lax.rsqrt(var + RW_GN_EPS) * lnw_ref[...]
    o_ref[...] = ((yn + bonus_ref[...]) * g_ref[...]).astype(o_ref.dtype)


def _rwkv7(cols, p, *, batch, seq, ts):
    n_rows = cols.shape[0]
    dim = RW_DIM
    row = lambda i: (i, 0)
    fixed2 = lambda i: (0, 0)
    fixed3 = lambda i: (0, 0, 0)
    prev, nxt = _halo_specs(ts, RW_COLS, 0, n_rows)
    ones = jnp.asarray(np.kron(np.eye(RW_HEADS), np.ones((RW_HEAD, RW_HEAD))), BF16)
    zeros = jnp.zeros((64, dim), F32)
    w2 = jnp.concatenate([p["w2"], jnp.broadcast_to(zeros, (2, 64, dim))], axis=1).astype(BF16)
    a2 = jnp.concatenate([zeros, p["a2"]], axis=0).astype(BF16)
    vec = lambda t: t.reshape(1, dim)
    tile = jax.ShapeDtypeStruct((n_rows, dim), F32)
    btile = jax.ShapeDtypeStruct((n_rows, dim), BF16)
    r, k, v, kk, b, lw, bonus, g = pl.pallas_call(
        functools.partial(_rw_prep_kernel, tiles_per_seq=seq // ts),
        grid=(n_rows // ts,),
        in_specs=[pl.BlockSpec((ts, RW_COLS), row), prev, nxt,
                  pl.BlockSpec((2, RW_COLS), fixed2),
                  pl.BlockSpec((2, dim), fixed2),
                  pl.BlockSpec((2, LANES, dim), fixed3),
                  pl.BlockSpec((1, dim), fixed2),
                  pl.BlockSpec((LANES, dim), fixed2),
                  pl.BlockSpec((LANES, dim), fixed2),
                  pl.BlockSpec((1, dim), fixed2),
                  pl.BlockSpec((1, dim), fixed2),
                  pl.BlockSpec((1, dim), fixed2),
                  pl.BlockSpec((dim, dim), fixed2)],
        out_specs=[pl.BlockSpec((ts, dim), row)] * 5
                  + [pl.BlockSpec((2, ts, dim), lambda i: (0, i, 0))]
                  + [pl.BlockSpec((ts, dim), row)] * 2,
        out_shape=[btile] * 5 + [jax.ShapeDtypeStruct((2, n_rows, dim), F32)] + [tile] * 2,
        compiler_params=_cparams("parallel"),
        name="rw_prep",
    )(cols, cols, cols, p["mu"], p["w0"], w2, vec(p["a0"]), a2, p["g2"].astype(BF16),
      vec(p["k_k"]), vec(p["k_a"]), vec(p["r_k"]), ones)

    L = RW_CHUNK
    nc = seq // L
    fwd = lambda bi, c: (bi * nc + c, 0)
    bwd = lambda bi, c: (bi * nc + nc - 1 - c, 0)
    y_fwd, y_bwd = pl.pallas_call(
        functools.partial(_rw_chunk_kernel, chunk=L),
        grid=(batch, nc),
        in_specs=[pl.BlockSpec((L, dim), fwd)] * 5 + [pl.BlockSpec((1, L, dim), lambda bi, c: (0, bi * nc + c, 0))]
                 + [pl.BlockSpec((L, dim), bwd)] * 5
                 + [pl.BlockSpec((1, L, dim), lambda bi, c: (1, bi * nc + nc - 1 - c, 0))],
        out_specs=[pl.BlockSpec((L, dim), fwd), pl.BlockSpec((L, dim), bwd)],
        out_shape=[btile, btile],
        scratch_shapes=[pltpu.VMEM((2, dim // LANES, LANES, LANES), F32)],
        compiler_params=_cparams("parallel", "arbitrary"),
        name="rw_chunk",
    )(r, k, v, kk, b, lw, r, k, v, kk, b, lw)

    return pl.pallas_call(
        _rw_post_kernel,
        grid=(n_rows // ts,),
        in_specs=[pl.BlockSpec((ts, dim), row), pl.BlockSpec((ts, dim), row),
                  pl.BlockSpec((ts, dim), row), pl.BlockSpec((ts, dim), row),
                  pl.BlockSpec((1, dim), fixed2), pl.BlockSpec((dim, dim), fixed2)],
        out_specs=pl.BlockSpec((ts, dim), row),
        out_shape=btile,
        compiler_params=_cparams("parallel"),
        name="rw_post",
    )(y_fwd, y_bwd, bonus, g, vec(p["ln_w"]), ones)


def _mb_ssd_kernel(*refs, chunk):
    L = chunk
    ins = (refs[0:5], refs[5:10])
    bias_c_ref, alog_c_ref, bias_r_ref, alog_r_ref, e1_ref = refs[10:15]
    y_refs = refs[15:17]
    st_ref = refs[17]

    @pl.when(pl.program_id(1) == 0)
    def _():
        st_ref[...] = jnp.zeros_like(st_ref)

    ri = lax.broadcasted_iota(jnp.int32, (L, L), 0)
    ci = lax.broadcasted_iota(jnp.int32, (L, L), 1)
    first = lax.broadcasted_iota(jnp.int32, (L, LANES), 1) < MB_HEADDIM
    gw = MB_HPG * MB_HEADDIM
    groups = []
    for dd in range(2):
        xs_ref, bc_ref, bt_ref, dtc_ref, dtr_ref = ins[dd]
        sgn = 1 - 2 * dd
        incl = (ci - ri) * sgn <= 0
        tri = jnp.where(incl, 1.0, 0.0).astype(BF16)
        tri_t = jnp.where((ri - ci) * sgn <= 0, 1.0, 0.0).astype(BF16)
        last = L - 1 if dd == 0 else 0
        dt_c = _softplus(dtc_ref[...] + bias_c_ref[...])
        cs_c = _xdot_r(tri, dt_c * (-jnp.exp(alog_c_ref[...])))
        cols2 = jnp.concatenate([jnp.exp(cs_c), dt_c * jnp.exp(cs_c[last:last + 1, :] - cs_c)], axis=0)
        hi = cols2.astype(BF16)
        mid = (cols2 - hi.astype(F32)).astype(BF16)
        full2 = jnp.dot(jnp.concatenate([hi, mid], axis=1), e1_ref[dd], preferred_element_type=F32)
        ecs = full2[:L]
        xw = (xs_ref[...] * full2[L:]).astype(BF16)
        etot = ecs[last:last + 1, :]
        xb = xs_ref[...].astype(BF16)
        dt_r = _softplus(dtr_ref[0] + bias_r_ref[dd])
        cs_r = _xdot_l(dt_r * (-jnp.exp(alog_r_ref[dd])), tri_t)
        for g in range(MB_GROUPS):
            gs = slice(g * gw, (g + 1) * gw)
            groups.append(dict(
                dd=dd, g=g, gs=gs, incl=incl, ecs=ecs[:, gs], etot=etot[:, gs], xw=xw[:, gs], xb=xb[:, gs],
                cs_c=cs_c, cs_r=cs_r, dt_r=dt_r,
                bg=bc_ref[:, g * MB_STATE:(g + 1) * MB_STATE].astype(BF16),
                cg=bc_ref[:, (MB_GROUPS + g) * MB_STATE:(MB_GROUPS + g + 1) * MB_STATE].astype(BF16),
                bt=bt_ref[0, g * MB_STATE:(g + 1) * MB_STATE, :],
                st=st_ref[dd, :, gs]))

    for c in groups:
        c["scores"] = _bdot_nt(c["cg"], c["bg"])
        c["y_off"] = _bdot(c["cg"], c["st"]) * c["ecs"]
        st_ref[c["dd"], :, c["gs"]] = c["st"] * c["etot"] + _bdot(c["bt"], c["xw"])
    for c in groups:
        pairs = []
        for j in range(MB_HPG // 2):
            xp = c["xb"][:, j * LANES:(j + 1) * LANES]
            halves = []
            for hh in range(2):
                h = c["g"] * MB_HPG + 2 * j + hh
                lane = MB_HEADS * c["dd"] + h
                b_t = jnp.broadcast_to(c["cs_c"][:, lane:lane + 1], (L, L))
                expo = jnp.minimum(b_t - c["cs_r"][h:h + 1, :], 0.0)
                m = jnp.where(c["incl"], c["scores"] * jnp.exp(expo) * c["dt_r"][h:h + 1, :], 0.0)
                halves.append(_bdot(m, xp))
            pairs.append(jnp.where(first, halves[0], halves[1]))
        y_refs[c["dd"]][:, c["gs"]] = (c["y_off"] + jnp.concatenate(pairs, axis=1)).astype(y_refs[c["dd"]].dtype)


def _mb_post_kernel(yf_ref, yb_ref, xs_ref, z_ref, d_ref, nw_ref, o_ref):
    y = yf_ref[...].astype(F32) + yb_ref[...].astype(F32) + d_ref[...] * xs_ref[...]
    z = z_ref[...]
    y = y * (z * _sigmoid(z))
    gw = MB_DIM // MB_GROUPS
    for g in range(MB_GROUPS):
        yg = y[:, g * gw:(g + 1) * gw]
        o_ref[:, g * gw:(g + 1) * gw] = (yg * lax.rsqrt(jnp.mean(yg * yg, axis=-1, keepdims=True) + EPS)
                                          * nw_ref[:, g * gw:(g + 1) * gw]).astype(o_ref.dtype)


def _mamba2(cols, p, *, batch, seq, ts):
    n_rows = cols.shape[0]
    xs = _dwconv_silu(cols, 1, MB_DIM, p["conv_w"][:, :MB_DIM], p["conv_b"][:MB_DIM], seq=seq, ts=ts)
    bc = _dwconv_silu(cols, 2 * MB_DIM // MB_BC, MB_BC, p["conv_w"][:, MB_DIM:], p["conv_b"][MB_DIM:],
                      seq=seq, ts=ts)
    L = MB_CHUNK
    nc = seq // L
    dt_col_block = (2 * MB_DIM + MB_BC) // LANES
    dt_raw = cols[:, 2 * MB_DIM + MB_BC:2 * MB_DIM + MB_BC + 2 * MB_HEADS]
    dt_rows = jnp.swapaxes(dt_raw.reshape(batch, seq, 2 * MB_HEADS), 1, 2)
    pad = jnp.zeros((LANES - 2 * MB_HEADS,), F32)
    bias_c = jnp.concatenate([p["dt_bias"].reshape(-1), pad]).reshape(1, LANES)
    alog_c = jnp.concatenate([p["A_log"].reshape(-1), pad]).reshape(1, LANES)
    bias_r = p["dt_bias"].reshape(2, MB_HEADS, 1)
    alog_r = p["A_log"].reshape(2, MB_HEADS, 1)
    e1 = np.zeros((2, 2 * LANES, MB_DIM), np.float32)
    for d in range(2):
        for h in range(MB_HEADS):
            e1[d, MB_HEADS * d + h, h * MB_HEADDIM:(h + 1) * MB_HEADDIM] = 1.0
            e1[d, LANES + MB_HEADS * d + h, h * MB_HEADDIM:(h + 1) * MB_HEADDIM] = 1.0
    b_t = jnp.swapaxes(bc[:, :MB_GROUPS * MB_STATE].reshape(batch, seq, MB_GROUPS * MB_STATE), 1, 2).astype(BF16)
    fixed = lambda bi, c: (0, 0)
    fixed3c = lambda bi, c: (0, 0, 0)

    def dir_specs(d):
        chunk_of = (lambda c: c) if d == 0 else (lambda c: nc - 1 - c)
        rows = lambda bi, c: (bi * nc + chunk_of(c), 0)
        return ([pl.BlockSpec((L, MB_DIM), rows),
                 pl.BlockSpec((L, MB_BC), rows),
                 pl.BlockSpec((1, MB_GROUPS * MB_STATE, L), lambda bi, c: (bi, 0, chunk_of(c))),
                 pl.BlockSpec((L, LANES), lambda bi, c: (bi * nc + chunk_of(c), dt_col_block)),
                 pl.BlockSpec((1, MB_HEADS, L), lambda bi, c: (bi, d, chunk_of(c)))],
                pl.BlockSpec((L, MB_DIM), rows))

    (in_f, out_f), (in_b, out_b) = dir_specs(0), dir_specs(1)
    y_tile = jax.ShapeDtypeStruct((n_rows, MB_DIM), BF16)
    y_fwd, y_bwd = pl.pallas_call(
        functools.partial(_mb_ssd_kernel, chunk=L),
        grid=(batch, nc),
        in_specs=in_f + in_b + [pl.BlockSpec((1, LANES), fixed), pl.BlockSpec((1, LANES), fixed),
                                pl.BlockSpec((2, MB_HEADS, 1), fixed3c), pl.BlockSpec((2, MB_HEADS, 1), fixed3c),
                                pl.BlockSpec((2, 2 * LANES, MB_DIM), fixed3c)],
        out_specs=[out_f, out_b],
        out_shape=[y_tile, y_tile],
        scratch_shapes=[pltpu.VMEM((2, MB_STATE, MB_DIM), F32)],
        compiler_params=_cparams("parallel", "arbitrary"),
        name="mb_ssd",
    )(xs, bc, b_t, cols, dt_rows, xs, bc, b_t, cols, dt_rows, bias_c, alog_c, bias_r, alog_r, jnp.asarray(e1, BF16))

    row = lambda i: (i, 0)
    fixed2 = lambda i: (0, 0)
    return pl.pallas_call(
        _mb_post_kernel,
        grid=(n_rows // ts,),
        in_specs=[pl.BlockSpec((ts, MB_DIM), row), pl.BlockSpec((ts, MB_DIM), row),
                  pl.BlockSpec((ts, MB_DIM), row),
                  pl.BlockSpec((ts, MB_DIM), row),
                  pl.BlockSpec((1, MB_DIM), fixed2), pl.BlockSpec((1, MB_DIM), fixed2)],
        out_specs=pl.BlockSpec((ts, MB_DIM), row),
        out_shape=y_tile,
        compiler_params=_cparams("parallel"),
        name="mb_post",
    )(y_fwd, y_bwd, xs, cols, jnp.repeat(p["D"], MB_HEADDIM).reshape(1, MB_DIM), p["norm_w"].reshape(1, MB_DIM))


S5_GROUP, S5_GROUPS, S5_STATE = 16, 32, 64
S5_DIM = S5_GROUP * S5_GROUPS
S5_GPB = LANES // S5_GROUP
S5_BLOCKS = S5_GROUPS // S5_GPB
S5_HALF = S5_GPB * S5_STATE
S5_STEPS = 64


def _s5_disc_kernel(ar_ref, ai_ref, ldt_ref, bre_ref, bim_ref, abr_o, abi_o, bbr_o, bbi_o):
    dt = jnp.exp(ldt_ref[0])
    ar = jnp.minimum(ar_ref[0], -1e-4)
    ai = ai_ref[0]
    mag = jnp.exp(dt * ar)
    abr = mag * jnp.cos(dt * ai)
    abi = mag * jnp.sin(dt * ai)
    den = ar * ar + ai * ai
    fr = ((abr - 1.0) * ar + abi * ai) / den
    fi = (abi * ar - (abr - 1.0) * ai) / den
    bre = bre_ref[...]
    bim = bim_ref[...]
    abr_o[0] = abr
    abi_o[0] = abi
    bbr_o[0] = fr * bre - fi * bim
    bbi_o[0] = fr * bim + fi * bre


def _s5_scan_kernel(u_ref, bw_ref, cw_ref, lr_ref, li_ref, y_ref, x_s, st_s, *, batch, steps):
    d = pl.program_id(0)

    @pl.when(pl.program_id(1) == 0)
    def _():
        st_s[...] = jnp.zeros_like(st_s)

    u = u_ref[...].astype(BF16)
    width = 2 * S5_HALF
    for j in range(S5_BLOCKS):
        x_s[:, j * width:(j + 1) * width] = jnp.dot(u[:, j * LANES:(j + 1) * LANES], bw_ref[0, j],
                                                     preferred_element_type=F32)
    for j in range(S5_BLOCKS):
        re = slice(j * width, j * width + S5_HALF)
        im = slice(j * width + S5_HALF, (j + 1) * width)
        lam_r = jnp.broadcast_to(lr_ref[0, :, j * S5_HALF:(j + 1) * S5_HALF], (batch, S5_HALF))
        lam_i = jnp.broadcast_to(li_ref[0, :, j * S5_HALF:(j + 1) * S5_HALF], (batch, S5_HALF))

        def body(i, carry):
            xr, xi = carry
            t = jnp.where(d == 0, i, steps - 1 - i)
            rows = pl.ds(pl.multiple_of(t * batch, batch), batch)
            nr = lam_r * xr - lam_i * xi + x_s[rows, re]
            ni = lam_r * xi + lam_i * xr + x_s[rows, im]
            x_s[rows, re] = nr
            x_s[rows, im] = ni
            return nr, ni

        xr, xi = lax.fori_loop(0, steps, body, (st_s[:, re], st_s[:, im]))
        st_s[:, re] = xr
        st_s[:, im] = xi
    for j in range(S5_BLOCKS):
        y_ref[0, :, j * LANES:(j + 1) * LANES] = jnp.dot(
            x_s[:, j * width:(j + 1) * width].astype(BF16), cw_ref[0, j],
            preferred_element_type=F32).astype(y_ref.dtype)


def _s5_post_kernel(y_ref, u_ref, d_ref, gw_ref, gb_ref, o_ref):
    y = d_ref[...] * u_ref[...] + y_ref[0].astype(F32) + y_ref[1].astype(F32)
    y = 0.5 * y * (1.0 + jnp.tanh(math.sqrt(2.0 / math.pi) * (y + 0.044715 * (y * y * y))))
    o_ref[...] = (y * _sigmoid(_bdot(y, gw_ref[...]) + gb_ref[...])).astype(o_ref.dtype)


def _s5(u, p, *, batch, seq):
    n_rows = u.shape[0]
    gp = S5_GROUPS * S5_STATE
    bc = lambda t: jnp.broadcast_to(t.reshape(2, gp, 1), (2, gp, S5_GROUP))
    ldt = jnp.broadcast_to(p["log_dt"][:, :, None, None], (2, S5_GROUPS, S5_STATE, S5_GROUP)).reshape(2, gp, S5_GROUP)
    per_dir = pl.BlockSpec((1, gp, S5_GROUP), lambda d: (d, 0, 0))
    shared = pl.BlockSpec((gp, S5_GROUP), lambda d: (0, 0))
    disc = jax.ShapeDtypeStruct((2, gp, S5_GROUP), F32)
    abr, abi, bbr, bbi = pl.pallas_call(
        _s5_disc_kernel,
        grid=(2,),
        in_specs=[per_dir, per_dir, per_dir, shared, shared],
        out_specs=[per_dir] * 4,
        out_shape=[disc] * 4,
        compiler_params=_cparams("parallel"),
        name="s5_disc",
    )(bc(p["A_re"]), bc(p["A_im"]), ldt, p["B_re"].reshape(gp, S5_GROUP), p["B_im"].reshape(gp, S5_GROUP))

    eye = jnp.eye(S5_GPB, dtype=F32)
    shp = (2, S5_BLOCKS, S5_GPB, S5_STATE, S5_GROUP)
    b_blk = lambda t: jnp.einsum("djgpm,gh->djgmhp", t.reshape(shp), eye).reshape(2, S5_BLOCKS, LANES, S5_HALF)
    bw = jnp.concatenate([b_blk(bbr), b_blk(bbi)], axis=-1).astype(BF16)
    cshp = (2, S5_BLOCKS, S5_GPB, S5_GROUP, S5_STATE)
    c_blk = lambda t: jnp.einsum("djgmp,gh->djgphm", t.reshape(cshp), eye).reshape(2, S5_BLOCKS, S5_HALF, LANES)
    cw = jnp.concatenate([c_blk(p["C_re"]), c_blk(-p["C_im"])], axis=2).astype(BF16)
    lam_r = abr[:, :, 0].reshape(2, 1, gp)
    lam_i = abi[:, :, 0].reshape(2, 1, gp)

    steps = min(S5_STEPS, seq)
    nc = seq // steps
    tr = steps * batch
    width = 2 * S5_HALF
    rows = lambda d, c: (c + d * (nc - 1 - 2 * c), 0)
    by_dir4 = lambda d, c: (d, 0, 0, 0)
    by_dir3 = lambda d, c: (d, 0, 0)
    y = pl.pallas_call(
        functools.partial(_s5_scan_kernel, batch=batch, steps=steps),
        grid=(2, nc),
        in_specs=[pl.BlockSpec((tr, S5_DIM), rows),
                  pl.BlockSpec((1, S5_BLOCKS, LANES, width), by_dir4),
                  pl.BlockSpec((1, S5_BLOCKS, width, LANES), by_dir4),
                  pl.BlockSpec((1, 1, gp), by_dir3), pl.BlockSpec((1, 1, gp), by_dir3)],
        out_specs=pl.BlockSpec((1, tr, S5_DIM), lambda d, c: (d, c + d * (nc - 1 - 2 * c), 0)),
        out_shape=jax.ShapeDtypeStruct((2, n_rows, S5_DIM), BF16),
        scratch_shapes=[pltpu.VMEM((tr, S5_BLOCKS * width), F32), pltpu.VMEM((batch, S5_BLOCKS * width), F32)],
        compiler_params=_cparams("parallel", "arbitrary"),
        name="s5_scan",
    )(u, bw, cw, lam_r, lam_i)

    tp = min(512, n_rows)
    row = lambda i: (i, 0)
    fixed = lambda i: (0, 0)
    return pl.pallas_call(
        _s5_post_kernel,
        grid=(n_rows // tp,),
        in_specs=[pl.BlockSpec((2, tp, S5_DIM), lambda i: (0, i, 0)), pl.BlockSpec((tp, S5_DIM), row),
                  pl.BlockSpec((1, S5_DIM), fixed), pl.BlockSpec((S5_DIM, S5_DIM), fixed),
                  pl.BlockSpec((1, S5_DIM), fixed)],
        out_specs=pl.BlockSpec((tp, S5_DIM), row),
        out_shape=jax.ShapeDtypeStruct((n_rows, S5_DIM), BF16),
        compiler_params=_cparams("parallel"),
        name="s5_post",
    )(y, u, p["D"].reshape(1, S5_DIM), p["glu_w"].astype(BF16), p["glu_b"].reshape(1, S5_DIM))


ML_HEADS, ML_HEAD, ML_BLOCK = 8, 128, 4
ML_DIM = ML_HEADS * ML_HEAD
ML_COLS_PAD = 2 * ML_DIM + LANES
ML_CHUNK = 128
ML_PROJ = 256
NEG_BIG = -1e30


def _log_sigmoid(u):
    return -_softplus(-u)


def _ml_prep_kernel(x_ref, xp_ref, xn_ref, cw_ref, cb_ref, wq_ref, wk_ref, wv_ref,
                    xc_o, q_o, k_o, v_o, *, tiles_per_seq):
    x = x_ref[...]
    prev_row, next_row = _halo_rows(xp_ref, xn_ref, tiles_per_seq)
    x_prev, x_next = _shifted(x, prev_row, next_row)
    w = cw_ref[...]
    y = w[0:1] * x_prev + w[1:2] * x + w[2:3] * x_next + cb_ref[...]
    xc = y * _sigmoid(y)
    xc_o[...] = xc.astype(BF16)
    xcb = xc.astype(BF16)
    xb = x.astype(BF16)
    for j in range(ML_DIM // ML_PROJ):
        sl = slice(j * ML_PROJ, (j + 1) * ML_PROJ)
        q_o[:, sl] = jnp.dot(xcb[:, sl], wq_ref[j], preferred_element_type=F32).astype(BF16)
        k_o[:, sl] = (jnp.dot(xcb[:, sl], wk_ref[j], preferred_element_type=F32) * (ML_HEAD ** -0.5)).astype(BF16)
        v_o[:, sl] = jnp.dot(xb[:, sl], wv_ref[j], preferred_element_type=F32).astype(BF16)


def _ml_chunk_kernel(*refs, chunk):
    L = chunk
    ins = (refs[0:7], refs[7:14])
    bias_c_ref, ib_ref, fb_ref = refs[14:17]
    h_refs = refs[17:19]
    c_s, m_s = refs[19:21]

    @pl.when(pl.program_id(1) == 0)
    def _():
        c_s[...] = jnp.zeros_like(c_s)
        m_s[...] = jnp.zeros_like(m_s)

    lane = lax.broadcasted_iota(jnp.int32, (L, LANES), 1)
    ones_tile = jnp.ones((L, ML_HEAD), BF16)
    ri = lax.broadcasted_iota(jnp.int32, (L, L), 0)
    ci = lax.broadcasted_iota(jnp.int32, (L, L), 1)
    chains = []
    for dd in range(2):
        q_ref, k_ref, kt_ref, v_ref, gc_ref, gi_ref, gf_ref = ins[dd]
        sgn = 1 - 2 * dd
        incl = (ci - ri) * sgn <= 0
        tri = jnp.where(incl, 1.0, 0.0).astype(BF16)
        tri_t = jnp.where((ri - ci) * sgn <= 0, 1.0, 0.0).astype(BF16)
        gpre = gc_ref[...] + bias_c_ref[...]
        gcol = jnp.where(lane < 2 * ML_HEADS, gpre, _log_sigmoid(gpre))
        cs_c = _xdot_r(tri, gcol)
        li_r = gi_ref[0] + ib_ref[dd]
        lf_r = _log_sigmoid(gf_ref[0] + fb_ref[dd])
        b_r = _xdot_l(lf_r, tri_t)
        b_last = b_r[:, L - 1:L] if dd == 0 else b_r[:, 0:1]
        lw_r = b_last - b_r + li_r
        lw_max = jnp.max(lw_r, axis=-1, keepdims=True)
        for h in range(ML_HEADS):
            sl = slice(h * ML_HEAD, (h + 1) * ML_HEAD)
            jf = 2 * ML_HEADS + ML_HEADS * dd + h
            chains.append(dict(
                dd=dd, h=h, sl=sl, incl=incl, q=q_ref[:, sl], k=k_ref[:, sl], kt=kt_ref[0, sl, :],
                v_ext=jnp.concatenate([v_ref[:, sl], ones_tile], axis=1),
                b_t=jnp.broadcast_to(cs_c[:, jf:jf + 1], (L, LANES)),
                b_row=b_r[h:h + 1, :], li_row=li_r[h:h + 1, :], lw_row=lw_r[h:h + 1, :],
                bl=b_last[h:h + 1, :], lw_max=lw_max[h:h + 1, :],
                m_prev=m_s[dd, h:h + 1, 0:1], c_ext=c_s[dd, h]))

    for c in chains:
        log_d = jnp.where(c["incl"], c["b_t"] - c["b_row"] + c["li_row"], NEG_BIG)
        inter = c["b_t"] + c["m_prev"]
        m_t = jnp.maximum(jnp.broadcast_to(jnp.max(log_d, axis=-1, keepdims=True), (L, LANES)), inter)
        c["m_t"] = m_t
        c["dmat"] = jnp.exp(log_d - m_t)
        c["w_in"] = jnp.exp(inter - m_t)
        c["qk"] = _bdot_nt(c["q"], c["k"])
        c["qc"] = _bdot(c["q"], c["c_ext"])
        m_new = jnp.maximum(c["bl"] + c["m_prev"], c["lw_max"])
        wkt = c["kt"].astype(F32) * jnp.exp(c["lw_row"] - m_new)
        c["c_new"] = jnp.exp(c["bl"] + c["m_prev"] - m_new) * c["c_ext"] + _bdot(wkt, c["v_ext"])
        c["m_new"] = m_new
    for c in chains:
        w_in2 = jnp.concatenate([c["w_in"], c["w_in"]], axis=1)
        nd = _bdot(c["qk"] * c["dmat"], c["v_ext"]) + w_in2 * c["qc"]
        den = nd[:, ML_HEAD:]
        h_refs[c["dd"]][:, c["sl"]] = (nd[:, :ML_HEAD] / jnp.maximum(jnp.abs(den), jnp.exp(-c["m_t"]))).astype(BF16)
        c_s[c["dd"], c["h"]] = c["c_new"]
        m_s[c["dd"], c["h"]:c["h"] + 1, :] = jnp.broadcast_to(c["m_new"], (1, LANES))


def _ml_post_kernel(hf_ref, hb_ref, o_ref_in, xc_ref, nw_ref, skip_ref, out_ref):
    hsum = hf_ref[...].astype(F32) + hb_ref[...].astype(F32)
    for h in range(ML_HEADS):
        sl = slice(h * ML_HEAD, (h + 1) * ML_HEAD)
        x = hsum[:, sl]
        xc = x - jnp.mean(x, axis=-1, keepdims=True)
        hn = xc * lax.rsqrt(jnp.mean(xc * xc, axis=-1, keepdims=True) + EPS) * nw_ref[:, sl]
        out_ref[:, sl] = (_sigmoid(o_ref_in[:, sl]) * hn + skip_ref[:, sl] * xc_ref[:, sl]).astype(out_ref.dtype)


def _mlstm(cols, p, *, batch, seq, ts):
    n_rows = cols.shape[0]
    dim = ML_DIM
    row = lambda i: (i, 0)
    fixed2 = lambda i: (0, 0)
    fixed3 = lambda i: (0, 0, 0)
    prev, nxt = _halo_specs(ts, dim, 0, n_rows)
    nblk = dim // ML_PROJ
    per = ML_PROJ // ML_BLOCK
    eye = jnp.eye(per, dtype=F32)
    blockdiag = lambda w: jnp.einsum("bjcd,jk->bjckd", w.reshape(nblk, per, ML_BLOCK, ML_BLOCK),
                                     eye).reshape(nblk, ML_PROJ, ML_PROJ).astype(BF16)
    wspec = pl.BlockSpec((nblk, ML_PROJ, ML_PROJ), fixed3)
    xc, q, k, v = pl.pallas_call(
        functools.partial(_ml_prep_kernel, tiles_per_seq=seq // ts),
        grid=(n_rows // ts,),
        in_specs=[pl.BlockSpec((ts, dim), row), prev, nxt,
                  pl.BlockSpec((3, dim), fixed2), pl.BlockSpec((1, dim), fixed2), wspec, wspec, wspec],
        out_specs=[pl.BlockSpec((ts, dim), row)] * 4,
        out_shape=[jax.ShapeDtypeStruct((n_rows, dim), BF16)] * 4,
        compiler_params=_cparams("parallel"),
        name="ml_prep",
    )(cols, cols, cols, p["conv_w"], p["conv_b"].reshape(1, dim),
      blockdiag(p["wq"]), blockdiag(p["wk"]), blockdiag(p["wv"]))

    L = min(ML_CHUNK, seq)
    nc = seq // L
    ng = 4 * ML_HEADS
    gate_col_block = 2 * dim // LANES
    g_rows = jnp.swapaxes(cols[:, 2 * dim:2 * dim + ng].reshape(batch, seq, ng), 1, 2)
    k_t = jnp.swapaxes(k.reshape(batch, seq, dim), 1, 2)
    pad = jnp.zeros((LANES - ng,), F32)
    bias_c = jnp.concatenate([p["i_b"].reshape(-1), p["f_b"].reshape(-1), pad]).reshape(1, LANES)
    ib = p["i_b"].reshape(2, ML_HEADS, 1)
    fb = p["f_b"].reshape(2, ML_HEADS, 1)
    fixed = lambda bi, c: (0, 0)
    fixed3c = lambda bi, c: (0, 0, 0)

    def dir_specs(d):
        chunk_of = (lambda c: c) if d == 0 else (lambda c: nc - 1 - c)
        rows = lambda bi, c: (bi * nc + chunk_of(c), 0)
        return ([pl.BlockSpec((L, dim), rows)] * 2
                + [pl.BlockSpec((1, dim, L), lambda bi, c: (bi, 0, chunk_of(c))),
                   pl.BlockSpec((L, dim), rows),
                   pl.BlockSpec((L, LANES), lambda bi, c: (bi * nc + chunk_of(c), gate_col_block)),
                   pl.BlockSpec((1, ML_HEADS, L), lambda bi, c: (bi, d, chunk_of(c))),
                   pl.BlockSpec((1, ML_HEADS, L), lambda bi, c: (bi, 2 + d, chunk_of(c)))],
                pl.BlockSpec((L, dim), rows))

    (in_f, out_f), (in_b, out_b) = dir_specs(0), dir_specs(1)
    h_tile = jax.ShapeDtypeStruct((n_rows, dim), BF16)
    h_fwd, h_bwd = pl.pallas_call(
        functools.partial(_ml_chunk_kernel, chunk=L),
        grid=(batch, nc),
        in_specs=in_f + in_b + [pl.BlockSpec((1, LANES), fixed),
                                pl.BlockSpec((2, ML_HEADS, 1), fixed3c), pl.BlockSpec((2, ML_HEADS, 1), fixed3c)],
        out_specs=[out_f, out_b],
        out_shape=[h_tile, h_tile],
        scratch_shapes=[pltpu.VMEM((2, ML_HEADS, ML_HEAD, 2 * ML_HEAD), F32),
                        pltpu.VMEM((2, ML_HEADS, LANES), F32)],
        compiler_params=_cparams("parallel", "arbitrary"),
        name="ml_chunk",
    )(q, k, k_t, v, cols, g_rows, g_rows, q, k, k_t, v, cols, g_rows, g_rows, bias_c, ib, fb)

    return pl.pallas_call(
        _ml_post_kernel,
        grid=(n_rows // ts,),
        in_specs=[pl.BlockSpec((ts, dim), row), pl.BlockSpec((ts, dim), row),
                  pl.BlockSpec((ts, dim), lambda i: (i, 1)),
                  pl.BlockSpec((ts, dim), row),
                  pl.BlockSpec((1, dim), fixed2), pl.BlockSpec((1, dim), fixed2)],
        out_specs=pl.BlockSpec((ts, dim), row),
        out_shape=h_tile,
        compiler_params=_cparams("parallel"),
        name="ml_post",
    )(h_fwd, h_bwd, cols, xc, p["norm_w"].reshape(1, dim), p["skip"].reshape(1, dim))


ROW_TILE = 512
MIXER_TILE = 512
FF_TILE = 1024


def _pad_cols(w, n):
    return jnp.pad(w, ((0, 0), (0, n - w.shape[1])))


def kernel(x, norm_mix, norm_mlp, norm_final, mlp_w1, mlp_w2, ab_w_in, ab_w_out, rw_mu, rw_w0, rw_w2, rw_a0, rw_a2, rw_g2, rw_k_k, rw_k_a, rw_r_k, rw_ln_w, mb_conv_w, mb_conv_b, mb_dt_bias, mb_A_log, mb_D, mb_norm_w, cd_w_in, cd_w_out, s5_A_re, s5_A_im, s5_log_dt, s5_B_re, s5_B_im, s5_C_re, s5_C_im, s5_D, s5_glu_w, s5_glu_b, ml_conv_w, ml_conv_b, ml_wq, ml_wk, ml_wv, ml_i_b, ml_f_b, ml_norm_w, ml_skip):
    batch, seq, dm = x.shape
    n_rows = batch * seq
    tm = min(ROW_TILE, seq)
    ts = min(MIXER_TILE, seq)
    tiles_per_seq = seq // tm
    h = x.reshape(n_rows, dm)
    depth = norm_mix.shape[0]
    for layer in range(depth):
        i = layer // 2
        last = layer == depth - 1
        if layer % 2 == 0:
            w_in = ab_w_in[i]
            rw_cols = _norm_matmul(h, norm_mix[layer], w_in[:, :RW_COLS].astype(BF16), tm=tm)
            mb_cols = _norm_matmul(h, norm_mix[layer], _pad_cols(w_in[:, RW_COLS:], MB_COLS_PAD).astype(BF16), tm=tm)
            y1 = _rwkv7(rw_cols, dict(mu=rw_mu[i], w0=rw_w0[i], w2=rw_w2[i], a0=rw_a0[i], a2=rw_a2[i], g2=rw_g2[i],
                                      k_k=rw_k_k[i], k_a=rw_k_a[i], r_k=rw_r_k[i].reshape(-1), ln_w=rw_ln_w[i]),
                        batch=batch, seq=seq, ts=ts)
            y2 = _mamba2(mb_cols, dict(conv_w=mb_conv_w[i], conv_b=mb_conv_b[i], dt_bias=mb_dt_bias[i],
                                       A_log=mb_A_log[i], D=mb_D[i], norm_w=mb_norm_w[i]),
                         batch=batch, seq=seq, ts=ts)
            y1_spec = pl.BlockSpec((tm, RW_DIM), lambda r, k: (r, 0))
            w_out, k1 = ab_w_out[i], RW_DIM
        else:
            w_in = cd_w_in[i]
            tm_spec = pl.BlockSpec((tm, S5_DIM), lambda r: (r % tiles_per_seq, r // tiles_per_seq))
            s5_cols = _norm_matmul(h, norm_mix[layer], w_in[:, :S5_DIM].astype(BF16), tm=tm,
                                   out_spec=tm_spec, out_shape=(seq, batch * S5_DIM))
            ml_cols = _norm_matmul(h, norm_mix[layer], _pad_cols(w_in[:, S5_DIM:], ML_COLS_PAD).astype(BF16), tm=tm)
            y1 = _s5(s5_cols.reshape(seq * batch, S5_DIM),
                     dict(A_re=s5_A_re[i], A_im=s5_A_im[i], log_dt=s5_log_dt[i], B_re=s5_B_re[i], B_im=s5_B_im[i],
                          C_re=s5_C_re[i], C_im=s5_C_im[i], D=s5_D[i], glu_w=s5_glu_w[i], glu_b=s5_glu_b[i]),
                     batch=batch, seq=seq).reshape(seq, batch * S5_DIM)
            y2 = _mlstm(ml_cols, dict(conv_w=ml_conv_w[i], conv_b=ml_conv_b[i], wq=ml_wq[i], wk=ml_wk[i], wv=ml_wv[i],
                                      i_b=ml_i_b[i], f_b=ml_f_b[i], norm_w=ml_norm_w[i], skip=ml_skip[i]),
                        batch=batch, seq=seq, ts=ts)
            y1_spec = pl.BlockSpec((tm, S5_DIM), lambda r, k: (r % tiles_per_seq, r // tiles_per_seq))
            w_out, k1 = cd_w_out[i], S5_DIM
        h = _mix_mlp(h, y1, y1_spec, y2, w_out[:k1].astype(BF16), w_out[k1:].astype(BF16), norm_mlp[layer],
                     mlp_w1[layer].astype(BF16), mlp_w2[layer].astype(BF16), norm_final,
                     tm=tm, tf=FF_TILE, final_norm=last)
    return h.reshape(batch, seq, dm)
```

```python
import functools
import math

import jax
import jax.numpy as jnp
import numpy as np
from jax import lax
from jax.experimental import pallas as pl
from jax.experimental.pallas import tpu as pltpu

F32 = jnp.float32
BF16 = jnp.bfloat16

EPS = 1e-5
LANES = 128
SUBLANES = 8
VMEM_LIMIT_BYTES = 48 * 1024 * 1024

D_MODEL = 1024
D_FF = 4 * D_MODEL

RW_HEADS, RW_HEAD = 8, 64
RW_DIM = RW_HEADS * RW_HEAD
RW_LR = 64 + 64 + 128
RW_COLS = 3 * RW_DIM + RW_LR
RW_GN_EPS = 64e-5
RW_CHUNK = 64

MB_HEADS, MB_HEADDIM, MB_GROUPS, MB_STATE = 16, 64, 2, 128
MB_HPG = MB_HEADS // MB_GROUPS
MB_DIM = MB_HEADS * MB_HEADDIM
MB_BC = 2 * MB_GROUPS * MB_STATE
MB_COLS_PAD = 2 * MB_DIM + MB_BC + LANES
MB_CHUNK = 128


def _cparams(*sem):
    return pltpu.CompilerParams(dimension_semantics=sem, vmem_limit_bytes=VMEM_LIMIT_BYTES)


def _bdot(a, b):
    return jnp.dot(a.astype(BF16), b.astype(BF16), preferred_element_type=F32)


def _bdot_nt(a, b):
    return lax.dot_general(a.astype(BF16), b.astype(BF16), (((1,), (1,)), ((), ())),
                           preferred_element_type=F32)


def _bdot_tn(a, b):
    return lax.dot_general(a.astype(BF16), b.astype(BF16), (((0,), (0,)), ((), ())),
                           preferred_element_type=F32)


def _split3(x):
    hi = x.astype(BF16)
    r = x - hi.astype(F32)
    mid = r.astype(BF16)
    lo = (r - mid.astype(F32)).astype(BF16)
    return hi, mid, lo


def _xdot_l(x, m):
    hi, mid, lo = _split3(x)
    dot = functools.partial(jnp.dot, preferred_element_type=F32)
    return dot(lo, m) + dot(mid, m) + dot(hi, m)


def _xdot_r(m, x):
    hi, mid, lo = _split3(x)
    dot = functools.partial(jnp.dot, preferred_element_type=F32)
    return dot(m, lo) + dot(m, mid) + dot(m, hi)


def _softplus(u):
    return jnp.maximum(u, 0.0) + jnp.log(1.0 + jnp.exp(-jnp.abs(u)))


def _sigmoid(u):
    return 1.0 / (1.0 + jnp.exp(-u))


def _rms(x, gain):
    return x * lax.rsqrt(jnp.mean(x * x, axis=-1, keepdims=True) + EPS) * gain


def _shifted(x, prev_row, next_row):
    n, w = x.shape
    rows = lax.broadcasted_iota(jnp.int32, (SUBLANES, w), 0)
    down = pltpu.roll(x, 1, 0)
    up = pltpu.roll(x, n - 1, 0)
    x_prev = jnp.concatenate([jnp.where(rows == 0, prev_row, down[:SUBLANES]), down[SUBLANES:]], axis=0)
    x_next = jnp.concatenate([up[:n - SUBLANES], jnp.where(rows == SUBLANES - 1, next_row, up[n - SUBLANES:])],
                             axis=0)
    return x_prev, x_next


HALO_ROWS = 2 * SUBLANES


def _halo_rows(xp_ref, xn_ref, tiles_per_seq):
    si = pl.program_id(0) % tiles_per_seq
    prev_row = jnp.where(si == 0, 0.0, xp_ref[...].astype(F32)[HALO_ROWS - 1:HALO_ROWS, :])
    next_row = jnp.where(si == tiles_per_seq - 1, 0.0, xn_ref[...].astype(F32)[0:1, :])
    return prev_row, next_row


def _halo_specs(ts, width, col_block, n_rows):
    per = ts // HALO_ROWS
    last = n_rows // HALO_ROWS - 1
    prev = pl.BlockSpec((HALO_ROWS, width), lambda i: (jnp.maximum(i * per - 1, 0), col_block))
    nxt = pl.BlockSpec((HALO_ROWS, width), lambda i: (jnp.minimum((i + 1) * per, last), col_block))
    return prev, nxt


def _scan_masks(n, period, sgn):
    ri = lax.broadcasted_iota(jnp.int32, (n, n), 0)
    ci = lax.broadcasted_iota(jnp.int32, (n, n), 1)
    delta = ((ci & (period - 1)) - (ri & (period - 1))) * sgn
    return delta < 0, delta <= 0, ri == ci


def _norm_mm_kernel(x_ref, g_ref, w_ref, o_ref, *tail_ref):
    xn = _rms(x_ref[...], g_ref[...])
    res = jnp.dot(xn.astype(BF16), w_ref[...], preferred_element_type=F32)
    if tail_ref:
        o_ref[...] = res[:, :-LANES].astype(o_ref.dtype)
        tail_ref[0][...] = res[:, -LANES:]
    else:
        o_ref[...] = res.astype(o_ref.dtype)


def _norm_matmul(h, gain, w, *, tm, tail=False, out_spec=None, out_shape=None):
    m, dm = h.shape
    n = w.shape[1]
    n_main = n - LANES if tail else n
    if out_spec is None:
        out_spec = pl.BlockSpec((tm, n_main), lambda i: (i, 0))
        out_shape = (m, n_main)
    out_specs = [out_spec]
    out_shapes = [jax.ShapeDtypeStruct(out_shape, BF16)]
    if tail:
        out_specs.append(pl.BlockSpec((tm, LANES), lambda i: (i, 0)))
        out_shapes.append(jax.ShapeDtypeStruct((m, LANES), F32))
    res = pl.pallas_call(
        _norm_mm_kernel,
        grid=(m // tm,),
        in_specs=[pl.BlockSpec((tm, dm), lambda i: (i, 0)),
                  pl.BlockSpec((1, dm), lambda i: (0, 0)),
                  pl.BlockSpec((dm, n), lambda i: (0, 0))],
        out_specs=out_specs,
        out_shape=out_shapes,
        compiler_params=_cparams("parallel"),
        name="norm_matmul",
    )(h, gain.reshape(1, dm), w)
    return res if tail else res[0]


def _mix_mlp_kernel(h_ref, y1_ref, y2_ref, wo1_ref, wo2_ref, gm_ref, w1_ref, w2_ref, gf_ref,
                    o_ref, h1_s, xn_s, acc_s, *, final_norm):
    kf = pl.program_id(1)

    @pl.when(kf == 0)
    def _():
        h1 = h_ref[...] + _bdot(y1_ref[...], wo1_ref[...]) + _bdot(y2_ref[...], wo2_ref[...])
        h1_s[...] = h1
        xn_s[...] = _rms(h1, gm_ref[...]).astype(BF16)
        acc_s[...] = jnp.zeros_like(acc_s)

    hid = jnp.dot(xn_s[...], w1_ref[...], preferred_element_type=F32)
    hid = jnp.square(jnp.maximum(hid, 0.0))
    acc_s[...] += jnp.dot(hid.astype(BF16), w2_ref[...], preferred_element_type=F32)

    @pl.when(kf == pl.num_programs(1) - 1)
    def _():
        out = h1_s[...] + acc_s[...]
        if final_norm:
            out = _rms(out, gf_ref[...])
        o_ref[...] = out


def _mix_mlp(h, y1, y1_spec, y2, wo1, wo2, g_mlp, w1, w2, g_final, *, tm, tf, final_norm):
    m, dm = h.shape
    ff = w1.shape[1]
    k2 = y2.shape[1]
    row = lambda i, k: (i, 0)
    fixed = lambda i, k: (0, 0)
    return pl.pallas_call(
        functools.partial(_mix_mlp_kernel, final_norm=final_norm),
        grid=(m // tm, ff // tf),
        in_specs=[pl.BlockSpec((tm, dm), row),
                  y1_spec,
                  pl.BlockSpec((tm, k2), row),
                  pl.BlockSpec(wo1.shape, fixed),
                  pl.BlockSpec(wo2.shape, fixed),
                  pl.BlockSpec((1, dm), fixed),
                  pl.BlockSpec((dm, tf), lambda i, k: (0, k)),
                  pl.BlockSpec((tf, dm), lambda i, k: (k, 0)),
                  pl.BlockSpec((1, dm), fixed)],
        out_specs=pl.BlockSpec((tm, dm), row),
        out_shape=jax.ShapeDtypeStruct((m, dm), F32),
        scratch_shapes=[pltpu.VMEM((tm, dm), F32), pltpu.VMEM((tm, dm), BF16), pltpu.VMEM((tm, dm), F32)],
        compiler_params=_cparams("parallel", "arbitrary"),
        name="mix_mlp",
    )(h, y1, y2, wo1, wo2, g_mlp.reshape(1, dm), w1, w2, g_final.reshape(1, dm))


def _dwconv_silu_kernel(x_ref, xp_ref, xn_ref, w_ref, b_ref, o_ref, *, tiles_per_seq):
    x = x_ref[...].astype(F32)
    prev_row, next_row = _halo_rows(xp_ref, xn_ref, tiles_per_seq)
    x_prev, x_next = _shifted(x, prev_row, next_row)
    w = w_ref[...]
    y = w[0:1] * x_prev + w[1:2] * x + w[2:3] * x_next + b_ref[...]
    o_ref[...] = (y * _sigmoid(y)).astype(o_ref.dtype)


def _dwconv_silu(cols, col_block, width, w, b, *, seq, ts):
    n_rows = cols.shape[0]
    prev, nxt = _halo_specs(ts, width, col_block, n_rows)
    return pl.pallas_call(
        functools.partial(_dwconv_silu_kernel, tiles_per_seq=seq // ts),
        grid=(n_rows // ts,),
        in_specs=[pl.BlockSpec((ts, width), lambda i: (i, col_block)), prev, nxt,
                  pl.BlockSpec((3, width), lambda i: (0, 0)),
                  pl.BlockSpec((1, width), lambda i: (0, 0))],
        out_specs=pl.BlockSpec((ts, width), lambda i: (i, 0)),
        out_shape=jax.ShapeDtypeStruct((n_rows, width), BF16),
        compiler_params=_cparams("parallel"),
        name="dwconv_silu",
    )(cols, cols, cols, w, b.reshape(1, width))


def _rw_prep_kernel(x_ref, xp_ref, xn_ref, mu_ref, w0_ref, w2_ref, a0_ref, a2_ref, g2_ref,
                    kk_ref, ka_ref, rk_ref, ones_ref,
                    r_o, k_o, v_o, kk_o, b_o, lw_o, bonus_o, g_o, *, tiles_per_seq):
    x = x_ref[...].astype(F32)
    prev_row, next_row = _halo_rows(xp_ref, xn_ref, tiles_per_seq)
    x_prev, x_next = _shifted(x, prev_row, next_row)
    mu = mu_ref[...]
    xs = x + mu[0:1] * (x_prev - x) + mu[1:2] * (x_next - x)
    r = xs[:, 0:RW_DIM]
    k = xs[:, RW_DIM:2 * RW_DIM]
    v = xs[:, 2 * RW_DIM:3 * RW_DIM]
    lr = xs[:, 3 * RW_DIM:3 * RW_DIM + LANES]
    g_lr = xs[:, 3 * RW_DIM + LANES:3 * RW_DIM + 2 * LANES]
    th = jnp.tanh(lr)
    for d in range(2):
        z = w0_ref[d:d + 1, :] + _bdot(th, w2_ref[d])
        lw_o[d] = -jnp.exp(-_softplus(-z) - 0.5)
    a_gate = _sigmoid(a0_ref[...] + _bdot(lr, a2_ref[...]))
    g_o[...] = _bdot(_sigmoid(g_lr), g2_ref[...])
    ones = ones_ref[...]
    kk = k * kk_ref[...]
    kk = kk * lax.rsqrt(jnp.maximum(_xdot_l(kk * kk, ones), 1e-12))
    k2 = k * (1.0 + (a_gate - 1.0) * ka_ref[...])
    r_o[...] = r.astype(BF16)
    k_o[...] = k2.astype(BF16)
    v_o[...] = v.astype(BF16)
    kk_o[...] = kk.astype(BF16)
    b_o[...] = (kk * a_gate).astype(BF16)
    bonus_o[...] = _xdot_l(r * k2 * rk_ref[...], ones) * v


def _rw_chunk_kernel(*refs, chunk):
    L = chunk
    ins = (refs[0:6], refs[6:12])
    y_refs = refs[12:14]
    s_ref = refs[14]

    @pl.when(pl.program_id(1) == 0)
    def _():
        s_ref[...] = jnp.zeros_like(s_ref)

    first = lax.broadcasted_iota(jnp.int32, (L, LANES), 1) < RW_HEAD

    def stack(t):
        return jnp.concatenate([jnp.where(first, t, 0.0), jnp.where(first, 0.0, t)], axis=0).astype(BF16)

    chains = []
    for dd in range(2):
        r_ref, k_ref, v_ref, kk_ref, b_ref, lw_ref = ins[dd]
        sgn = 1 - 2 * dd
        _, incl_l, _ = _scan_masks(L, L, sgn)
        tri = jnp.where(incl_l, 1.0, 0.0).astype(BF16)
        lw = lw_ref[0]
        c_incl = _xdot_r(tri, lw)
        c_tot = jnp.sum(lw, axis=0, keepdims=True)
        e_in = jnp.exp(c_incl)
        e_neg = jnp.exp(-c_incl)
        e_tot = jnp.exp(c_tot)
        e_rem = e_tot * e_neg
        kk = kk_ref[...]
        bb = b_ref[...]
        k = k_ref[...]
        v = v_ref[...]
        rt = r_ref[...] * e_in
        at = -kk * jnp.exp(c_incl - lw)
        bt = bb * e_neg
        kt = k * e_neg
        bh = bb * e_rem
        kh = k * e_rem
        strict, incl, eye = _scan_masks(2 * L, L, sgn)
        for p in range(RW_DIM // LANES):
            sl = slice(p * LANES, (p + 1) * LANES)
            chains.append(dict(
                dd=dd, p=p, sl=sl, strict=strict, incl=incl, eye=eye, e_tot=e_tot[:, sl],
                at=stack(at[:, sl]), rt=stack(rt[:, sl]), bt=stack(bt[:, sl]), kt=stack(kt[:, sl]),
                bh=stack(bh[:, sl]), kh=stack(kh[:, sl]), v=stack(v[:, sl]),
                s=s_ref[dd, p]))

    n2 = 2 * L
    for c in chains:
        ar = jnp.concatenate([c["at"], c["rt"]], axis=0)
        g = _bdot_nt(ar, jnp.concatenate([c["bt"], c["kt"]], axis=0))
        c["a_ab"] = jnp.where(c["strict"], g[:n2, :n2], 0.0)
        c["a_ak"] = jnp.where(c["strict"], g[:n2, n2:], 0.0)
        c["r_b"] = jnp.where(c["incl"], g[n2:, :n2], 0.0)
        c["r_k"] = jnp.where(c["incl"], g[n2:, n2:], 0.0)
        c["ar"] = ar
    n_sq = int(math.log2(L)) - 1
    for c in chains:
        c["inv"] = jnp.where(c["eye"], 1.0, 0.0) + c["a_ab"]
        c["pw"] = _bdot(c["a_ab"], c["a_ab"])
        lhs = jnp.concatenate([c["ar"], jnp.concatenate([c["a_ak"], c["r_k"]], axis=0).astype(BF16)], axis=1)
        c["wy"] = _bdot(lhs, jnp.concatenate([c["s"].T.astype(BF16), c["v"]], axis=0))
    for i in range(n_sq):
        for c in chains:
            if i + 1 < n_sq:
                both = _bdot(c["pw"], jnp.concatenate([c["inv"], c["pw"]], axis=1))
                c["inv"] = c["inv"] + both[:, :n2]
                c["pw"] = both[:, n2:]
            else:
                c["inv"] = c["inv"] + _bdot(c["pw"], c["inv"])
    for c in chains:
        c["u"] = _bdot(c["inv"], c["wy"][:n2])
    for c in chains:
        y_s = c["wy"][n2:] + _bdot(c["r_b"], c["u"])
        y_refs[c["dd"]][:, c["sl"]] = (y_s[:L] + y_s[L:]).astype(y_refs[c["dd"]].dtype)
        uv = jnp.concatenate([c["u"].astype(BF16), c["v"]], axis=0)
        s_ref[c["dd"], c["p"]] = c["s"] * c["e_tot"] + _bdot_tn(uv, jnp.concatenate([c["bh"], c["kh"]], axis=0))


def _rw_post_kernel(yf_ref, yb_ref, bonus_ref, g_ref, lnw_ref, ones_ref, o_ref):
    y = yf_ref[...].astype(F32) + yb_ref[...].astype(F32)
    ones = ones_ref[...]
    yc = y - _xdot_l(y, ones) * (1.0 / RW_HEAD)
    var = _xdot_l(yc * yc, ones) * (1.0 / RW_HEAD)
    yn = yc * lax.rsqrt(var + RW_GN_EPS) * lnw_ref[...]
    o_ref[...] = ((yn + bonus_ref[...]) * g_ref[...]).astype(o_ref.dtype)


def _rwkv7(cols, p, *, batch, seq, ts):
    n_rows = cols.shape[0]
    dim = RW_DIM
    row = lambda i: (i, 0)
    fixed2 = lambda i: (0, 0)
    fixed3 = lambda i: (0, 0, 0)
    prev, nxt = _halo_specs(ts, RW_COLS, 0, n_rows)
    ones = jnp.asarray(np.kron(np.eye(RW_HEADS), np.ones((RW_HEAD, RW_HEAD))), BF16)
    zeros = jnp.zeros((64, dim), F32)
    w2 = jnp.concatenate([p["w2"], jnp.broadcast_to(zeros, (2, 64, dim))], axis=1).astype(BF16)
    a2 = jnp.concatenate([zeros, p["a2"]], axis=0).astype(BF16)
    vec = lambda t: t.reshape(1, dim)
    tile = jax.ShapeDtypeStruct((n_rows, dim), F32)
    btile = jax.ShapeDtypeStruct((n_rows, dim), BF16)
    r, k, v, kk, b, lw, bonus, g = pl.pallas_call(
        functools.partial(_rw_prep_kernel, tiles_per_seq=seq // ts),
        grid=(n_rows // ts,),
        in_specs=[pl.BlockSpec((ts, RW_COLS), row), prev, nxt,
                  pl.BlockSpec((2, RW_COLS), fixed2),
                  pl.BlockSpec((2, dim), fixed2),
                  pl.BlockSpec((2, LANES, dim), fixed3),
                  pl.BlockSpec((1, dim), fixed2),
                  pl.BlockSpec((LANES, dim), fixed2),
                  pl.BlockSpec((LANES, dim), fixed2),
                  pl.BlockSpec((1, dim), fixed2),
                  pl.BlockSpec((1, dim), fixed2),
                  pl.BlockSpec((1, dim), fixed2),
                  pl.BlockSpec((dim, dim), fixed2)],
        out_specs=[pl.BlockSpec((ts, dim), row)] * 5
                  + [pl.BlockSpec((2, ts, dim), lambda i: (0, i, 0))]
                  + [pl.BlockSpec((ts, dim), row)] * 2,
        out_shape=[btile] * 5 + [jax.ShapeDtypeStruct((2, n_rows, dim), F32)] + [tile] * 2,
        compiler_params=_cparams("parallel"),
        name="rw_prep",
    )(cols, cols, cols, p["mu"], p["w0"], w2, vec(p["a0"]), a2, p["g2"].astype(BF16),
      vec(p["k_k"]), vec(p["k_a"]), vec(p["r_k"]), ones)

    L = RW_CHUNK
    nc = seq // L
    fwd = lambda bi, c: (bi * nc + c, 0)
    bwd = lambda bi, c: (bi * nc + nc - 1 - c, 0)
    y_fwd, y_bwd = pl.pallas_call(
        functools.partial(_rw_chunk_kernel, chunk=L),
        grid=(batch, nc),
        in_specs=[pl.BlockSpec((L, dim), fwd)] * 5 + [pl.BlockSpec((1, L, dim), lambda bi, c: (0, bi * nc + c, 0))]
                 + [pl.BlockSpec((L, dim), bwd)] * 5
                 + [pl.BlockSpec((1, L, dim), lambda bi, c: (1, bi * nc + nc - 1 - c, 0))],
        out_specs=[pl.BlockSpec((L, dim), fwd), pl.BlockSpec((L, dim), bwd)],
        out_shape=[btile, btile],
        scratch_shapes=[pltpu.VMEM((2, dim // LANES, LANES, LANES), F32)],
        compiler_params=_cparams("parallel", "arbitrary"),
        name="rw_chunk",
    )(r, k, v, kk, b, lw, r, k, v, kk, b, lw)

    return pl.pallas_call(
        _rw_post_kernel,
        grid=(n_rows // ts,),
        in_specs=[pl.BlockSpec((ts, dim), row), pl.BlockSpec((ts, dim), row),
                  pl.BlockSpec((ts, dim), row), pl.BlockSpec((ts, dim), row),
                  pl.BlockSpec((1, dim), fixed2), pl.BlockSpec((dim, dim), fixed2)],
        out_specs=pl.BlockSpec((ts, dim), row),
        out_shape=btile,
        compiler_params=_cparams("parallel"),
        name="rw_post",
    )(y_fwd, y_bwd, bonus, g, vec(p["ln_w"]), ones)


def _mb_ssd_kernel(*refs, chunk):
    L = chunk
    ins = (refs[0:5], refs[5:10])
    bias_c_ref, alog_c_ref, bias_r_ref, alog_r_ref, e1_ref = refs[10:15]
    y_refs = refs[15:17]
    st_ref = refs[17]

    @pl.when(pl.program_id(1) == 0)
    def _():
        st_ref[...] = jnp.zeros_like(st_ref)

    ri = lax.broadcasted_iota(jnp.int32, (L, L), 0)
    ci = lax.broadcasted_iota(jnp.int32, (L, L), 1)
    first = lax.broadcasted_iota(jnp.int32, (L, LANES), 1) < MB_HEADDIM
    gw = MB_HPG * MB_HEADDIM
    groups = []
    for dd in range(2):
        xs_ref, bc_ref, bt_ref, dtc_ref, dtr_ref = ins[dd]
        sgn = 1 - 2 * dd
        incl = (ci - ri) * sgn <= 0
        tri = jnp.where(incl, 1.0, 0.0).astype(BF16)
        tri_t = jnp.where((ri - ci) * sgn <= 0, 1.0, 0.0).astype(BF16)
        last = L - 1 if dd == 0 else 0
        dt_c = _softplus(dtc_ref[...] + bias_c_ref[...])
        cs_c = _xdot_r(tri, dt_c * (-jnp.exp(alog_c_ref[...])))
        cols2 = jnp.concatenate([jnp.exp(cs_c), dt_c * jnp.exp(cs_c[last:last + 1, :] - cs_c)], axis=0)
        hi = cols2.astype(BF16)
        mid = (cols2 - hi.astype(F32)).astype(BF16)
        full2 = jnp.dot(jnp.concatenate([hi, mid], axis=1), e1_ref[dd], preferred_element_type=F32)
        ecs = full2[:L]
        xw = (xs_ref[...] * full2[L:]).astype(BF16)
        etot = ecs[last:last + 1, :]
        xb = xs_ref[...].astype(BF16)
        dt_r = _softplus(dtr_ref[0] + bias_r_ref[dd])
        cs_r = _xdot_l(dt_r * (-jnp.exp(alog_r_ref[dd])), tri_t)
        for g in range(MB_GROUPS):
            gs = slice(g * gw, (g + 1) * gw)
            groups.append(dict(
                dd=dd, g=g, gs=gs, incl=incl, ecs=ecs[:, gs], etot=etot[:, gs], xw=xw[:, gs], xb=xb[:, gs],
                cs_c=cs_c, cs_r=cs_r, dt_r=dt_r,
                bg=bc_ref[:, g * MB_STATE:(g + 1) * MB_STATE].astype(BF16),
                cg=bc_ref[:, (MB_GROUPS + g) * MB_STATE:(MB_GROUPS + g + 1) * MB_STATE].astype(BF16),
                bt=bt_ref[0, g * MB_STATE:(g + 1) * MB_STATE, :],
                st=st_ref[dd, :, gs]))

    for c in groups:
        c["scores"] = _bdot_nt(c["cg"], c["bg"])
        c["y_off"] = _bdot(c["cg"], c["st"]) * c["ecs"]
        st_ref[c["dd"], :, c["gs"]] = c["st"] * c["etot"] + _bdot(c["bt"], c["xw"])
    for c in groups:
        pairs = []
        for j in range(MB_HPG // 2):
            xp = c["xb"][:, j * LANES:(j + 1) * LANES]
            halves = []
            for hh in range(2):
                h = c["g"] * MB_HPG + 2 * j + hh
                lane = MB_HEADS * c["dd"] + h
                b_t = jnp.broadcast_to(c["cs_c"][:, lane:lane + 1], (L, L))
                expo = jnp.minimum(b_t - c["cs_r"][h:h + 1, :], 0.0)
                m = jnp.where(c["incl"], c["scores"] * jnp.exp(expo) * c["dt_r"][h:h + 1, :], 0.0)
                halves.append(_bdot(m, xp))
            pairs.append(jnp.where(first, halves[0], halves[1]))
        y_refs[c["dd"]][:, c["gs"]] = (c["y_off"] + jnp.concatenate(pairs, axis=1)).astype(y_refs[c["dd"]].dtype)


def _mb_post_kernel(yf_ref, yb_ref, xs_ref, z_ref, d_ref, nw_ref, o_ref):
    y = yf_ref[...].astype(F32) + yb_ref[...].astype(F32) + d_ref[...] * xs_ref[...]
    z = z_ref[...].astype(F32)
    y = y * (z * _sigmoid(z))
    gw = MB_DIM // MB_GROUPS
    for g in range(MB_GROUPS):
        yg = y[:, g * gw:(g + 1) * gw]
        o_ref[:, g * gw:(g + 1) * gw] = (yg * lax.rsqrt(jnp.mean(yg * yg, axis=-1, keepdims=True) + EPS)
                                          * nw_ref[:, g * gw:(g + 1) * gw]).astype(o_ref.dtype)


def _mamba2(cols, dt_cols, p, *, batch, seq, ts):
    n_rows = cols.shape[0]
    xs = _dwconv_silu(cols, 1, MB_DIM, p["conv_w"][:, :MB_DIM], p["conv_b"][:MB_DIM], seq=seq, ts=ts)
    bc = _dwconv_silu(cols, 2 * MB_DIM // MB_BC, MB_BC, p["conv_w"][:, MB_DIM:], p["conv_b"][MB_DIM:],
                      seq=seq, ts=ts)
    L = MB_CHUNK
    nc = seq // L
    dt_raw = dt_cols[:, :2 * MB_HEADS]
    dt_rows = jnp.swapaxes(dt_raw.reshape(batch, seq, 2 * MB_HEADS), 1, 2)
    pad = jnp.zeros((LANES - 2 * MB_HEADS,), F32)
    bias_c = jnp.concatenate([p["dt_bias"].reshape(-1), pad]).reshape(1, LANES)
    alog_c = jnp.concatenate([p["A_log"].reshape(-1), pad]).reshape(1, LANES)
    bias_r = p["dt_bias"].reshape(2, MB_HEADS, 1)
    alog_r = p["A_log"].reshape(2, MB_HEADS, 1)
    e1 = np.zeros((2, 2 * LANES, MB_DIM), np.float32)
    for d in range(2):
        for h in range(MB_HEADS):
            e1[d, MB_HEADS * d + h, h * MB_HEADDIM:(h + 1) * MB_HEADDIM] = 1.0
            e1[d, LANES + MB_HEADS * d + h, h * MB_HEADDIM:(h + 1) * MB_HEADDIM] = 1.0
    b_t = jnp.swapaxes(bc[:, :MB_GROUPS * MB_STATE].reshape(batch, seq, MB_GROUPS * MB_STATE), 1, 2).astype(BF16)
    fixed = lambda bi, c: (0, 0)
    fixed3c = lambda bi, c: (0, 0, 0)

    def dir_specs(d):
        chunk_of = (lambda c: c) if d == 0 else (lambda c: nc - 1 - c)
        rows = lambda bi, c: (bi * nc + chunk_of(c), 0)
        return ([pl.BlockSpec((L, MB_DIM), rows),
                 pl.BlockSpec((L, MB_BC), rows),
                 pl.BlockSpec((1, MB_GROUPS * MB_STATE, L), lambda bi, c: (bi, 0, chunk_of(c))),
                 pl.BlockSpec((L, LANES), rows),
                 pl.BlockSpec((1, MB_HEADS, L), lambda bi, c: (bi, d, chunk_of(c)))],
                pl.BlockSpec((L, MB_DIM), rows))

    (in_f, out_f), (in_b, out_b) = dir_specs(0), dir_specs(1)
    y_tile = jax.ShapeDtypeStruct((n_rows, MB_DIM), BF16)
    y_fwd, y_bwd = pl.pallas_call(
        functools.partial(_mb_ssd_kernel, chunk=L),
        grid=(batch, nc),
        in_specs=in_f + in_b + [pl.BlockSpec((1, LANES), fixed), pl.BlockSpec((1, LANES), fixed),
                                pl.BlockSpec((2, MB_HEADS, 1), fixed3c), pl.BlockSpec((2, MB_HEADS, 1), fixed3c),
                                pl.BlockSpec((2, 2 * LANES, MB_DIM), fixed3c)],
        out_specs=[out_f, out_b],
        out_shape=[y_tile, y_tile],
        scratch_shapes=[pltpu.VMEM((2, MB_STATE, MB_DIM), F32)],
        compiler_params=_cparams("parallel", "arbitrary"),
        name="mb_ssd",
    )(xs, bc, b_t, dt_cols, dt_rows, xs, bc, b_t, dt_cols, dt_rows, bias_c, alog_c, bias_r, alog_r,
      jnp.asarray(e1, BF16))

    row = lambda i: (i, 0)
    fixed2 = lambda i: (0, 0)
    return pl.pallas_call(
        _mb_post_kernel,
        grid=(n_rows // ts,),
        in_specs=[pl.BlockSpec((ts, MB_DIM), row), pl.BlockSpec((ts, MB_DIM), row),
                  pl.BlockSpec((ts, MB_DIM), row),
                  pl.BlockSpec((ts, MB_DIM), row),
                  pl.BlockSpec((1, MB_DIM), fixed2), pl.BlockSpec((1, MB_DIM), fixed2)],
        out_specs=pl.BlockSpec((ts, MB_DIM), row),
        out_shape=y_tile,
        compiler_params=_cparams("parallel"),
        name="mb_post",
    )(y_fwd, y_bwd, xs, cols, jnp.repeat(p["D"], MB_HEADDIM).reshape(1, MB_DIM), p["norm_w"].reshape(1, MB_DIM))


S5_GROUP, S5_GROUPS, S5_STATE = 16, 32, 64
S5_DIM = S5_GROUP * S5_GROUPS
S5_GPB = LANES // S5_GROUP
S5_BLOCKS = S5_GROUPS // S5_GPB
S5_HALF = S5_GPB * S5_STATE
S5_STEPS = 32


def _s5_disc_kernel(ar_ref, ai_ref, ldt_ref, bre_ref, bim_ref, abr_o, abi_o, bbr_o, bbi_o):
    dt = jnp.exp(ldt_ref[0])
    ar = jnp.minimum(ar_ref[0], -1e-4)
    ai = ai_ref[0]
    mag = jnp.exp(dt * ar)
    abr = mag * jnp.cos(dt * ai)
    abi = mag * jnp.sin(dt * ai)
    den = ar * ar + ai * ai
    fr = ((abr - 1.0) * ar + abi * ai) / den
    fi = (abi * ar - (abr - 1.0) * ai) / den
    bre = bre_ref[...]
    bim = bim_ref[...]
    abr_o[0] = abr
    abi_o[0] = abi
    bbr_o[0] = fr * bre - fi * bim
    bbi_o[0] = fr * bim + fi * bre


def _s5_scan_kernel(uf_ref, ub_ref, bw_ref, cw_ref, lr_ref, li_ref, yf_ref, yb_ref, x_s, st_s, *, batch, steps):
    @pl.when(pl.program_id(0) == 0)
    def _():
        st_s[...] = jnp.zeros_like(st_s)

    u_refs = (uf_ref, ub_ref)
    y_refs = (yf_ref, yb_ref)
    width = 2 * S5_HALF
    for dd in range(2):
        u = u_refs[dd][...].astype(BF16)
        for j in range(S5_BLOCKS):
            x_s[dd, :, j * width:(j + 1) * width] = jnp.dot(u[:, j * LANES:(j + 1) * LANES], bw_ref[dd, j],
                                                             preferred_element_type=F32)
    for dd in range(2):
        for j in range(S5_BLOCKS):
            re = slice(j * width, j * width + S5_HALF)
            im = slice(j * width + S5_HALF, (j + 1) * width)
            lam_r = jnp.broadcast_to(lr_ref[dd, :, j * S5_HALF:(j + 1) * S5_HALF], (batch, S5_HALF))
            lam_i = jnp.broadcast_to(li_ref[dd, :, j * S5_HALF:(j + 1) * S5_HALF], (batch, S5_HALF))
            xr = st_s[dd, :, re]
            xi = st_s[dd, :, im]
            for i in range(steps):
                t = i if dd == 0 else steps - 1 - i
                rows = slice(t * batch, (t + 1) * batch)
                xr, xi = (lam_r * xr - lam_i * xi + x_s[dd, rows, re],
                          lam_r * xi + lam_i * xr + x_s[dd, rows, im])
                x_s[dd, rows, re] = xr
                x_s[dd, rows, im] = xi
            st_s[dd, :, re] = xr
            st_s[dd, :, im] = xi
    for dd in range(2):
        for j in range(S5_BLOCKS):
            y_refs[dd][:, j * LANES:(j + 1) * LANES] = jnp.dot(
                x_s[dd, :, j * width:(j + 1) * width].astype(BF16), cw_ref[dd, j],
                preferred_element_type=F32).astype(y_refs[dd].dtype)


def _s5_post_kernel(yf_ref, yb_ref, u_ref, d_ref, gw_ref, gb_ref, o_ref):
    y = d_ref[...] * u_ref[...] + yf_ref[...].astype(F32) + yb_ref[...].astype(F32)
    y = 0.5 * y * (1.0 + jnp.tanh(math.sqrt(2.0 / math.pi) * (y + 0.044715 * (y * y * y))))
    o_ref[...] = (y * _sigmoid(_bdot(y, gw_ref[...]) + gb_ref[...])).astype(o_ref.dtype)


def _s5(u, p, *, batch, seq):
    n_rows = u.shape[0]
    gp = S5_GROUPS * S5_STATE
    bc = lambda t: jnp.broadcast_to(t.reshape(2, gp, 1), (2, gp, S5_GROUP))
    ldt = jnp.broadcast_to(p["log_dt"][:, :, None, None], (2, S5_GROUPS, S5_STATE, S5_GROUP)).reshape(2, gp, S5_GROUP)
    per_dir = pl.BlockSpec((1, gp, S5_GROUP), lambda d: (d, 0, 0))
    shared = pl.BlockSpec((gp, S5_GROUP), lambda d: (0, 0))
    disc = jax.ShapeDtypeStruct((2, gp, S5_GROUP), F32)
    abr, abi, bbr, bbi = pl.pallas_call(
        _s5_disc_kernel,
        grid=(2,),
        in_specs=[per_dir, per_dir, per_dir, shared, shared],
        out_specs=[per_dir] * 4,
        out_shape=[disc] * 4,
        compiler_params=_cparams("parallel"),
        name="s5_disc",
    )(bc(p["A_re"]), bc(p["A_im"]), ldt, p["B_re"].reshape(gp, S5_GROUP), p["B_im"].reshape(gp, S5_GROUP))

    eye = jnp.eye(S5_GPB, dtype=F32)
    shp = (2, S5_BLOCKS, S5_GPB, S5_STATE, S5_GROUP)
    b_blk = lambda t: jnp.einsum("djgpm,gh->djgmhp", t.reshape(shp), eye).reshape(2, S5_BLOCKS, LANES, S5_HALF)
    bw = jnp.concatenate([b_blk(bbr), b_blk(bbi)], axis=-1).astype(BF16)
    cshp = (2, S5_BLOCKS, S5_GPB, S5_GROUP, S5_STATE)
    c_blk = lambda t: jnp.einsum("djgmp,gh->djgphm", t.reshape(cshp), eye).reshape(2, S5_BLOCKS, S5_HALF, LANES)
    cw = jnp.concatenate([c_blk(p["C_re"]), c_blk(-p["C_im"])], axis=2).astype(BF16)
    lam_r = abr[:, :, 0].reshape(2, 1, gp)
    lam_i = abi[:, :, 0].reshape(2, 1, gp)

    steps = min(S5_STEPS, seq)
    nc = seq // steps
    tr = steps * batch
    width = 2 * S5_HALF
    fwd = lambda c: (c, 0)
    bwd = lambda c: (nc - 1 - c, 0)
    fixed4 = lambda c: (0, 0, 0, 0)
    fixed3 = lambda c: (0, 0, 0)
    y_tile = jax.ShapeDtypeStruct((n_rows, S5_DIM), BF16)
    y_fwd, y_bwd = pl.pallas_call(
        functools.partial(_s5_scan_kernel, batch=batch, steps=steps),
        grid=(nc,),
        in_specs=[pl.BlockSpec((tr, S5_DIM), fwd), pl.BlockSpec((tr, S5_DIM), bwd),
                  pl.BlockSpec((2, S5_BLOCKS, LANES, width), fixed4),
                  pl.BlockSpec((2, S5_BLOCKS, width, LANES), fixed4),
                  pl.BlockSpec((2, 1, gp), fixed3), pl.BlockSpec((2, 1, gp), fixed3)],
        out_specs=[pl.BlockSpec((tr, S5_DIM), fwd), pl.BlockSpec((tr, S5_DIM), bwd)],
        out_shape=[y_tile, y_tile],
        scratch_shapes=[pltpu.VMEM((2, tr, S5_BLOCKS * width), F32),
                        pltpu.VMEM((2, batch, S5_BLOCKS * width), F32)],
        compiler_params=_cparams("arbitrary"),
        name="s5_scan",
    )(u, u, bw, cw, lam_r, lam_i)

    tp = min(512, n_rows)
    row = lambda i: (i, 0)
    fixed = lambda i: (0, 0)
    return pl.pallas_call(
        _s5_post_kernel,
        grid=(n_rows // tp,),
        in_specs=[pl.BlockSpec((tp, S5_DIM), row), pl.BlockSpec((tp, S5_DIM), row), pl.BlockSpec((tp, S5_DIM), row),
                  pl.BlockSpec((1, S5_DIM), fixed), pl.BlockSpec((S5_DIM, S5_DIM), fixed),
                  pl.BlockSpec((1, S5_DIM), fixed)],
        out_specs=pl.BlockSpec((tp, S5_DIM), row),
        out_shape=jax.ShapeDtypeStruct((n_rows, S5_DIM), BF16),
        compiler_params=_cparams("parallel"),
        name="s5_post",
    )(y_fwd, y_bwd, u, p["D"].reshape(1, S5_DIM), p["glu_w"].astype(BF16), p["glu_b"].reshape(1, S5_DIM))


ML_HEADS, ML_HEAD, ML_BLOCK = 8, 128, 4
ML_DIM = ML_HEADS * ML_HEAD
ML_COLS_PAD = 2 * ML_DIM + LANES
ML_CHUNK = 128
ML_PROJ = 256
NEG_BIG = -1e30


def _log_sigmoid(u):
    return -_softplus(-u)


def _ml_prep_kernel(x_ref, xp_ref, xn_ref, cw_ref, cb_ref, wq_ref, wk_ref, wv_ref,
                    xc_o, q_o, k_o, v_o, *, tiles_per_seq):
    x = x_ref[...].astype(F32)
    prev_row, next_row = _halo_rows(xp_ref, xn_ref, tiles_per_seq)
    x_prev, x_next = _shifted(x, prev_row, next_row)
    w = cw_ref[...]
    y = w[0:1] * x_prev + w[1:2] * x + w[2:3] * x_next + cb_ref[...]
    xc = y * _sigmoid(y)
    xc_o[...] = xc.astype(BF16)
    xcb = xc.astype(BF16)
    xb = x.astype(BF16)
    for j in range(ML_DIM // ML_PROJ):
        sl = slice(j * ML_PROJ, (j + 1) * ML_PROJ)
        q_o[:, sl] = jnp.dot(xcb[:, sl], wq_ref[j], preferred_element_type=F32).astype(BF16)
        k_o[:, sl] = (jnp.dot(xcb[:, sl], wk_ref[j], preferred_element_type=F32) * (ML_HEAD ** -0.5)).astype(BF16)
        v_o[:, sl] = jnp.dot(xb[:, sl], wv_ref[j], preferred_element_type=F32).astype(BF16)


def _ml_chunk_kernel(*refs, chunk):
    L = chunk
    ins = (refs[0:7], refs[7:14])
    bias_c_ref, ib_ref, fb_ref = refs[14:17]
    h_refs = refs[17:19]
    c_s, m_s = refs[19:21]

    @pl.when(pl.program_id(1) == 0)
    def _():
        c_s[...] = jnp.zeros_like(c_s)
        m_s[...] = jnp.zeros_like(m_s)

    lane = lax.broadcasted_iota(jnp.int32, (L, LANES), 1)
    ones_tile = jnp.ones((L, ML_HEAD), BF16)
    ri = lax.broadcasted_iota(jnp.int32, (L, L), 0)
    ci = lax.broadcasted_iota(jnp.int32, (L, L), 1)
    chains = []
    for dd in range(2):
        q_ref, k_ref, kt_ref, v_ref, gc_ref, gi_ref, gf_ref = ins[dd]
        sgn = 1 - 2 * dd
        incl = (ci - ri) * sgn <= 0
        tri = jnp.where(incl, 1.0, 0.0).astype(BF16)
        tri_t = jnp.where((ri - ci) * sgn <= 0, 1.0, 0.0).astype(BF16)
        gpre = gc_ref[...] + bias_c_ref[...]
        gcol = jnp.where(lane < 2 * ML_HEADS, gpre, _log_sigmoid(gpre))
        cs_c = _xdot_r(tri, gcol)
        li_r = gi_ref[0] + ib_ref[dd]
        lf_r = _log_sigmoid(gf_ref[0] + fb_ref[dd])
        b_r = _xdot_l(lf_r, tri_t)
        b_last = b_r[:, L - 1:L] if dd == 0 else b_r[:, 0:1]
        lw_r = b_last - b_r + li_r
        lw_max = jnp.max(lw_r, axis=-1, keepdims=True)
        for h in range(ML_HEADS):
            sl = slice(h * ML_HEAD, (h + 1) * ML_HEAD)
            jf = 2 * ML_HEADS + ML_HEADS * dd + h
            chains.append(dict(
                dd=dd, h=h, sl=sl, incl=incl, q=q_ref[:, sl], k=k_ref[:, sl], kt=kt_ref[0, sl, :],
                v_ext=jnp.concatenate([v_ref[:, sl], ones_tile], axis=1),
                b_t=jnp.broadcast_to(cs_c[:, jf:jf + 1], (L, LANES)),
                b_row=b_r[h:h + 1, :], li_row=li_r[h:h + 1, :], lw_row=lw_r[h:h + 1, :],
                bl=b_last[h:h + 1, :], lw_max=lw_max[h:h + 1, :],
                m_prev=m_s[dd, h:h + 1, 0:1], c_ext=c_s[dd, h]))

    for c in chains:
        log_d = jnp.where(c["incl"], c["b_t"] - c["b_row"] + c["li_row"], NEG_BIG)
        inter = c["b_t"] + c["m_prev"]
        m_t = jnp.maximum(jnp.broadcast_to(jnp.max(log_d, axis=-1, keepdims=True), (L, LANES)), inter)
        c["m_t"] = m_t
        c["dmat"] = jnp.exp(log_d - m_t)
        c["w_in"] = jnp.exp(inter - m_t)
        c["qk"] = _bdot_nt(c["q"], c["k"])
        c["qc"] = _bdot(c["q"], c["c_ext"])
        m_new = jnp.maximum(c["bl"] + c["m_prev"], c["lw_max"])
        wkt = c["kt"].astype(F32) * jnp.exp(c["lw_row"] - m_new)
        c["c_new"] = jnp.exp(c["bl"] + c["m_prev"] - m_new) * c["c_ext"] + _bdot(wkt, c["v_ext"])
        c["m_new"] = m_new
    for c in chains:
        w_in2 = jnp.concatenate([c["w_in"], c["w_in"]], axis=1)
        nd = _bdot(c["qk"] * c["dmat"], c["v_ext"]) + w_in2 * c["qc"]
        den = nd[:, ML_HEAD:]
        h_refs[c["dd"]][:, c["sl"]] = (nd[:, :ML_HEAD] / jnp.maximum(jnp.abs(den), jnp.exp(-c["m_t"]))).astype(BF16)
        c_s[c["dd"], c["h"]] = c["c_new"]
        m_s[c["dd"], c["h"]:c["h"] + 1, :] = jnp.broadcast_to(c["m_new"], (1, LANES))


def _ml_post_kernel(hf_ref, hb_ref, o_ref_in, xc_ref, nw_ref, skip_ref, out_ref):
    hsum = hf_ref[...].astype(F32) + hb_ref[...].astype(F32)
    for h in range(ML_HEADS):
        sl = slice(h * ML_HEAD, (h + 1) * ML_HEAD)
        x = hsum[:, sl]
        xc = x - jnp.mean(x, axis=-1, keepdims=True)
        hn = xc * lax.rsqrt(jnp.mean(xc * xc, axis=-1, keepdims=True) + EPS) * nw_ref[:, sl]
        out_ref[:, sl] = (_sigmoid(o_ref_in[:, sl].astype(F32)) * hn
                          + skip_ref[:, sl] * xc_ref[:, sl]).astype(out_ref.dtype)


def _mlstm(cols, gate_cols, p, *, batch, seq, ts):
    n_rows = cols.shape[0]
    dim = ML_DIM
    row = lambda i: (i, 0)
    fixed2 = lambda i: (0, 0)
    fixed3 = lambda i: (0, 0, 0)
    prev, nxt = _halo_specs(ts, dim, 0, n_rows)
    nblk = dim // ML_PROJ
    per = ML_PROJ // ML_BLOCK
    eye = jnp.eye(per, dtype=F32)
    blockdiag = lambda w: jnp.einsum("bjcd,jk->bjckd", w.reshape(nblk, per, ML_BLOCK, ML_BLOCK),
                                     eye).reshape(nblk, ML_PROJ, ML_PROJ).astype(BF16)
    wspec = pl.BlockSpec((nblk, ML_PROJ, ML_PROJ), fixed3)
    xc, q, k, v = pl.pallas_call(
        functools.partial(_ml_prep_kernel, tiles_per_seq=seq // ts),
        grid=(n_rows // ts,),
        in_specs=[pl.BlockSpec((ts, dim), row), prev, nxt,
                  pl.BlockSpec((3, dim), fixed2), pl.BlockSpec((1, dim), fixed2), wspec, wspec, wspec],
        out_specs=[pl.BlockSpec((ts, dim), row)] * 4,
        out_shape=[jax.ShapeDtypeStruct((n_rows, dim), BF16)] * 4,
        compiler_params=_cparams("parallel"),
        name="ml_prep",
    )(cols, cols, cols, p["conv_w"], p["conv_b"].reshape(1, dim),
      blockdiag(p["wq"]), blockdiag(p["wk"]), blockdiag(p["wv"]))

    L = min(ML_CHUNK, seq)
    nc = seq // L
    ng = 4 * ML_HEADS
    g_rows = jnp.swapaxes(gate_cols[:, :ng].reshape(batch, seq, ng), 1, 2)
    k_t = jnp.swapaxes(k.reshape(batch, seq, dim), 1, 2)
    pad = jnp.zeros((LANES - ng,), F32)
    bias_c = jnp.concatenate([p["i_b"].reshape(-1), p["f_b"].reshape(-1), pad]).reshape(1, LANES)
    ib = p["i_b"].reshape(2, ML_HEADS, 1)
    fb = p["f_b"].reshape(2, ML_HEADS, 1)
    fixed = lambda bi, c: (0, 0)
    fixed3c = lambda bi, c: (0, 0, 0)

    def dir_specs(d):
        chunk_of = (lambda c: c) if d == 0 else (lambda c: nc - 1 - c)
        rows = lambda bi, c: (bi * nc + chunk_of(c), 0)
        return ([pl.BlockSpec((L, dim), rows)] * 2
                + [pl.BlockSpec((1, dim, L), lambda bi, c: (bi, 0, chunk_of(c))),
                   pl.BlockSpec((L, dim), rows),
                   pl.BlockSpec((L, LANES), rows),
                   pl.BlockSpec((1, ML_HEADS, L), lambda bi, c: (bi, d, chunk_of(c))),
                   pl.BlockSpec((1, ML_HEADS, L), lambda bi, c: (bi, 2 + d, chunk_of(c)))],
                pl.BlockSpec((L, dim), rows))

    (in_f, out_f), (in_b, out_b) = dir_specs(0), dir_specs(1)
    h_tile = jax.ShapeDtypeStruct((n_rows, dim), BF16)
    h_fwd, h_bwd = pl.pallas_call(
        functools.partial(_ml_chunk_kernel, chunk=L),
        grid=(batch, nc),
        in_specs=in_f + in_b + [pl.BlockSpec((1, LANES), fixed),
                                pl.BlockSpec((2, ML_HEADS, 1), fixed3c), pl.BlockSpec((2, ML_HEADS, 1), fixed3c)],
        out_specs=[out_f, out_b],
        out_shape=[h_tile, h_tile],
        scratch_shapes=[pltpu.VMEM((2, ML_HEADS, ML_HEAD, 2 * ML_HEAD), F32),
                        pltpu.VMEM((2, ML_HEADS, LANES), F32)],
        compiler_params=_cparams("parallel", "arbitrary"),
        name="ml_chunk",
    )(q, k, k_t, v, gate_cols, g_rows, g_rows, q, k, k_t, v, gate_cols, g_rows, g_rows, bias_c, ib, fb)

    return pl.pallas_call(
        _ml_post_kernel,
        grid=(n_rows // ts,),
        in_specs=[pl.BlockSpec((ts, dim), row), pl.BlockSpec((ts, dim), row),
                  pl.BlockSpec((ts, dim), lambda i: (i, 1)),
                  pl.BlockSpec((ts, dim), row),
                  pl.BlockSpec((1, dim), fixed2), pl.BlockSpec((1, dim), fixed2)],
        out_specs=pl.BlockSpec((ts, dim), row),
        out_shape=h_tile,
        compiler_params=_cparams("parallel"),
        name="ml_post",
    )(h_fwd, h_bwd, cols, xc, p["norm_w"].reshape(1, dim), p["skip"].reshape(1, dim))


ROW_TILE = 512
MIXER_TILE = 512
FF_TILE = 2048


def _pad_cols(w, n):
    return jnp.pad(w, ((0, 0), (0, n - w.shape[1])))


def kernel(x, norm_mix, norm_mlp, norm_final, mlp_w1, mlp_w2, ab_w_in, ab_w_out, rw_mu, rw_w0, rw_w2, rw_a0, rw_a2, rw_g2, rw_k_k, rw_k_a, rw_r_k, rw_ln_w, mb_conv_w, mb_conv_b, mb_dt_bias, mb_A_log, mb_D, mb_norm_w, cd_w_in, cd_w_out, s5_A_re, s5_A_im, s5_log_dt, s5_B_re, s5_B_im, s5_C_re, s5_C_im, s5_D, s5_glu_w, s5_glu_b, ml_conv_w, ml_conv_b, ml_wq, ml_wk, ml_wv, ml_i_b, ml_f_b, ml_norm_w, ml_skip):
    batch, seq, dm = x.shape
    n_rows = batch * seq
    tm = min(ROW_TILE, seq)
    ts = min(MIXER_TILE, seq)
    tiles_per_seq = seq // tm
    h = x.reshape(n_rows, dm)
    depth = norm_mix.shape[0]
    for layer in range(depth):
        i = layer // 2
        last = layer == depth - 1
        if layer % 2 == 0:
            w_in = ab_w_in[i]
            rw_cols = _norm_matmul(h, norm_mix[layer], w_in[:, :RW_COLS].astype(BF16), tm=tm)
            mb_cols, mb_dt = _norm_matmul(h, norm_mix[layer], _pad_cols(w_in[:, RW_COLS:], MB_COLS_PAD).astype(BF16),
                                          tm=tm, tail=True)
            y1 = _rwkv7(rw_cols, dict(mu=rw_mu[i], w0=rw_w0[i], w2=rw_w2[i], a0=rw_a0[i], a2=rw_a2[i], g2=rw_g2[i],
                                      k_k=rw_k_k[i], k_a=rw_k_a[i], r_k=rw_r_k[i].reshape(-1), ln_w=rw_ln_w[i]),
                        batch=batch, seq=seq, ts=ts)
            y2 = _mamba2(mb_cols, mb_dt, dict(conv_w=mb_conv_w[i], conv_b=mb_conv_b[i], dt_bias=mb_dt_bias[i],
                                       A_log=mb_A_log[i], D=mb_D[i], norm_w=mb_norm_w[i]),
                         batch=batch, seq=seq, ts=ts)
            y1_spec = pl.BlockSpec((tm, RW_DIM), lambda r, k: (r, 0))
            w_out, k1 = ab_w_out[i], RW_DIM
        else:
            w_in = cd_w_in[i]
            tm_spec = pl.BlockSpec((tm, S5_DIM), lambda r: (r % tiles_per_seq, r // tiles_per_seq))
            s5_cols = _norm_matmul(h, norm_mix[layer], w_in[:, :S5_DIM].astype(BF16), tm=tm,
                                   out_spec=tm_spec, out_shape=(seq, batch * S5_DIM))
            ml_cols, ml_gates = _norm_matmul(h, norm_mix[layer], _pad_cols(w_in[:, S5_DIM:], ML_COLS_PAD).astype(BF16),
                                             tm=tm, tail=True)
            y1 = _s5(s5_cols.reshape(seq * batch, S5_DIM),
                     dict(A_re=s5_A_re[i], A_im=s5_A_im[i], log_dt=s5_log_dt[i], B_re=s5_B_re[i], B_im=s5_B_im[i],
                          C_re=s5_C_re[i], C_im=s5_C_im[i], D=s5_D[i], glu_w=s5_glu_w[i], glu_b=s5_glu_b[i]),
                     batch=batch, seq=seq).reshape(seq, batch * S5_DIM)
            y2 = _mlstm(ml_cols, ml_gates, dict(conv_w=ml_conv_w[i], conv_b=ml_conv_b[i], wq=ml_wq[i], wk=ml_wk[i], wv=ml_wv[i],
                                      i_b=ml_i_b[i], f_b=ml_f_b[i], norm_w=ml_norm_w[i], skip=ml_skip[i]),
                        batch=batch, seq=seq, ts=ts)
            y1_spec = pl.BlockSpec((tm, S5_DIM), lambda r, k: (r % tiles_per_seq, r // tiles_per_seq))
            w_out, k1 = cd_w_out[i], S5_DIM
        h = _mix_mlp(h, y1, y1_spec, y2, w_out[:k1].astype(BF16), w_out[k1:].astype(BF16), norm_mlp[layer],
                     mlp_w1[layer].astype(BF16), mlp_w2[layer].astype(BF16), norm_final,
                     tm=tm, tf=FF_TILE, final_norm=last)
    return h.reshape(batch, seq, dm)
```

```python
import functools
import math

import jax
import jax.numpy as jnp
import numpy as np
from jax import lax
from jax.experimental import pallas as pl
from jax.experimental.pallas import tpu as pltpu

F32 = jnp.float32
BF16 = jnp.bfloat16

EPS = 1e-5
LANES = 128
SUBLANES = 8
VMEM_LIMIT_BYTES = 48 * 1024 * 1024

D_MODEL = 1024
D_FF = 4 * D_MODEL

RW_HEADS, RW_HEAD = 8, 64
RW_DIM = RW_HEADS * RW_HEAD
RW_LR = 64 + 64 + 128
RW_COLS = 3 * RW_DIM + RW_LR
RW_GN_EPS = 64e-5
RW_CHUNK = 64
RW_BATCH_PER_STEP = 2

MB_HEADS, MB_HEADDIM, MB_GROUPS, MB_STATE = 16, 64, 2, 128
MB_HPG = MB_HEADS // MB_GROUPS
MB_DIM = MB_HEADS * MB_HEADDIM
MB_BC = 2 * MB_GROUPS * MB_STATE
MB_COLS_PAD = 2 * MB_DIM + MB_BC + LANES
MB_CHUNK = 128


def _cparams(*sem):
    return pltpu.CompilerParams(dimension_semantics=sem, vmem_limit_bytes=VMEM_LIMIT_BYTES)


def _bdot(a, b):
    return jnp.dot(a.astype(BF16), b.astype(BF16), preferred_element_type=F32)


def _bdot_nt(a, b):
    return lax.dot_general(a.astype(BF16), b.astype(BF16), (((1,), (1,)), ((), ())),
                           preferred_element_type=F32)


def _bdot_tn(a, b):
    return lax.dot_general(a.astype(BF16), b.astype(BF16), (((0,), (0,)), ((), ())),
                           preferred_element_type=F32)


def _split3(x):
    hi = x.astype(BF16)
    r = x - hi.astype(F32)
    mid = r.astype(BF16)
    lo = (r - mid.astype(F32)).astype(BF16)
    return hi, mid, lo


def _xdot_l(x, m):
    hi, mid, lo = _split3(x)
    dot = functools.partial(jnp.dot, preferred_element_type=F32)
    return dot(lo, m) + dot(mid, m) + dot(hi, m)


def _xdot_r(m, x):
    hi, mid, lo = _split3(x)
    dot = functools.partial(jnp.dot, preferred_element_type=F32)
    return dot(m, lo) + dot(m, mid) + dot(m, hi)


def _softplus(u):
    return jnp.maximum(u, 0.0) + jnp.log(1.0 + jnp.exp(-jnp.abs(u)))


def _sigmoid(u):
    return 1.0 / (1.0 + jnp.exp(-u))


def _rms(x, gain):
    return x * lax.rsqrt(jnp.mean(x * x, axis=-1, keepdims=True) + EPS) * gain


def _shifted(x, prev_row, next_row):
    n, w = x.shape
    rows = lax.broadcasted_iota(jnp.int32, (SUBLANES, w), 0)
    down = pltpu.roll(x, 1, 0)
    up = pltpu.roll(x, n - 1, 0)
    x_prev = jnp.concatenate([jnp.where(rows == 0, prev_row, down[:SUBLANES]), down[SUBLANES:]], axis=0)
    x_next = jnp.concatenate([up[:n - SUBLANES], jnp.where(rows == SUBLANES - 1, next_row, up[n - SUBLANES:])],
                             axis=0)
    return x_prev, x_next


HALO_ROWS = 2 * SUBLANES


def _halo_rows(xp_ref, xn_ref, tiles_per_seq):
    si = pl.program_id(0) % tiles_per_seq
    prev_row = jnp.where(si == 0, 0.0, xp_ref[...].astype(F32)[HALO_ROWS - 1:HALO_ROWS, :])
    next_row = jnp.where(si == tiles_per_seq - 1, 0.0, xn_ref[...].astype(F32)[0:1, :])
    return prev_row, next_row


def _halo_specs(ts, width, col_block, n_rows):
    per = ts // HALO_ROWS
    last = n_rows // HALO_ROWS - 1
    prev = pl.BlockSpec((HALO_ROWS, width), lambda i: (jnp.maximum(i * per - 1, 0), col_block))
    nxt = pl.BlockSpec((HALO_ROWS, width), lambda i: (jnp.minimum((i + 1) * per, last), col_block))
    return prev, nxt


def _scan_masks(n, period, sgn):
    ri = lax.broadcasted_iota(jnp.int32, (n, n), 0)
    ci = lax.broadcasted_iota(jnp.int32, (n, n), 1)
    delta = ((ci & (period - 1)) - (ri & (period - 1))) * sgn
    return delta < 0, delta <= 0, ri == ci


def _norm_mm_kernel(x_ref, g_ref, w_ref, *rest):
    xn = _rms(x_ref[...], g_ref[...]).astype(BF16)
    res = jnp.dot(xn, w_ref[...], preferred_element_type=F32)
    if len(rest) == 1:
        rest[0][...] = res.astype(rest[0].dtype)
    else:
        wt_ref, o_ref, tail_ref, tail_t_ref = rest
        o_ref[...] = res[:, :-LANES].astype(o_ref.dtype)
        tail_ref[...] = res[:, -LANES:]
        tail_t_ref[0] = lax.dot_general(wt_ref[...], xn, (((1,), (1,)), ((), ())), preferred_element_type=F32)


def _norm_matmul(h, gain, w, *, tm, seq=None, tail=False, out_spec=None, out_shape=None):
    m, dm = h.shape
    n = w.shape[1]
    n_main = n - LANES if tail else n
    if out_spec is None:
        out_spec = pl.BlockSpec((tm, n_main), lambda i: (i, 0))
        out_shape = (m, n_main)
    in_specs = [pl.BlockSpec((tm, dm), lambda i: (i, 0)),
                pl.BlockSpec((1, dm), lambda i: (0, 0)),
                pl.BlockSpec((dm, n), lambda i: (0, 0))]
    args = [h, gain.reshape(1, dm), w]
    out_specs = [out_spec]
    out_shapes = [jax.ShapeDtypeStruct(out_shape, BF16)]
    if tail:
        tps = seq // tm
        in_specs.append(pl.BlockSpec((LANES, dm), lambda i: (0, 0)))
        args.append(w[:, -LANES:].T)
        out_specs += [pl.BlockSpec((tm, LANES), lambda i: (i, 0)),
                      pl.BlockSpec((1, LANES, tm), lambda i: (i // tps, 0, i % tps))]
        out_shapes += [jax.ShapeDtypeStruct((m, LANES), F32), jax.ShapeDtypeStruct((m // seq, LANES, seq), F32)]
    res = pl.pallas_call(
        _norm_mm_kernel,
        grid=(m // tm,),
        in_specs=in_specs,
        out_specs=out_specs,
        out_shape=out_shapes,
        compiler_params=_cparams("parallel"),
        name="norm_matmul",
    )(*args)
    return res if tail else res[0]


def _mix_mlp_kernel(h_ref, y1_ref, y2_ref, wo1_ref, wo2_ref, gm_ref, w1_ref, w2_ref, gf_ref,
                    o_ref, h1_s, xn_s, acc_s, *, final_norm):
    kf = pl.program_id(1)

    @pl.when(kf == 0)
    def _():
        h1 = h_ref[...] + _bdot(y1_ref[...], wo1_ref[...]) + _bdot(y2_ref[...], wo2_ref[...])
        h1_s[...] = h1
        xn_s[...] = _rms(h1, gm_ref[...]).astype(BF16)
        acc_s[...] = jnp.zeros_like(acc_s)

    hid = jnp.dot(xn_s[...], w1_ref[...], preferred_element_type=F32)
    hid = jnp.square(jnp.maximum(hid, 0.0))
    acc_s[...] += jnp.dot(hid.astype(BF16), w2_ref[...], preferred_element_type=F32)

    @pl.when(kf == pl.num_programs(1) - 1)
    def _():
        out = h1_s[...] + acc_s[...]
        if final_norm:
            out = _rms(out, gf_ref[...])
        o_ref[...] = out


def _mix_mlp(h, y1, y1_spec, y2, wo1, wo2, g_mlp, w1, w2, g_final, *, tm, tf, final_norm):
    m, dm = h.shape
    ff = w1.shape[1]
    k2 = y2.shape[1]
    row = lambda i, k: (i, 0)
    fixed = lambda i, k: (0, 0)
    return pl.pallas_call(
        functools.partial(_mix_mlp_kernel, final_norm=final_norm),
        grid=(m // tm, ff // tf),
        in_specs=[pl.BlockSpec((tm, dm), row),
                  y1_spec,
                  pl.BlockSpec((tm, k2), row),
                  pl.BlockSpec(wo1.shape, fixed),
                  pl.BlockSpec(wo2.shape, fixed),
                  pl.BlockSpec((1, dm), fixed),
                  pl.BlockSpec((dm, tf), lambda i, k: (0, k)),
                  pl.BlockSpec((tf, dm), lambda i, k: (k, 0)),
                  pl.BlockSpec((1, dm), fixed)],
        out_specs=pl.BlockSpec((tm, dm), row),
        out_shape=jax.ShapeDtypeStruct((m, dm), F32),
        scratch_shapes=[pltpu.VMEM((tm, dm), F32), pltpu.VMEM((tm, dm), BF16), pltpu.VMEM((tm, dm), F32)],
        compiler_params=_cparams("parallel", "arbitrary"),
        name="mix_mlp",
    )(h, y1, y2, wo1, wo2, g_mlp.reshape(1, dm), w1, w2, g_final.reshape(1, dm))


def _dwconv_silu_kernel(x_ref, xp_ref, xn_ref, w_ref, b_ref, o_ref, *, tiles_per_seq):
    x = x_ref[...].astype(F32)
    prev_row, next_row = _halo_rows(xp_ref, xn_ref, tiles_per_seq)
    x_prev, x_next = _shifted(x, prev_row, next_row)
    w = w_ref[...]
    y = w[0:1] * x_prev + w[1:2] * x + w[2:3] * x_next + b_ref[...]
    o_ref[...] = (y * _sigmoid(y)).astype(o_ref.dtype)


def _dwconv_silu(cols, col_block, width, w, b, *, seq, ts):
    n_rows = cols.shape[0]
    prev, nxt = _halo_specs(ts, width, col_block, n_rows)
    return pl.pallas_call(
        functools.partial(_dwconv_silu_kernel, tiles_per_seq=seq // ts),
        grid=(n_rows // ts,),
        in_specs=[pl.BlockSpec((ts, width), lambda i: (i, col_block)), prev, nxt,
                  pl.BlockSpec((3, width), lambda i: (0, 0)),
                  pl.BlockSpec((1, width), lambda i: (0, 0))],
        out_specs=pl.BlockSpec((ts, width), lambda i: (i, 0)),
        out_shape=jax.ShapeDtypeStruct((n_rows, width), BF16),
        compiler_params=_cparams("parallel"),
        name="dwconv_silu",
    )(cols, cols, cols, w, b.reshape(1, width))


def _rw_prep_kernel(x_ref, xp_ref, xn_ref, mu_ref, w0_ref, w2_ref, a0_ref, a2_ref, g2_ref,
                    kk_ref, ka_ref, rk_ref, ones_ref,
                    r_o, k_o, v_o, kk_o, b_o, lw_o, bonus_o, g_o, *, tiles_per_seq):
    x = x_ref[...].astype(F32)
    prev_row, next_row = _halo_rows(xp_ref, xn_ref, tiles_per_seq)
    x_prev, x_next = _shifted(x, prev_row, next_row)
    mu = mu_ref[...]
    xs = x + mu[0:1] * (x_prev - x) + mu[1:2] * (x_next - x)
    r = xs[:, 0:RW_DIM]
    k = xs[:, RW_DIM:2 * RW_DIM]
    v = xs[:, 2 * RW_DIM:3 * RW_DIM]
    lr = xs[:, 3 * RW_DIM:3 * RW_DIM + LANES]
    g_lr = xs[:, 3 * RW_DIM + LANES:3 * RW_DIM + 2 * LANES]
    th = jnp.tanh(lr)
    for d in range(2):
        z = w0_ref[d:d + 1, :] + _bdot(th, w2_ref[d])
        lw_o[d] = -jnp.exp(-_softplus(-z) - 0.5)
    a_gate = _sigmoid(a0_ref[...] + _bdot(lr, a2_ref[...]))
    g_o[...] = _bdot(_sigmoid(g_lr), g2_ref[...])
    ones = ones_ref[...]
    kk = k * kk_ref[...]
    kk = kk * lax.rsqrt(jnp.maximum(_xdot_l(kk * kk, ones), 1e-12))
    k2 = k * (1.0 + (a_gate - 1.0) * ka_ref[...])
    r_o[...] = r.astype(BF16)
    k_o[...] = k2.astype(BF16)
    v_o[...] = v.astype(BF16)
    kk_o[...] = kk.astype(BF16)
    b_o[...] = (kk * a_gate).astype(BF16)
    bonus_o[...] = _xdot_l(r * k2 * rk_ref[...], ones) * v


def _rw_chunk_kernel(*refs, chunk):
    L = chunk
    ins = (refs[0:6], refs[6:12])
    y_refs = refs[12:14]
    s_ref = refs[14]

    @pl.when(pl.program_id(1) == 0)
    def _():
        s_ref[...] = jnp.zeros_like(s_ref)

    n2 = 2 * L
    lane = lax.broadcasted_iota(jnp.int32, (L, LANES), 1)
    first = lane < RW_HEAD
    eye_l = lane == lax.broadcasted_iota(jnp.int32, (L, LANES), 0)
    zeros_l = jnp.zeros((L, LANES), BF16)
    rows2 = lax.broadcasted_iota(jnp.int32, (n2, n2), 0)
    cols2 = lax.broadcasted_iota(jnp.int32, (n2, n2), 1)

    def top(t):
        return jnp.concatenate([t.astype(BF16), zeros_l], axis=0)

    def bottom(t):
        return jnp.concatenate([zeros_l, t.astype(BF16)], axis=0)

    pairs = []
    chains = []
    for bs, dd in [(bs, dd) for bs in range(y_refs[0].shape[0]) for dd in range(2)]:
        r_ref, k_ref, v_ref, kk_ref, b_ref, lw_ref = ins[dd]
        sgn = 1 - 2 * dd
        _, incl_l, _ = _scan_masks(L, L, sgn)
        tri = jnp.where(incl_l, 1.0, 0.0).astype(BF16)
        lw = lw_ref[0, bs]
        c_incl = _xdot_r(tri, lw)
        c_tot = jnp.sum(lw, axis=0, keepdims=True)
        e_in = jnp.exp(c_incl)
        e_neg = jnp.exp(-c_incl)
        e_tot = jnp.exp(c_tot)
        e_rem = e_tot * e_neg
        kk = kk_ref[bs]
        bb = b_ref[bs]
        k = k_ref[bs]
        v = v_ref[bs]
        rt = r_ref[bs] * e_in
        at = -kk * jnp.exp(c_incl - lw)
        bt = bb * e_neg
        kt = k * e_neg
        bh = bb * e_rem
        kh = k * e_rem
        delta = ((cols2 & (L - 1)) - (rows2 & (L - 1))) * sgn
        g_mask = delta < jnp.where(rows2 < L, 0, 1)
        for p in range(RW_DIM // LANES):
            sl = slice(p * LANES, (p + 1) * LANES)
            s = s_ref[bs, dd, p]
            pair = dict(bs=bs, dd=dd, p=p, sl=sl, s=s, s_t=s.T.astype(BF16), e_tot=e_tot[:, sl], heads=[])
            pairs.append(pair)
            for hh in range(2):
                keep = first if hh == 0 else lane >= RW_HEAD
                head = lambda t: jnp.where(keep, t[:, sl], 0.0).astype(BF16)
                c = dict(pair=pair, g_mask=g_mask,
                         ar=jnp.concatenate([head(at), head(rt)], axis=0),
                         bk=jnp.concatenate([head(bt), head(kt)], axis=0),
                         bh=head(bh), kh=head(kh), v=head(v))
                pair["heads"].append(c)
                chains.append(c)

    for c in chains:
        c["g"] = jnp.where(c["g_mask"], _bdot_nt(c["ar"], c["bk"]), 0.0)
    for c in chains:
        g_a = c["g"][:L]
        c["ym"] = jnp.where(first, jnp.where(eye_l, 1.0, 0.0), pltpu.roll(g_a, L, 1))
        c["q"] = _bdot(g_a, top(c["ym"]))
        lhs = jnp.concatenate([c["ar"], jnp.where(cols2 < L, 0.0, c["g"]).astype(BF16)], axis=1)
        c["wy"] = _bdot(lhs, jnp.concatenate([c["pair"]["s_t"], zeros_l, c["v"]], axis=0))
    n_stage = int(math.log2(L))
    for i in range(1, n_stage):
        for c in chains:
            c["ym"] = jnp.where(first, c["ym"], 0.0) + c["q"]
            c["q"] = _bdot(c["q"], bottom(c["ym"]))
    for c in chains:
        c["ym"] = jnp.where(first, c["ym"], 0.0) + c["q"]
        c["u"] = _bdot(c["ym"], top(c["wy"][:L]))
    for c in chains:
        c["y"] = c["wy"][L:] + _bdot(c["g"][L:], top(c["u"]))
    for pair in pairs:
        h0, h1 = pair["heads"]
        y_refs[pair["dd"]][pair["bs"], :, pair["sl"]] = (h0["y"] + h1["y"]).astype(y_refs[pair["dd"]].dtype)
        uv = jnp.concatenate([h0["u"].astype(BF16), h1["u"].astype(BF16), h0["v"], h1["v"]], axis=0)
        bk = jnp.concatenate([h0["bh"], h1["bh"], h0["kh"], h1["kh"]], axis=0)
        s_ref[pair["bs"], pair["dd"], pair["p"]] = pair["s"] * pair["e_tot"] + _bdot_tn(uv, bk)


def _rw_post_kernel(yf_ref, yb_ref, bonus_ref, g_ref, lnw_ref, ones_ref, o_ref):
    y = yf_ref[...].astype(F32) + yb_ref[...].astype(F32)
    ones = ones_ref[...]
    yc = y - _xdot_l(y, ones) * (1.0 / RW_HEAD)
    var = _xdot_l(yc * yc, ones) * (1.0 / RW_HEAD)
    yn = yc * lax.rsqrt(var + RW_GN_EPS) * lnw_ref[...]
    o_ref[...] = ((yn + bonus_ref[...]) * g_ref[...]).astype(o_ref.dtype)


def _rwkv7(cols, p, *, batch, seq, ts):
    n_rows = cols.shape[0]
    dim = RW_DIM
    row = lambda i: (i, 0)
    fixed2 = lambda i: (0, 0)
    fixed3 = lambda i: (0, 0, 0)
    prev, nxt = _halo_specs(ts, RW_COLS, 0, n_rows)
    ones = jnp.asarray(np.kron(np.eye(RW_HEADS), np.ones((RW_HEAD, RW_HEAD))), BF16)
    zeros = jnp.zeros((64, dim), F32)
    w2 = jnp.concatenate([p["w2"], jnp.broadcast_to(zeros, (2, 64, dim))], axis=1).astype(BF16)
    a2 = jnp.concatenate([zeros, p["a2"]], axis=0).astype(BF16)
    vec = lambda t: t.reshape(1, dim)
    tile = jax.ShapeDtypeStruct((n_rows, dim), F32)
    btile = jax.ShapeDtypeStruct((n_rows, dim), BF16)
    r, k, v, kk, b, lw, bonus, g = pl.pallas_call(
        functools.partial(_rw_prep_kernel, tiles_per_seq=seq // ts),
        grid=(n_rows // ts,),
        in_specs=[pl.BlockSpec((ts, RW_COLS), row), prev, nxt,
                  pl.BlockSpec((2, RW_COLS), fixed2),
                  pl.BlockSpec((2, dim), fixed2),
                  pl.BlockSpec((2, LANES, dim), fixed3),
                  pl.BlockSpec((1, dim), fixed2),
                  pl.BlockSpec((LANES, dim), fixed2),
                  pl.BlockSpec((LANES, dim), fixed2),
                  pl.BlockSpec((1, dim), fixed2),
                  pl.BlockSpec((1, dim), fixed2),
                  pl.BlockSpec((1, dim), fixed2),
                  pl.BlockSpec((dim, dim), fixed2)],
        out_specs=[pl.BlockSpec((ts, dim), row)] * 5
                  + [pl.BlockSpec((2, ts, dim), lambda i: (0, i, 0))]
                  + [pl.BlockSpec((ts, dim), row)] * 2,
        out_shape=[btile] * 5 + [jax.ShapeDtypeStruct((2, n_rows, dim), F32)] + [tile] * 2,
        compiler_params=_cparams("parallel"),
        name="rw_prep",
    )(cols, cols, cols, p["mu"], p["w0"], w2, vec(p["a0"]), a2, p["g2"].astype(BF16),
      vec(p["k_k"]), vec(p["k_a"]), vec(p["r_k"]), ones)

    L = RW_CHUNK
    nc = seq // L
    nb = min(RW_BATCH_PER_STEP, batch)
    fwd = lambda bi, c: (bi, c, 0)
    bwd = lambda bi, c: (bi, nc - 1 - c, 0)
    by_seq = lambda t: t.reshape(batch, seq, dim)
    r3, k3, v3, kk3, b3 = by_seq(r), by_seq(k), by_seq(v), by_seq(kk), by_seq(b)
    lw4 = lw.reshape(2, batch, seq, dim)
    y3 = jax.ShapeDtypeStruct((batch, seq, dim), BF16)
    y_fwd, y_bwd = pl.pallas_call(
        functools.partial(_rw_chunk_kernel, chunk=L),
        grid=(batch // nb, nc),
        in_specs=[pl.BlockSpec((nb, L, dim), fwd)] * 5
                 + [pl.BlockSpec((1, nb, L, dim), lambda bi, c: (0, bi, c, 0))]
                 + [pl.BlockSpec((nb, L, dim), bwd)] * 5
                 + [pl.BlockSpec((1, nb, L, dim), lambda bi, c: (1, bi, nc - 1 - c, 0))],
        out_specs=[pl.BlockSpec((nb, L, dim), fwd), pl.BlockSpec((nb, L, dim), bwd)],
        out_shape=[y3, y3],
        scratch_shapes=[pltpu.VMEM((nb, 2, dim // LANES, LANES, LANES), F32)],
        compiler_params=_cparams("parallel", "arbitrary"),
        name="rw_chunk",
    )(r3, k3, v3, kk3, b3, lw4, r3, k3, v3, kk3, b3, lw4)
    y_fwd = y_fwd.reshape(n_rows, dim)
    y_bwd = y_bwd.reshape(n_rows, dim)

    return pl.pallas_call(
        _rw_post_kernel,
        grid=(n_rows // ts,),
        in_specs=[pl.BlockSpec((ts, dim), row), pl.BlockSpec((ts, dim), row),
                  pl.BlockSpec((ts, dim), row), pl.BlockSpec((ts, dim), row),
                  pl.BlockSpec((1, dim), fixed2), pl.BlockSpec((dim, dim), fixed2)],
        out_specs=pl.BlockSpec((ts, dim), row),
        out_shape=btile,
        compiler_params=_cparams("parallel"),
        name="rw_post",
    )(y_fwd, y_bwd, bonus, g, vec(p["ln_w"]), ones)


def _mb_ssd_kernel(*refs, chunk):
    L = chunk
    ins = (refs[0:5], refs[5:10])
    bias_c_ref, alog_c_ref, bias_r_ref, alog_r_ref, e1_ref = refs[10:15]
    y_refs = refs[15:17]
    st_ref = refs[17]

    @pl.when(pl.program_id(1) == 0)
    def _():
        st_ref[...] = jnp.zeros_like(st_ref)

    ri = lax.broadcasted_iota(jnp.int32, (L, L), 0)
    ci = lax.broadcasted_iota(jnp.int32, (L, L), 1)
    first = lax.broadcasted_iota(jnp.int32, (L, LANES), 1) < MB_HEADDIM
    gw = MB_HPG * MB_HEADDIM
    groups = []
    for dd in range(2):
        xs_ref, bc_ref, bt_ref, dtc_ref, dtr_ref = ins[dd]
        sgn = 1 - 2 * dd
        incl = (ci - ri) * sgn <= 0
        tri = jnp.where(incl, 1.0, 0.0).astype(BF16)
        tri_t = jnp.where((ri - ci) * sgn <= 0, 1.0, 0.0).astype(BF16)
        last = L - 1 if dd == 0 else 0
        dt_c = _softplus(dtc_ref[...] + bias_c_ref[...])
        cs_c = _xdot_r(tri, dt_c * (-jnp.exp(alog_c_ref[...])))
        cols2 = jnp.concatenate([jnp.exp(cs_c), dt_c * jnp.exp(cs_c[last:last + 1, :] - cs_c)], axis=0)
        hi = cols2.astype(BF16)
        mid = (cols2 - hi.astype(F32)).astype(BF16)
        full2 = jnp.dot(jnp.concatenate([hi, mid], axis=1), e1_ref[dd], preferred_element_type=F32)
        ecs = full2[:L]
        xw = (xs_ref[...] * full2[L:]).astype(BF16)
        etot = ecs[last:last + 1, :]
        xb = xs_ref[...].astype(BF16)
        dt_r = _softplus(dtr_ref[0] + bias_r_ref[dd])
        cs_r = _xdot_l(dt_r * (-jnp.exp(alog_r_ref[dd])), tri_t)
        for g in range(MB_GROUPS):
            gs = slice(g * gw, (g + 1) * gw)
            groups.append(dict(
                dd=dd, g=g, gs=gs, incl=incl, ecs=ecs[:, gs], etot=etot[:, gs], xw=xw[:, gs], xb=xb[:, gs],
                cs_c=cs_c, cs_r=cs_r, dt_r=dt_r,
                bg=bc_ref[:, g * MB_STATE:(g + 1) * MB_STATE].astype(BF16),
                cg=bc_ref[:, (MB_GROUPS + g) * MB_STATE:(MB_GROUPS + g + 1) * MB_STATE].astype(BF16),
                bt=bt_ref[0, g * MB_STATE:(g + 1) * MB_STATE, :],
                st=st_ref[dd, :, gs]))

    for c in groups:
        c["scores"] = _bdot_nt(c["cg"], c["bg"])
        c["y_off"] = _bdot(c["cg"], c["st"]) * c["ecs"]
        st_ref[c["dd"], :, c["gs"]] = c["st"] * c["etot"] + _bdot(c["bt"], c["xw"])
    for c in groups:
        pairs = []
        for j in range(MB_HPG // 2):
            xp = c["xb"][:, j * LANES:(j + 1) * LANES]
            halves = []
            for hh in range(2):
                h = c["g"] * MB_HPG + 2 * j + hh
                lane = MB_HEADS * c["dd"] + h
                b_t = jnp.broadcast_to(c["cs_c"][:, lane:lane + 1], (L, L))
                expo = jnp.minimum(b_t - c["cs_r"][h:h + 1, :], 0.0)
                m = jnp.where(c["incl"], c["scores"] * jnp.exp(expo) * c["dt_r"][h:h + 1, :], 0.0)
                halves.append(_bdot(m, xp))
            pairs.append(jnp.where(first, halves[0], halves[1]))
        y_refs[c["dd"]][:, c["gs"]] = (c["y_off"] + jnp.concatenate(pairs, axis=1)).astype(y_refs[c["dd"]].dtype)


def _mb_post_kernel(yf_ref, yb_ref, xs_ref, z_ref, d_ref, nw_ref, o_ref):
    y = yf_ref[...].astype(F32) + yb_ref[...].astype(F32) + d_ref[...] * xs_ref[...]
    z = z_ref[...].astype(F32)
    y = y * (z * _sigmoid(z))
    gw = MB_DIM // MB_GROUPS
    for g in range(MB_GROUPS):
        yg = y[:, g * gw:(g + 1) * gw]
        o_ref[:, g * gw:(g + 1) * gw] = (yg * lax.rsqrt(jnp.mean(yg * yg, axis=-1, keepdims=True) + EPS)
                                          * nw_ref[:, g * gw:(g + 1) * gw]).astype(o_ref.dtype)


def _mamba2(cols, dt_cols, dt_rows, p, *, batch, seq, ts):
    n_rows = cols.shape[0]
    xs = _dwconv_silu(cols, 1, MB_DIM, p["conv_w"][:, :MB_DIM], p["conv_b"][:MB_DIM], seq=seq, ts=ts)
    bc = _dwconv_silu(cols, 2 * MB_DIM // MB_BC, MB_BC, p["conv_w"][:, MB_DIM:], p["conv_b"][MB_DIM:],
                      seq=seq, ts=ts)
    L = MB_CHUNK
    nc = seq // L
    pad = jnp.zeros((LANES - 2 * MB_HEADS,), F32)
    bias_c = jnp.concatenate([p["dt_bias"].reshape(-1), pad]).reshape(1, LANES)
    alog_c = jnp.concatenate([p["A_log"].reshape(-1), pad]).reshape(1, LANES)
    bias_r = p["dt_bias"].reshape(2, MB_HEADS, 1)
    alog_r = p["A_log"].reshape(2, MB_HEADS, 1)
    e1 = np.zeros((2, 2 * LANES, MB_DIM), np.float32)
    for d in range(2):
        for h in range(MB_HEADS):
            e1[d, MB_HEADS * d + h, h * MB_HEADDIM:(h + 1) * MB_HEADDIM] = 1.0
            e1[d, LANES + MB_HEADS * d + h, h * MB_HEADDIM:(h + 1) * MB_HEADDIM] = 1.0
    b_t = jnp.swapaxes(bc[:, :MB_GROUPS * MB_STATE].reshape(batch, seq, MB_GROUPS * MB_STATE), 1, 2).astype(BF16)
    fixed = lambda bi, c: (0, 0)
    fixed3c = lambda bi, c: (0, 0, 0)

    def dir_specs(d):
        chunk_of = (lambda c: c) if d == 0 else (lambda c: nc - 1 - c)
        rows = lambda bi, c: (bi * nc + chunk_of(c), 0)
        return ([pl.BlockSpec((L, MB_DIM), rows),
                 pl.BlockSpec((L, MB_BC), rows),
                 pl.BlockSpec((1, MB_GROUPS * MB_STATE, L), lambda bi, c: (bi, 0, chunk_of(c))),
                 pl.BlockSpec((L, LANES), rows),
                 pl.BlockSpec((1, MB_HEADS, L), lambda bi, c: (bi, d, chunk_of(c)))],
                pl.BlockSpec((L, MB_DIM), rows))

    (in_f, out_f), (in_b, out_b) = dir_specs(0), dir_specs(1)
    y_tile = jax.ShapeDtypeStruct((n_rows, MB_DIM), BF16)
    y_fwd, y_bwd = pl.pallas_call(
        functools.partial(_mb_ssd_kernel, chunk=L),
        grid=(batch, nc),
        in_specs=in_f + in_b + [pl.BlockSpec((1, LANES), fixed), pl.BlockSpec((1, LANES), fixed),
                                pl.BlockSpec((2, MB_HEADS, 1), fixed3c), pl.BlockSpec((2, MB_HEADS, 1), fixed3c),
                                pl.BlockSpec((2, 2 * LANES, MB_DIM), fixed3c)],
        out_specs=[out_f, out_b],
        out_shape=[y_tile, y_tile],
        scratch_shapes=[pltpu.VMEM((2, MB_STATE, MB_DIM), F32)],
        compiler_params=_cparams("parallel", "arbitrary"),
        name="mb_ssd",
    )(xs, bc, b_t, dt_cols, dt_rows, xs, bc, b_t, dt_cols, dt_rows, bias_c, alog_c, bias_r, alog_r,
      jnp.asarray(e1, BF16))

    row = lambda i: (i, 0)
    fixed2 = lambda i: (0, 0)
    return pl.pallas_call(
        _mb_post_kernel,
        grid=(n_rows // ts,),
        in_specs=[pl.BlockSpec((ts, MB_DIM), row), pl.BlockSpec((ts, MB_DIM), row),
                  pl.BlockSpec((ts, MB_DIM), row),
                  pl.BlockSpec((ts, MB_DIM), row),
                  pl.BlockSpec((1, MB_DIM), fixed2), pl.BlockSpec((1, MB_DIM), fixed2)],
        out_specs=pl.BlockSpec((ts, MB_DIM), row),
        out_shape=y_tile,
        compiler_params=_cparams("parallel"),
        name="mb_post",
    )(y_fwd, y_bwd, xs, cols, jnp.repeat(p["D"], MB_HEADDIM).reshape(1, MB_DIM), p["norm_w"].reshape(1, MB_DIM))


S5_GROUP, S5_GROUPS, S5_STATE = 16, 32, 64
S5_DIM = S5_GROUP * S5_GROUPS
S5_GPB = LANES // S5_GROUP
S5_BLOCKS = S5_GROUPS // S5_GPB
S5_HALF = S5_GPB * S5_STATE
S5_STEPS = 32


def _s5_disc_kernel(ar_ref, ai_ref, ldt_ref, bre_ref, bim_ref, abr_o, abi_o, bbr_o, bbi_o):
    dt = jnp.exp(ldt_ref[0])
    ar = jnp.minimum(ar_ref[0], -1e-4)
    ai = ai_ref[0]
    mag = jnp.exp(dt * ar)
    abr = mag * jnp.cos(dt * ai)
    abi = mag * jnp.sin(dt * ai)
    den = ar * ar + ai * ai
    fr = ((abr - 1.0) * ar + abi * ai) / den
    fi = (abi * ar - (abr - 1.0) * ai) / den
    bre = bre_ref[...]
    bim = bim_ref[...]
    abr_o[0] = abr
    abi_o[0] = abi
    bbr_o[0] = fr * bre - fi * bim
    bbi_o[0] = fr * bim + fi * bre


def _s5_scan_kernel(uf_ref, ub_ref, bw_ref, cw_ref, lr_ref, li_ref, yf_ref, yb_ref, x_s, st_s, *, batch, steps):
    @pl.when(pl.program_id(0) == 0)
    def _():
        st_s[...] = jnp.zeros_like(st_s)

    u_refs = (uf_ref, ub_ref)
    y_refs = (yf_ref, yb_ref)
    width = 2 * S5_HALF
    for dd in range(2):
        u = u_refs[dd][...].astype(BF16)
        for j in range(S5_BLOCKS):
            x_s[dd, :, j * width:(j + 1) * width] = jnp.dot(u[:, j * LANES:(j + 1) * LANES], bw_ref[dd, j],
                                                             preferred_element_type=F32)
    for dd in range(2):
        for j in range(S5_BLOCKS):
            re = slice(j * width, j * width + S5_HALF)
            im = slice(j * width + S5_HALF, (j + 1) * width)
            lam_r = jnp.broadcast_to(lr_ref[dd, :, j * S5_HALF:(j + 1) * S5_HALF], (batch, S5_HALF))
            lam_i = jnp.broadcast_to(li_ref[dd, :, j * S5_HALF:(j + 1) * S5_HALF], (batch, S5_HALF))
            xr = st_s[dd, :, re]
            xi = st_s[dd, :, im]
            for i in range(steps):
                t = i if dd == 0 else steps - 1 - i
                rows = slice(t * batch, (t + 1) * batch)
                xr, xi = (lam_r * xr - lam_i * xi + x_s[dd, rows, re],
                          lam_r * xi + lam_i * xr + x_s[dd, rows, im])
                x_s[dd, rows, re] = xr
                x_s[dd, rows, im] = xi
            st_s[dd, :, re] = xr
            st_s[dd, :, im] = xi
    for dd in range(2):
        for j in range(S5_BLOCKS):
            y_refs[dd][:, j * LANES:(j + 1) * LANES] = jnp.dot(
                x_s[dd, :, j * width:(j + 1) * width].astype(BF16), cw_ref[dd, j],
                preferred_element_type=F32).astype(y_refs[dd].dtype)


def _s5_post_kernel(yf_ref, yb_ref, u_ref, d_ref, gw_ref, gb_ref, o_ref):
    y = d_ref[...] * u_ref[...] + yf_ref[...].astype(F32) + yb_ref[...].astype(F32)
    y = 0.5 * y * (1.0 + jnp.tanh(math.sqrt(2.0 / math.pi) * (y + 0.044715 * (y * y * y))))
    o_ref[...] = (y * _sigmoid(_bdot(y, gw_ref[...]) + gb_ref[...])).astype(o_ref.dtype)


def _s5(u, p, *, batch, seq):
    n_rows = u.shape[0]
    gp = S5_GROUPS * S5_STATE
    bc = lambda t: jnp.broadcast_to(t.reshape(2, gp, 1), (2, gp, S5_GROUP))
    ldt = jnp.broadcast_to(p["log_dt"][:, :, None, None], (2, S5_GROUPS, S5_STATE, S5_GROUP)).reshape(2, gp, S5_GROUP)
    per_dir = pl.BlockSpec((1, gp, S5_GROUP), lambda d: (d, 0, 0))
    shared = pl.BlockSpec((gp, S5_GROUP), lambda d: (0, 0))
    disc = jax.ShapeDtypeStruct((2, gp, S5_GROUP), F32)
    abr, abi, bbr, bbi = pl.pallas_call(
        _s5_disc_kernel,
        grid=(2,),
        in_specs=[per_dir, per_dir, per_dir, shared, shared],
        out_specs=[per_dir] * 4,
        out_shape=[disc] * 4,
        compiler_params=_cparams("parallel"),
        name="s5_disc",
    )(bc(p["A_re"]), bc(p["A_im"]), ldt, p["B_re"].reshape(gp, S5_GROUP), p["B_im"].reshape(gp, S5_GROUP))

    eye = jnp.eye(S5_GPB, dtype=F32)
    shp = (2, S5_BLOCKS, S5_GPB, S5_STATE, S5_GROUP)
    b_blk = lambda t: jnp.einsum("djgpm,gh->djgmhp", t.reshape(shp), eye).reshape(2, S5_BLOCKS, LANES, S5_HALF)
    bw = jnp.concatenate([b_blk(bbr), b_blk(bbi)], axis=-1).astype(BF16)
    cshp = (2, S5_BLOCKS, S5_GPB, S5_GROUP, S5_STATE)
    c_blk = lambda t: jnp.einsum("djgmp,gh->djgphm", t.reshape(cshp), eye).reshape(2, S5_BLOCKS, S5_HALF, LANES)
    cw = jnp.concatenate([c_blk(p["C_re"]), c_blk(-p["C_im"])], axis=2).astype(BF16)
    lam_r = abr[:, :, 0].reshape(2, 1, gp)
    lam_i = abi[:, :, 0].reshape(2, 1, gp)

    steps = min(S5_STEPS, seq)
    nc = seq // steps
    tr = steps * batch
    width = 2 * S5_HALF
    fwd = lambda c: (c, 0)
    bwd = lambda c: (nc - 1 - c, 0)
    fixed4 = lambda c: (0, 0, 0, 0)
    fixed3 = lambda c: (0, 0, 0)
    y_tile = jax.ShapeDtypeStruct((n_rows, S5_DIM), BF16)
    y_fwd, y_bwd = pl.pallas_call(
        functools.partial(_s5_scan_kernel, batch=batch, steps=steps),
        grid=(nc,),
        in_specs=[pl.BlockSpec((tr, S5_DIM), fwd), pl.BlockSpec((tr, S5_DIM), bwd),
                  pl.BlockSpec((2, S5_BLOCKS, LANES, width), fixed4),
                  pl.BlockSpec((2, S5_BLOCKS, width, LANES), fixed4),
                  pl.BlockSpec((2, 1, gp), fixed3), pl.BlockSpec((2, 1, gp), fixed3)],
        out_specs=[pl.BlockSpec((tr, S5_DIM), fwd), pl.BlockSpec((tr, S5_DIM), bwd)],
        out_shape=[y_tile, y_tile],
        scratch_shapes=[pltpu.VMEM((2, tr, S5_BLOCKS * width), F32),
                        pltpu.VMEM((2, batch, S5_BLOCKS * width), F32)],
        compiler_params=_cparams("arbitrary"),
        name="s5_scan",
    )(u, u, bw, cw, lam_r, lam_i)

    tp = min(512, n_rows)
    row = lambda i: (i, 0)
    fixed = lambda i: (0, 0)
    return pl.pallas_call(
        _s5_post_kernel,
        grid=(n_rows // tp,),
        in_specs=[pl.BlockSpec((tp, S5_DIM), row), pl.BlockSpec((tp, S5_DIM), row), pl.BlockSpec((tp, S5_DIM), row),
                  pl.BlockSpec((1, S5_DIM), fixed), pl.BlockSpec((S5_DIM, S5_DIM), fixed),
                  pl.BlockSpec((1, S5_DIM), fixed)],
        out_specs=pl.BlockSpec((tp, S5_DIM), row),
        out_shape=jax.ShapeDtypeStruct((n_rows, S5_DIM), BF16),
        compiler_params=_cparams("parallel"),
        name="s5_post",
    )(y_fwd, y_bwd, u, p["D"].reshape(1, S5_DIM), p["glu_w"].astype(BF16), p["glu_b"].reshape(1, S5_DIM))


ML_HEADS, ML_HEAD, ML_BLOCK = 8, 128, 4
ML_DIM = ML_HEADS * ML_HEAD
ML_COLS_PAD = 2 * ML_DIM + LANES
ML_CHUNK = 128
ML_PROJ = 256
NEG_BIG = -1e30


def _log_sigmoid(u):
    return -_softplus(-u)


def _ml_prep_kernel(x_ref, xp_ref, xn_ref, cw_ref, cb_ref, wq_ref, wkt_ref, wv_ref,
                    xc_o, q_o, kt_o, v_o, *, tiles_per_seq):
    x = x_ref[...].astype(F32)
    prev_row, next_row = _halo_rows(xp_ref, xn_ref, tiles_per_seq)
    x_prev, x_next = _shifted(x, prev_row, next_row)
    w = cw_ref[...]
    y = w[0:1] * x_prev + w[1:2] * x + w[2:3] * x_next + cb_ref[...]
    xc = y * _sigmoid(y)
    xc_o[...] = xc.astype(BF16)
    xcb = xc.astype(BF16)
    xb = x.astype(BF16)
    for j in range(ML_DIM // ML_PROJ):
        sl = slice(j * ML_PROJ, (j + 1) * ML_PROJ)
        q_o[:, sl] = jnp.dot(xcb[:, sl], wq_ref[j], preferred_element_type=F32).astype(BF16)
        kt_o[0, sl, :] = (lax.dot_general(wkt_ref[j], xcb[:, sl], (((1,), (1,)), ((), ())),
                                          preferred_element_type=F32) * (ML_HEAD ** -0.5)).astype(BF16)
        v_o[:, sl] = jnp.dot(xb[:, sl], wv_ref[j], preferred_element_type=F32).astype(BF16)


def _ml_chunk_kernel(*refs, chunk):
    L = chunk
    ins = (refs[0:6], refs[6:12])
    bias_c_ref, ib_ref, fb_ref = refs[12:15]
    h_refs = refs[15:17]
    c_s, m_s = refs[17:19]

    @pl.when(pl.program_id(1) == 0)
    def _():
        c_s[...] = jnp.zeros_like(c_s)
        m_s[...] = jnp.zeros_like(m_s)

    lane = lax.broadcasted_iota(jnp.int32, (L, LANES), 1)
    ones_tile = jnp.ones((L, ML_HEAD), BF16)
    ri = lax.broadcasted_iota(jnp.int32, (L, L), 0)
    ci = lax.broadcasted_iota(jnp.int32, (L, L), 1)
    chains = []
    for dd in range(2):
        q_ref, kt_ref, v_ref, gc_ref, gi_ref, gf_ref = ins[dd]
        sgn = 1 - 2 * dd
        incl = (ci - ri) * sgn <= 0
        tri = jnp.where(incl, 1.0, 0.0).astype(BF16)
        tri_t = jnp.where((ri - ci) * sgn <= 0, 1.0, 0.0).astype(BF16)
        gpre = gc_ref[...] + bias_c_ref[...]
        gcol = jnp.where(lane < 2 * ML_HEADS, gpre, _log_sigmoid(gpre))
        cs_c = _xdot_r(tri, gcol)
        li_r = gi_ref[0] + ib_ref[dd]
        lf_r = _log_sigmoid(gf_ref[0] + fb_ref[dd])
        b_r = _xdot_l(lf_r, tri_t)
        b_last = b_r[:, L - 1:L] if dd == 0 else b_r[:, 0:1]
        lw_r = b_last - b_r + li_r
        lw_max = jnp.max(lw_r, axis=-1, keepdims=True)
        for h in range(ML_HEADS):
            sl = slice(h * ML_HEAD, (h + 1) * ML_HEAD)
            jf = 2 * ML_HEADS + ML_HEADS * dd + h
            chains.append(dict(
                dd=dd, h=h, sl=sl, incl=incl, q=q_ref[:, sl], kt=kt_ref[0, sl, :],
                v_ext=jnp.concatenate([v_ref[:, sl], ones_tile], axis=1),
                b_t=jnp.broadcast_to(cs_c[:, jf:jf + 1], (L, LANES)),
                b_row=b_r[h:h + 1, :], li_row=li_r[h:h + 1, :], lw_row=lw_r[h:h + 1, :],
                bl=b_last[h:h + 1, :], lw_max=lw_max[h:h + 1, :],
                m_prev=m_s[dd, h:h + 1, 0:1], c_ext=c_s[dd, h]))

    for c in chains:
        log_d = jnp.where(c["incl"], c["b_t"] - c["b_row"] + c["li_row"], NEG_BIG)
        inter = c["b_t"] + c["m_prev"]
        m_t = jnp.maximum(jnp.broadcast_to(jnp.max(log_d, axis=-1, keepdims=True), (L, LANES)), inter)
        c["m_t"] = m_t
        c["dmat"] = jnp.exp(log_d - m_t)
        c["w_in"] = jnp.exp(inter - m_t)
        c["qk"] = _bdot(c["q"], c["kt"])
        c["qc"] = _bdot(c["q"], c["c_ext"])
        m_new = jnp.maximum(c["bl"] + c["m_prev"], c["lw_max"])
        wkt = c["kt"].astype(F32) * jnp.exp(c["lw_row"] - m_new)
        c["c_new"] = jnp.exp(c["bl"] + c["m_prev"] - m_new) * c["c_ext"] + _bdot(wkt, c["v_ext"])
        c["m_new"] = m_new
    for c in chains:
        w_in2 = jnp.concatenate([c["w_in"], c["w_in"]], axis=1)
        nd = _bdot(c["qk"] * c["dmat"], c["v_ext"]) + w_in2 * c["qc"]
        den = nd[:, ML_HEAD:]
        h_refs[c["dd"]][:, c["sl"]] = (nd[:, :ML_HEAD] / jnp.maximum(jnp.abs(den), jnp.exp(-c["m_t"]))).astype(BF16)
        c_s[c["dd"], c["h"]] = c["c_new"]
        m_s[c["dd"], c["h"]:c["h"] + 1, :] = jnp.broadcast_to(c["m_new"], (1, LANES))


def _ml_post_kernel(hf_ref, hb_ref, o_ref_in, xc_ref, nw_ref, skip_ref, out_ref):
    hsum = hf_ref[...].astype(F32) + hb_ref[...].astype(F32)
    for h in range(ML_HEADS):
        sl = slice(h * ML_HEAD, (h + 1) * ML_HEAD)
        x = hsum[:, sl]
        xc = x - jnp.mean(x, axis=-1, keepdims=True)
        hn = xc * lax.rsqrt(jnp.mean(xc * xc, axis=-1, keepdims=True) + EPS) * nw_ref[:, sl]
        out_ref[:, sl] = (_sigmoid(o_ref_in[:, sl].astype(F32)) * hn
                          + skip_ref[:, sl] * xc_ref[:, sl]).astype(out_ref.dtype)


def _mlstm(cols, gate_cols, g_rows, p, *, batch, seq, ts):
    n_rows = cols.shape[0]
    dim = ML_DIM
    row = lambda i: (i, 0)
    fixed2 = lambda i: (0, 0)
    fixed3 = lambda i: (0, 0, 0)
    prev, nxt = _halo_specs(ts, dim, 0, n_rows)
    nblk = dim // ML_PROJ
    per = ML_PROJ // ML_BLOCK
    eye = jnp.eye(per, dtype=F32)
    blockdiag = lambda w: jnp.einsum("bjcd,jk->bjckd", w.reshape(nblk, per, ML_BLOCK, ML_BLOCK),
                                     eye).reshape(nblk, ML_PROJ, ML_PROJ).astype(BF16)
    wspec = pl.BlockSpec((nblk, ML_PROJ, ML_PROJ), fixed3)
    tps = seq // ts
    tile = jax.ShapeDtypeStruct((n_rows, dim), BF16)
    xc, q, k_t, v = pl.pallas_call(
        functools.partial(_ml_prep_kernel, tiles_per_seq=tps),
        grid=(n_rows // ts,),
        in_specs=[pl.BlockSpec((ts, dim), row), prev, nxt,
                  pl.BlockSpec((3, dim), fixed2), pl.BlockSpec((1, dim), fixed2), wspec, wspec, wspec],
        out_specs=[pl.BlockSpec((ts, dim), row)] * 2
                  + [pl.BlockSpec((1, dim, ts), lambda i: (i // tps, 0, i % tps)), pl.BlockSpec((ts, dim), row)],
        out_shape=[tile] * 2 + [jax.ShapeDtypeStruct((batch, dim, seq), BF16), tile],
        compiler_params=_cparams("parallel"),
        name="ml_prep",
    )(cols, cols, cols, p["conv_w"], p["conv_b"].reshape(1, dim),
      blockdiag(p["wq"]), jnp.swapaxes(blockdiag(p["wk"]), 1, 2), blockdiag(p["wv"]))

    L = min(ML_CHUNK, seq)
    nc = seq // L
    ng = 4 * ML_HEADS
    pad = jnp.zeros((LANES - ng,), F32)
    bias_c = jnp.concatenate([p["i_b"].reshape(-1), p["f_b"].reshape(-1), pad]).reshape(1, LANES)
    ib = p["i_b"].reshape(2, ML_HEADS, 1)
    fb = p["f_b"].reshape(2, ML_HEADS, 1)
    fixed = lambda bi, c: (0, 0)
    fixed3c = lambda bi, c: (0, 0, 0)

    def dir_specs(d):
        chunk_of = (lambda c: c) if d == 0 else (lambda c: nc - 1 - c)
        rows = lambda bi, c: (bi * nc + chunk_of(c), 0)
        return ([pl.BlockSpec((L, dim), rows),
                 pl.BlockSpec((1, dim, L), lambda bi, c: (bi, 0, chunk_of(c))),
                   pl.BlockSpec((L, dim), rows),
                   pl.BlockSpec((L, LANES), rows),
                   pl.BlockSpec((1, ML_HEADS, L), lambda bi, c: (bi, d, chunk_of(c))),
                   pl.BlockSpec((1, ML_HEADS, L), lambda bi, c: (bi, 2 + d, chunk_of(c)))],
                pl.BlockSpec((L, dim), rows))

    (in_f, out_f), (in_b, out_b) = dir_specs(0), dir_specs(1)
    h_tile = jax.ShapeDtypeStruct((n_rows, dim), BF16)
    h_fwd, h_bwd = pl.pallas_call(
        functools.partial(_ml_chunk_kernel, chunk=L),
        grid=(batch, nc),
        in_specs=in_f + in_b + [pl.BlockSpec((1, LANES), fixed),
                                pl.BlockSpec((2, ML_HEADS, 1), fixed3c), pl.BlockSpec((2, ML_HEADS, 1), fixed3c)],
        out_specs=[out_f, out_b],
        out_shape=[h_tile, h_tile],
        scratch_shapes=[pltpu.VMEM((2, ML_HEADS, ML_HEAD, 2 * ML_HEAD), F32),
                        pltpu.VMEM((2, ML_HEADS, LANES), F32)],
        compiler_params=_cparams("parallel", "arbitrary"),
        name="ml_chunk",
    )(q, k_t, v, gate_cols, g_rows, g_rows, q, k_t, v, gate_cols, g_rows, g_rows, bias_c, ib, fb)

    return pl.pallas_call(
        _ml_post_kernel,
        grid=(n_rows // ts,),
        in_specs=[pl.BlockSpec((ts, dim), row), pl.BlockSpec((ts, dim), row),
                  pl.BlockSpec((ts, dim), lambda i: (i, 1)),
                  pl.BlockSpec((ts, dim), row),
                  pl.BlockSpec((1, dim), fixed2), pl.BlockSpec((1, dim), fixed2)],
        out_specs=pl.BlockSpec((ts, dim), row),
        out_shape=h_tile,
        compiler_params=_cparams("parallel"),
        name="ml_post",
    )(h_fwd, h_bwd, cols, xc, p["norm_w"].reshape(1, dim), p["skip"].reshape(1, dim))


ROW_TILE = 512
MIXER_TILE = 512
FF_TILE = 2048


def _pad_cols(w, n):
    return jnp.pad(w, ((0, 0), (0, n - w.shape[1])))


def kernel(x, norm_mix, norm_mlp, norm_final, mlp_w1, mlp_w2, ab_w_in, ab_w_out, rw_mu, rw_w0, rw_w2, rw_a0, rw_a2, rw_g2, rw_k_k, rw_k_a, rw_r_k, rw_ln_w, mb_conv_w, mb_conv_b, mb_dt_bias, mb_A_log, mb_D, mb_norm_w, cd_w_in, cd_w_out, s5_A_re, s5_A_im, s5_log_dt, s5_B_re, s5_B_im, s5_C_re, s5_C_im, s5_D, s5_glu_w, s5_glu_b, ml_conv_w, ml_conv_b, ml_wq, ml_wk, ml_wv, ml_i_b, ml_f_b, ml_norm_w, ml_skip):
    batch, seq, dm = x.shape
    n_rows = batch * seq
    tm = min(ROW_TILE, seq)
    ts = min(MIXER_TILE, seq)
    tiles_per_seq = seq // tm
    h = x.reshape(n_rows, dm)
    depth = norm_mix.shape[0]
    for layer in range(depth):
        i = layer // 2
        last = layer == depth - 1
        if layer % 2 == 0:
            w_in = ab_w_in[i]
            rw_cols = _norm_matmul(h, norm_mix[layer], w_in[:, :RW_COLS].astype(BF16), tm=tm)
            mb_cols, mb_dt, mb_dt_t = _norm_matmul(h, norm_mix[layer],
                                                   _pad_cols(w_in[:, RW_COLS:], MB_COLS_PAD).astype(BF16),
                                                   tm=tm, seq=seq, tail=True)
            y1 = _rwkv7(rw_cols, dict(mu=rw_mu[i], w0=rw_w0[i], w2=rw_w2[i], a0=rw_a0[i], a2=rw_a2[i], g2=rw_g2[i],
                                      k_k=rw_k_k[i], k_a=rw_k_a[i], r_k=rw_r_k[i].reshape(-1), ln_w=rw_ln_w[i]),
                        batch=batch, seq=seq, ts=ts)
            y2 = _mamba2(mb_cols, mb_dt, mb_dt_t, dict(conv_w=mb_conv_w[i], conv_b=mb_conv_b[i], dt_bias=mb_dt_bias[i],
                                       A_log=mb_A_log[i], D=mb_D[i], norm_w=mb_norm_w[i]),
                         batch=batch, seq=seq, ts=ts)
            y1_spec = pl.BlockSpec((tm, RW_DIM), lambda r, k: (r, 0))
            w_out, k1 = ab_w_out[i], RW_DIM
        else:
            w_in = cd_w_in[i]
            tm_spec = pl.BlockSpec((tm, S5_DIM), lambda r: (r % tiles_per_seq, r // tiles_per_seq))
            s5_cols = _norm_matmul(h, norm_mix[layer], w_in[:, :S5_DIM].astype(BF16), tm=tm,
                                   out_spec=tm_spec, out_shape=(seq, batch * S5_DIM))
            ml_cols, ml_gates, ml_gates_t = _norm_matmul(h, norm_mix[layer],
                                                         _pad_cols(w_in[:, S5_DIM:], ML_COLS_PAD).astype(BF16),
                                                         tm=tm, seq=seq, tail=True)
            y1 = _s5(s5_cols.reshape(seq * batch, S5_DIM),
                     dict(A_re=s5_A_re[i], A_im=s5_A_im[i], log_dt=s5_log_dt[i], B_re=s5_B_re[i], B_im=s5_B_im[i],
                          C_re=s5_C_re[i], C_im=s5_C_im[i], D=s5_D[i], glu_w=s5_glu_w[i], glu_b=s5_glu_b[i]),
                     batch=batch, seq=seq).reshape(seq, batch * S5_DIM)
            y2 = _mlstm(ml_cols, ml_gates, ml_gates_t, dict(conv_w=ml_conv_w[i], conv_b=ml_conv_b[i], wq=ml_wq[i], wk=ml_wk[i], wv=ml_wv[i],
                                      i_b=ml_i_b[i], f_b=ml_f_b[i], norm_w=ml_norm_w[i], skip=ml_skip[i]),
                        batch=batch, seq=seq, ts=ts)
            y1_spec = pl.BlockSpec((tm, S5_DIM), lambda r, k: (r % tiles_per_seq, r // tiles_per_seq))
            w_out, k1 = cd_w_out[i], S5_DIM
        h = _mix_mlp(h, y1, y1_spec, y2, w_out[:k1].astype(BF16), w_out[k1:].astype(BF16), norm_mlp[layer],
                     mlp_w1[layer].astype(BF16), mlp_w2[layer].astype(BF16), norm_final,
                     tm=tm, tf=FF_TILE, final_norm=last)
    return h.reshape(batch, seq, dm)
```

```python
import functools
import math

import jax
import jax.numpy as jnp
import numpy as np
from jax import lax
from jax.experimental import pallas as pl
from jax.experimental.pallas import tpu as pltpu

F32 = jnp.float32
BF16 = jnp.bfloat16

EPS = 1e-5
LANES = 128
SUBLANES = 8
VMEM_LIMIT_BYTES = 48 * 1024 * 1024

D_MODEL = 1024
D_FF = 4 * D_MODEL

RW_HEADS, RW_HEAD = 8, 64
RW_DIM = RW_HEADS * RW_HEAD
RW_LR = 64 + 64 + 128
RW_COLS = 3 * RW_DIM + RW_LR
RW_GN_EPS = 64e-5
RW_CHUNK = 64
RW_BATCH_PER_STEP = 2

MB_HEADS, MB_HEADDIM, MB_GROUPS, MB_STATE = 16, 64, 2, 128
MB_HPG = MB_HEADS // MB_GROUPS
MB_DIM = MB_HEADS * MB_HEADDIM
MB_BC = 2 * MB_GROUPS * MB_STATE
MB_COLS_PAD = 2 * MB_DIM + MB_BC + LANES
MB_CHUNK = 128
MB_BATCH_PER_STEP = 2


def _cparams(*sem):
    return pltpu.CompilerParams(dimension_semantics=sem, vmem_limit_bytes=VMEM_LIMIT_BYTES)


def _bdot(a, b):
    return jnp.dot(a.astype(BF16), b.astype(BF16), preferred_element_type=F32)


def _bdot_nt(a, b):
    return lax.dot_general(a.astype(BF16), b.astype(BF16), (((1,), (1,)), ((), ())),
                           preferred_element_type=F32)


def _bdot_tn(a, b):
    return lax.dot_general(a.astype(BF16), b.astype(BF16), (((0,), (0,)), ((), ())),
                           preferred_element_type=F32)


def _split3(x):
    hi = x.astype(BF16)
    r = x - hi.astype(F32)
    mid = r.astype(BF16)
    lo = (r - mid.astype(F32)).astype(BF16)
    return hi, mid, lo


def _xdot_l(x, m):
    hi, mid, lo = _split3(x)
    dot = functools.partial(jnp.dot, preferred_element_type=F32)
    return dot(lo, m) + dot(mid, m) + dot(hi, m)


def _xdot_r(m, x):
    hi, mid, lo = _split3(x)
    dot = functools.partial(jnp.dot, preferred_element_type=F32)
    return dot(m, lo) + dot(m, mid) + dot(m, hi)


def _softplus(u):
    return jnp.maximum(u, 0.0) + jnp.log(1.0 + jnp.exp(-jnp.abs(u)))


def _sigmoid(u):
    return 1.0 / (1.0 + jnp.exp(-u))


def _rms(x, gain):
    return x * lax.rsqrt(jnp.mean(x * x, axis=-1, keepdims=True) + EPS) * gain


def _shifted(x, prev_row, next_row):
    n, w = x.shape
    rows = lax.broadcasted_iota(jnp.int32, (SUBLANES, w), 0)
    down = pltpu.roll(x, 1, 0)
    up = pltpu.roll(x, n - 1, 0)
    x_prev = jnp.concatenate([jnp.where(rows == 0, prev_row, down[:SUBLANES]), down[SUBLANES:]], axis=0)
    x_next = jnp.concatenate([up[:n - SUBLANES], jnp.where(rows == SUBLANES - 1, next_row, up[n - SUBLANES:])],
                             axis=0)
    return x_prev, x_next


HALO_ROWS = 2 * SUBLANES


def _halo_rows(xp_ref, xn_ref, tiles_per_seq):
    si = pl.program_id(0) % tiles_per_seq
    prev_row = jnp.where(si == 0, 0.0, xp_ref[...].astype(F32)[HALO_ROWS - 1:HALO_ROWS, :])
    next_row = jnp.where(si == tiles_per_seq - 1, 0.0, xn_ref[...].astype(F32)[0:1, :])
    return prev_row, next_row


def _halo_specs(ts, width, col_block, n_rows):
    per = ts // HALO_ROWS
    last = n_rows // HALO_ROWS - 1
    prev = pl.BlockSpec((HALO_ROWS, width), lambda i: (jnp.maximum(i * per - 1, 0), col_block))
    nxt = pl.BlockSpec((HALO_ROWS, width), lambda i: (jnp.minimum((i + 1) * per, last), col_block))
    return prev, nxt


def _scan_masks(n, period, sgn):
    ri = lax.broadcasted_iota(jnp.int32, (n, n), 0)
    ci = lax.broadcasted_iota(jnp.int32, (n, n), 1)
    delta = ((ci & (period - 1)) - (ri & (period - 1))) * sgn
    return delta < 0, delta <= 0, ri == ci


def _norm_mm_kernel(x_ref, g_ref, w_ref, *rest):
    xn = _rms(x_ref[...], g_ref[...]).astype(BF16)
    res = jnp.dot(xn, w_ref[...], preferred_element_type=F32)
    if len(rest) == 1:
        rest[0][...] = res.astype(rest[0].dtype)
    else:
        wt_ref, o_ref, tail_ref, tail_t_ref = rest
        o_ref[...] = res[:, :-LANES].astype(o_ref.dtype)
        tail_ref[...] = res[:, -LANES:]
        tail_t_ref[0] = lax.dot_general(wt_ref[...], xn, (((1,), (1,)), ((), ())), preferred_element_type=F32)


def _norm_matmul(h, gain, w, *, tm, seq=None, tail=False, out_spec=None, out_shape=None):
    m, dm = h.shape
    n = w.shape[1]
    n_main = n - LANES if tail else n
    if out_spec is None:
        out_spec = pl.BlockSpec((tm, n_main), lambda i: (i, 0))
        out_shape = (m, n_main)
    in_specs = [pl.BlockSpec((tm, dm), lambda i: (i, 0)),
                pl.BlockSpec((1, dm), lambda i: (0, 0)),
                pl.BlockSpec((dm, n), lambda i: (0, 0))]
    args = [h, gain.reshape(1, dm), w]
    out_specs = [out_spec]
    out_shapes = [jax.ShapeDtypeStruct(out_shape, BF16)]
    if tail:
        tps = seq // tm
        in_specs.append(pl.BlockSpec((LANES, dm), lambda i: (0, 0)))
        args.append(w[:, -LANES:].T)
        out_specs += [pl.BlockSpec((tm, LANES), lambda i: (i, 0)),
                      pl.BlockSpec((1, LANES, tm), lambda i: (i // tps, 0, i % tps))]
        out_shapes += [jax.ShapeDtypeStruct((m, LANES), F32), jax.ShapeDtypeStruct((m // seq, LANES, seq), F32)]
    res = pl.pallas_call(
        _norm_mm_kernel,
        grid=(m // tm,),
        in_specs=in_specs,
        out_specs=out_specs,
        out_shape=out_shapes,
        compiler_params=_cparams("parallel"),
        name="norm_matmul",
    )(*args)
    return res if tail else res[0]


def _mix_mlp_kernel(h_ref, y1_ref, y2_ref, wo1_ref, wo2_ref, gm_ref, w1_ref, w2_ref, gf_ref,
                    o_ref, h1_s, xn_s, acc_s, *, final_norm):
    kf = pl.program_id(1)

    @pl.when(kf == 0)
    def _():
        h1 = h_ref[...] + _bdot(y1_ref[...], wo1_ref[...]) + _bdot(y2_ref[...], wo2_ref[...])
        h1_s[...] = h1
        xn_s[...] = _rms(h1, gm_ref[...]).astype(BF16)
        acc_s[...] = jnp.zeros_like(acc_s)

    hid = jnp.dot(xn_s[...], w1_ref[...], preferred_element_type=F32)
    hid = jnp.square(jnp.maximum(hid, 0.0))
    acc_s[...] += jnp.dot(hid.astype(BF16), w2_ref[...], preferred_element_type=F32)

    @pl.when(kf == pl.num_programs(1) - 1)
    def _():
        out = h1_s[...] + acc_s[...]
        if final_norm:
            out = _rms(out, gf_ref[...])
        o_ref[...] = out


def _mix_mlp(h, y1, y1_spec, y2, wo1, wo2, g_mlp, w1, w2, g_final, *, tm, tf, final_norm):
    m, dm = h.shape
    ff = w1.shape[1]
    k2 = y2.shape[1]
    row = lambda i, k: (i, 0)
    fixed = lambda i, k: (0, 0)
    return pl.pallas_call(
        functools.partial(_mix_mlp_kernel, final_norm=final_norm),
        grid=(m // tm, ff // tf),
        in_specs=[pl.BlockSpec((tm, dm), row),
                  y1_spec,
                  pl.BlockSpec((tm, k2), row),
                  pl.BlockSpec(wo1.shape, fixed),
                  pl.BlockSpec(wo2.shape, fixed),
                  pl.BlockSpec((1, dm), fixed),
                  pl.BlockSpec((dm, tf), lambda i, k: (0, k)),
                  pl.BlockSpec((tf, dm), lambda i, k: (k, 0)),
                  pl.BlockSpec((1, dm), fixed)],
        out_specs=pl.BlockSpec((tm, dm), row),
        out_shape=jax.ShapeDtypeStruct((m, dm), F32),
        scratch_shapes=[pltpu.VMEM((tm, dm), F32), pltpu.VMEM((tm, dm), BF16), pltpu.VMEM((tm, dm), F32)],
        compiler_params=_cparams("parallel", "arbitrary"),
        name="mix_mlp",
    )(h, y1, y2, wo1, wo2, g_mlp.reshape(1, dm), w1, w2, g_final.reshape(1, dm))


def _dwconv_silu_kernel(x_ref, xp_ref, xn_ref, w_ref, b_ref, *rest, tiles_per_seq):
    x = x_ref[...].astype(F32)
    prev_row, next_row = _halo_rows(xp_ref, xn_ref, tiles_per_seq)
    x_prev, x_next = _shifted(x, prev_row, next_row)
    w = w_ref[...]
    y = w[0:1] * x_prev + w[1:2] * x + w[2:3] * x_next + b_ref[...]
    out = (y * _sigmoid(y)).astype(BF16)
    if len(rest) == 1:
        rest[0][...] = out
    else:
        eye_ref, o_ref, ot_ref = rest
        o_ref[...] = out
        n = eye_ref.shape[0]
        ot_ref[0] = lax.dot_general(eye_ref[...], out[:, :n], (((1,), (1,)), ((), ())),
                                    preferred_element_type=F32).astype(BF16)


def _dwconv_silu(cols, col_block, width, w, b, *, seq, ts, t_cols=0):
    n_rows = cols.shape[0]
    prev, nxt = _halo_specs(ts, width, col_block, n_rows)
    in_specs = [pl.BlockSpec((ts, width), lambda i: (i, col_block)), prev, nxt,
                pl.BlockSpec((3, width), lambda i: (0, 0)),
                pl.BlockSpec((1, width), lambda i: (0, 0))]
    args = [cols, cols, cols, w, b.reshape(1, width)]
    out_specs = [pl.BlockSpec((ts, width), lambda i: (i, 0))]
    out_shapes = [jax.ShapeDtypeStruct((n_rows, width), BF16)]
    if t_cols:
        tps = seq // ts
        in_specs.append(pl.BlockSpec((t_cols, t_cols), lambda i: (0, 0)))
        args.append(jnp.eye(t_cols, dtype=BF16))
        out_specs.append(pl.BlockSpec((1, t_cols, ts), lambda i: (i // tps, 0, i % tps)))
        out_shapes.append(jax.ShapeDtypeStruct((n_rows // seq, t_cols, seq), BF16))
    res = pl.pallas_call(
        functools.partial(_dwconv_silu_kernel, tiles_per_seq=seq // ts),
        grid=(n_rows // ts,),
        in_specs=in_specs,
        out_specs=out_specs,
        out_shape=out_shapes,
        compiler_params=_cparams("parallel"),
        name="dwconv_silu",
    )(*args)
    return res if t_cols else res[0]


def _rw_prep_kernel(x_ref, xp_ref, xn_ref, mu_ref, w0_ref, w2_ref, a0_ref, a2_ref, g2_ref,
                    kk_ref, ka_ref, rk_ref, ones_ref,
                    r_o, k_o, v_o, kk_o, b_o, lw_o, bonus_o, g_o, *, tiles_per_seq):
    x = x_ref[...].astype(F32)
    prev_row, next_row = _halo_rows(xp_ref, xn_ref, tiles_per_seq)
    x_prev, x_next = _shifted(x, prev_row, next_row)
    mu = mu_ref[...]
    xs = x + mu[0:1] * (x_prev - x) + mu[1:2] * (x_next - x)
    r = xs[:, 0:RW_DIM]
    k = xs[:, RW_DIM:2 * RW_DIM]
    v = xs[:, 2 * RW_DIM:3 * RW_DIM]
    lr = xs[:, 3 * RW_DIM:3 * RW_DIM + LANES]
    g_lr = xs[:, 3 * RW_DIM + LANES:3 * RW_DIM + 2 * LANES]
    th = jnp.tanh(lr)
    for d in range(2):
        z = w0_ref[d:d + 1, :] + _bdot(th, w2_ref[d])
        lw_o[d] = -jnp.exp(-_softplus(-z) - 0.5)
    a_gate = _sigmoid(a0_ref[...] + _bdot(lr, a2_ref[...]))
    g_o[...] = _bdot(_sigmoid(g_lr), g2_ref[...])
    ones = ones_ref[...]
    kk = k * kk_ref[...]
    kk = kk * lax.rsqrt(jnp.maximum(_xdot_l(kk * kk, ones), 1e-12))
    k2 = k * (1.0 + (a_gate - 1.0) * ka_ref[...])
    r_o[...] = r.astype(BF16)
    k_o[...] = k2.astype(BF16)
    v_o[...] = v.astype(BF16)
    kk_o[...] = kk.astype(BF16)
    b_o[...] = (kk * a_gate).astype(BF16)
    bonus_o[...] = _xdot_l(r * k2 * rk_ref[...], ones) * v


def _rw_chunk_kernel(*refs, chunk):
    L = chunk
    ins = (refs[0:6], refs[6:12])
    y_refs = refs[12:14]
    s_ref = refs[14]

    @pl.when(pl.program_id(1) == 0)
    def _():
        s_ref[...] = jnp.zeros_like(s_ref)

    n2 = 2 * L
    lane = lax.broadcasted_iota(jnp.int32, (L, LANES), 1)
    first = lane < RW_HEAD
    eye_l = lane == lax.broadcasted_iota(jnp.int32, (L, LANES), 0)
    zeros_l = jnp.zeros((L, LANES), BF16)
    rows2 = lax.broadcasted_iota(jnp.int32, (n2, n2), 0)
    cols2 = lax.broadcasted_iota(jnp.int32, (n2, n2), 1)

    def top(t):
        return jnp.concatenate([t.astype(BF16), zeros_l], axis=0)

    def bottom(t):
        return jnp.concatenate([zeros_l, t.astype(BF16)], axis=0)

    pairs = []
    chains = []
    for bs, dd in [(bs, dd) for bs in range(y_refs[0].shape[0]) for dd in range(2)]:
        r_ref, k_ref, v_ref, kk_ref, b_ref, lw_ref = ins[dd]
        sgn = 1 - 2 * dd
        _, incl_l, _ = _scan_masks(L, L, sgn)
        tri = jnp.where(incl_l, 1.0, 0.0).astype(BF16)
        lw = lw_ref[0, bs]
        c_incl = _xdot_r(tri, lw)
        c_tot = jnp.sum(lw, axis=0, keepdims=True)
        e_in = jnp.exp(c_incl)
        e_neg = jnp.exp(-c_incl)
        e_tot = jnp.exp(c_tot)
        e_rem = e_tot * e_neg
        kk = kk_ref[bs]
        bb = b_ref[bs]
        k = k_ref[bs]
        v = v_ref[bs]
        rt = r_ref[bs] * e_in
        at = -kk * jnp.exp(c_incl - lw)
        bt = bb * e_neg
        kt = k * e_neg
        bh = bb * e_rem
        kh = k * e_rem
        delta = ((cols2 & (L - 1)) - (rows2 & (L - 1))) * sgn
        g_mask = delta < jnp.where(rows2 < L, 0, 1)
        for p in range(RW_DIM // LANES):
            sl = slice(p * LANES, (p + 1) * LANES)
            s = s_ref[bs, dd, p]
            pair = dict(bs=bs, dd=dd, p=p, sl=sl, s=s, s_t=s.T.astype(BF16), e_tot=e_tot[:, sl], heads=[])
            pairs.append(pair)
            for hh in range(2):
                keep = first if hh == 0 else lane >= RW_HEAD
                head = lambda t: jnp.where(keep, t[:, sl], 0.0).astype(BF16)
                c = dict(pair=pair, g_mask=g_mask,
                         ar=jnp.concatenate([head(at), head(rt)], axis=0),
                         bk=jnp.concatenate([head(bt), head(kt)], axis=0),
                         bh=head(bh), kh=head(kh), v=head(v))
                pair["heads"].append(c)
                chains.append(c)

    for c in chains:
        c["g"] = jnp.where(c["g_mask"], _bdot_nt(c["ar"], c["bk"]), 0.0)
    for c in chains:
        g_a = c["g"][:L]
        c["ym"] = jnp.where(first, jnp.where(eye_l, 1.0, 0.0), pltpu.roll(g_a, L, 1))
        c["q"] = _bdot(g_a, top(c["ym"]))
        lhs = jnp.concatenate([c["ar"], jnp.where(cols2 < L, 0.0, c["g"]).astype(BF16)], axis=1)
        c["wy"] = _bdot(lhs, jnp.concatenate([c["pair"]["s_t"], zeros_l, c["v"]], axis=0))
    n_stage = int(math.log2(L))
    for i in range(1, n_stage):
        for c in chains:
            c["ym"] = jnp.where(first, c["ym"], 0.0) + c["q"]
            c["q"] = _bdot(c["q"], bottom(c["ym"]))
    for c in chains:
        c["ym"] = jnp.where(first, c["ym"], 0.0) + c["q"]
        c["u"] = _bdot(c["ym"], top(c["wy"][:L]))
    for c in chains:
        c["y"] = c["wy"][L:] + _bdot(c["g"][L:], top(c["u"]))
    for pair in pairs:
        h0, h1 = pair["heads"]
        y_refs[pair["dd"]][pair["bs"], :, pair["sl"]] = (h0["y"] + h1["y"]).astype(y_refs[pair["dd"]].dtype)
        uv = jnp.concatenate([h0["u"].astype(BF16), h1["u"].astype(BF16), h0["v"], h1["v"]], axis=0)
        bk = jnp.concatenate([h0["bh"], h1["bh"], h0["kh"], h1["kh"]], axis=0)
        s_ref[pair["bs"], pair["dd"], pair["p"]] = pair["s"] * pair["e_tot"] + _bdot_tn(uv, bk)


def _rw_post_kernel(yf_ref, yb_ref, bonus_ref, g_ref, lnw_ref, ones_ref, o_ref):
    y = yf_ref[...].astype(F32) + yb_ref[...].astype(F32)
    ones = ones_ref[...]
    yc = y - _xdot_l(y, ones) * (1.0 / RW_HEAD)
    var = _xdot_l(yc * yc, ones) * (1.0 / RW_HEAD)
    yn = yc * lax.rsqrt(var + RW_GN_EPS) * lnw_ref[...]
    o_ref[...] = ((yn + bonus_ref[...]) * g_ref[...]).astype(o_ref.dtype)


def _rwkv7(cols, p, *, batch, seq, ts):
    n_rows = cols.shape[0]
    dim = RW_DIM
    row = lambda i: (i, 0)
    fixed2 = lambda i: (0, 0)
    fixed3 = lambda i: (0, 0, 0)
    prev, nxt = _halo_specs(ts, RW_COLS, 0, n_rows)
    ones = jnp.asarray(np.kron(np.eye(RW_HEADS), np.ones((RW_HEAD, RW_HEAD))), BF16)
    zeros = jnp.zeros((64, dim), F32)
    w2 = jnp.concatenate([p["w2"], jnp.broadcast_to(zeros, (2, 64, dim))], axis=1).astype(BF16)
    a2 = jnp.concatenate([zeros, p["a2"]], axis=0).astype(BF16)
    vec = lambda t: t.reshape(1, dim)
    tile = jax.ShapeDtypeStruct((n_rows, dim), F32)
    btile = jax.ShapeDtypeStruct((n_rows, dim), BF16)
    r, k, v, kk, b, lw, bonus, g = pl.pallas_call(
        functools.partial(_rw_prep_kernel, tiles_per_seq=seq // ts),
        grid=(n_rows // ts,),
        in_specs=[pl.BlockSpec((ts, RW_COLS), row), prev, nxt,
                  pl.BlockSpec((2, RW_COLS), fixed2),
                  pl.BlockSpec((2, dim), fixed2),
                  pl.BlockSpec((2, LANES, dim), fixed3),
                  pl.BlockSpec((1, dim), fixed2),
                  pl.BlockSpec((LANES, dim), fixed2),
                  pl.BlockSpec((LANES, dim), fixed2),
                  pl.BlockSpec((1, dim), fixed2),
                  pl.BlockSpec((1, dim), fixed2),
                  pl.BlockSpec((1, dim), fixed2),
                  pl.BlockSpec((dim, dim), fixed2)],
        out_specs=[pl.BlockSpec((ts, dim), row)] * 5
                  + [pl.BlockSpec((2, ts, dim), lambda i: (0, i, 0))]
                  + [pl.BlockSpec((ts, dim), row)] * 2,
        out_shape=[btile] * 5 + [jax.ShapeDtypeStruct((2, n_rows, dim), F32)] + [tile] * 2,
        compiler_params=_cparams("parallel"),
        name="rw_prep",
    )(cols, cols, cols, p["mu"], p["w0"], w2, vec(p["a0"]), a2, p["g2"].astype(BF16),
      vec(p["k_k"]), vec(p["k_a"]), vec(p["r_k"]), ones)

    L = RW_CHUNK
    nc = seq // L
    nb = min(RW_BATCH_PER_STEP, batch)
    fwd = lambda bi, c: (bi, c, 0)
    bwd = lambda bi, c: (bi, nc - 1 - c, 0)
    by_seq = lambda t: t.reshape(batch, seq, dim)
    r3, k3, v3, kk3, b3 = by_seq(r), by_seq(k), by_seq(v), by_seq(kk), by_seq(b)
    lw4 = lw.reshape(2, batch, seq, dim)
    y3 = jax.ShapeDtypeStruct((batch, seq, dim), BF16)
    y_fwd, y_bwd = pl.pallas_call(
        functools.partial(_rw_chunk_kernel, chunk=L),
        grid=(batch // nb, nc),
        in_specs=[pl.BlockSpec((nb, L, dim), fwd)] * 5
                 + [pl.BlockSpec((1, nb, L, dim), lambda bi, c: (0, bi, c, 0))]
                 + [pl.BlockSpec((nb, L, dim), bwd)] * 5
                 + [pl.BlockSpec((1, nb, L, dim), lambda bi, c: (1, bi, nc - 1 - c, 0))],
        out_specs=[pl.BlockSpec((nb, L, dim), fwd), pl.BlockSpec((nb, L, dim), bwd)],
        out_shape=[y3, y3],
        scratch_shapes=[pltpu.VMEM((nb, 2, dim // LANES, LANES, LANES), F32)],
        compiler_params=_cparams("parallel", "arbitrary"),
        name="rw_chunk",
    )(r3, k3, v3, kk3, b3, lw4, r3, k3, v3, kk3, b3, lw4)
    y_fwd = y_fwd.reshape(n_rows, dim)
    y_bwd = y_bwd.reshape(n_rows, dim)

    return pl.pallas_call(
        _rw_post_kernel,
        grid=(n_rows // ts,),
        in_specs=[pl.BlockSpec((ts, dim), row), pl.BlockSpec((ts, dim), row),
                  pl.BlockSpec((ts, dim), row), pl.BlockSpec((ts, dim), row),
                  pl.BlockSpec((1, dim), fixed2), pl.BlockSpec((dim, dim), fixed2)],
        out_specs=pl.BlockSpec((ts, dim), row),
        out_shape=btile,
        compiler_params=_cparams("parallel"),
        name="rw_post",
    )(y_fwd, y_bwd, bonus, g, vec(p["ln_w"]), ones)


def _mb_ssd_kernel(*refs, chunk):
    L = chunk
    ins = (refs[0:5], refs[5:10])
    bias_c_ref, alog_c_ref, bias_r_ref, alog_r_ref, e1_ref = refs[10:15]
    y_refs = refs[15:17]
    st_ref = refs[17]

    @pl.when(pl.program_id(1) == 0)
    def _():
        st_ref[...] = jnp.zeros_like(st_ref)

    ri = lax.broadcasted_iota(jnp.int32, (L, L), 0)
    ci = lax.broadcasted_iota(jnp.int32, (L, L), 1)
    first = lax.broadcasted_iota(jnp.int32, (L, LANES), 1) < MB_HEADDIM
    gw = MB_HPG * MB_HEADDIM
    groups = []
    for bs, dd in [(bs, dd) for bs in range(y_refs[0].shape[0]) for dd in range(2)]:
        xs_ref, bc_ref, bt_ref, dtc_ref, dtr_ref = ins[dd]
        sgn = 1 - 2 * dd
        incl = (ci - ri) * sgn <= 0
        tri = jnp.where(incl, 1.0, 0.0).astype(BF16)
        tri_t = jnp.where((ri - ci) * sgn <= 0, 1.0, 0.0).astype(BF16)
        last = L - 1 if dd == 0 else 0
        dt_c = _softplus(dtc_ref[bs] + bias_c_ref[...])
        cs_c = _xdot_r(tri, dt_c * (-jnp.exp(alog_c_ref[...])))
        cols2 = jnp.concatenate([jnp.exp(cs_c), dt_c * jnp.exp(cs_c[last:last + 1, :] - cs_c)], axis=0)
        hi = cols2.astype(BF16)
        mid = (cols2 - hi.astype(F32)).astype(BF16)
        full2 = jnp.dot(jnp.concatenate([hi, mid], axis=1), e1_ref[dd], preferred_element_type=F32)
        ecs = full2[:L]
        xw = (xs_ref[bs] * full2[L:]).astype(BF16)
        etot = ecs[last:last + 1, :]
        xb = xs_ref[bs].astype(BF16)
        dt_r = _softplus(dtr_ref[bs] + bias_r_ref[dd])
        cs_r = _xdot_l(dt_r * (-jnp.exp(alog_r_ref[dd])), tri_t)
        for g in range(MB_GROUPS):
            gs = slice(g * gw, (g + 1) * gw)
            groups.append(dict(
                bs=bs, dd=dd, g=g, gs=gs, incl=incl, ecs=ecs[:, gs], etot=etot[:, gs], xw=xw[:, gs], xb=xb[:, gs],
                cs_c=cs_c, cs_r=cs_r, dt_r=dt_r,
                bg=bc_ref[bs, :, g * MB_STATE:(g + 1) * MB_STATE].astype(BF16),
                cg=bc_ref[bs, :, (MB_GROUPS + g) * MB_STATE:(MB_GROUPS + g + 1) * MB_STATE].astype(BF16),
                bt=bt_ref[bs, g * MB_STATE:(g + 1) * MB_STATE, :],
                st=st_ref[bs, dd, :, gs]))

    for c in groups:
        c["scores"] = _bdot_nt(c["cg"], c["bg"])
        c["y_off"] = _bdot(c["cg"], c["st"]) * c["ecs"]
        st_ref[c["bs"], c["dd"], :, c["gs"]] = c["st"] * c["etot"] + _bdot(c["bt"], c["xw"])
    for c in groups:
        pairs = []
        for j in range(MB_HPG // 2):
            xp = c["xb"][:, j * LANES:(j + 1) * LANES]
            halves = []
            for hh in range(2):
                h = c["g"] * MB_HPG + 2 * j + hh
                lane = MB_HEADS * c["dd"] + h
                b_t = jnp.broadcast_to(c["cs_c"][:, lane:lane + 1], (L, L))
                expo = jnp.minimum(b_t - c["cs_r"][h:h + 1, :], 0.0)
                m = jnp.where(c["incl"], c["scores"] * jnp.exp(expo) * c["dt_r"][h:h + 1, :], 0.0)
                halves.append(_bdot(m, xp))
            pairs.append(jnp.where(first, halves[0], halves[1]))
        y_refs[c["dd"]][c["bs"], :, c["gs"]] = (c["y_off"]
                                                + jnp.concatenate(pairs, axis=1)).astype(y_refs[c["dd"]].dtype)


def _mb_post_kernel(yf_ref, yb_ref, xs_ref, z_ref, d_ref, nw_ref, o_ref):
    y = yf_ref[...].astype(F32) + yb_ref[...].astype(F32) + d_ref[...] * xs_ref[...]
    z = z_ref[...].astype(F32)
    y = y * (z * _sigmoid(z))
    gw = MB_DIM // MB_GROUPS
    for g in range(MB_GROUPS):
        yg = y[:, g * gw:(g + 1) * gw]
        o_ref[:, g * gw:(g + 1) * gw] = (yg * lax.rsqrt(jnp.mean(yg * yg, axis=-1, keepdims=True) + EPS)
                                          * nw_ref[:, g * gw:(g + 1) * gw]).astype(o_ref.dtype)


def _mamba2(cols, dt_cols, dt_rows, p, *, batch, seq, ts):
    n_rows = cols.shape[0]
    xs = _dwconv_silu(cols, 1, MB_DIM, p["conv_w"][:, :MB_DIM], p["conv_b"][:MB_DIM], seq=seq, ts=ts)
    bc, b_t = _dwconv_silu(cols, 2 * MB_DIM // MB_BC, MB_BC, p["conv_w"][:, MB_DIM:], p["conv_b"][MB_DIM:],
                           seq=seq, ts=ts, t_cols=MB_GROUPS * MB_STATE)
    L = MB_CHUNK
    nc = seq // L
    pad = jnp.zeros((LANES - 2 * MB_HEADS,), F32)
    bias_c = jnp.concatenate([p["dt_bias"].reshape(-1), pad]).reshape(1, LANES)
    alog_c = jnp.concatenate([p["A_log"].reshape(-1), pad]).reshape(1, LANES)
    bias_r = p["dt_bias"].reshape(2, MB_HEADS, 1)
    alog_r = p["A_log"].reshape(2, MB_HEADS, 1)
    e1 = np.zeros((2, 2 * LANES, MB_DIM), np.float32)
    for d in range(2):
        for h in range(MB_HEADS):
            e1[d, MB_HEADS * d + h, h * MB_HEADDIM:(h + 1) * MB_HEADDIM] = 1.0
            e1[d, LANES + MB_HEADS * d + h, h * MB_HEADDIM:(h + 1) * MB_HEADDIM] = 1.0
    fixed = lambda bi, c: (0, 0)
    fixed3c = lambda bi, c: (0, 0, 0)

    nb = min(MB_BATCH_PER_STEP, batch)

    def dir_specs(d):
        chunk_of = (lambda c: c) if d == 0 else (lambda c: nc - 1 - c)
        rows = lambda bi, c: (bi, chunk_of(c), 0)
        return ([pl.BlockSpec((nb, L, MB_DIM), rows),
                 pl.BlockSpec((nb, L, MB_BC), rows),
                 pl.BlockSpec((nb, MB_GROUPS * MB_STATE, L), lambda bi, c: (bi, 0, chunk_of(c))),
                 pl.BlockSpec((nb, L, LANES), rows),
                 pl.BlockSpec((nb, MB_HEADS, L), lambda bi, c: (bi, d, chunk_of(c)))],
                pl.BlockSpec((nb, L, MB_DIM), rows))

    (in_f, out_f), (in_b, out_b) = dir_specs(0), dir_specs(1)
    y_tile = jax.ShapeDtypeStruct((n_rows, MB_DIM), BF16)
    y3 = jax.ShapeDtypeStruct((batch, seq, MB_DIM), BF16)
    xs3 = xs.reshape(batch, seq, MB_DIM)
    bc3 = bc.reshape(batch, seq, MB_BC)
    dt3 = dt_cols.reshape(batch, seq, LANES)
    y_fwd, y_bwd = pl.pallas_call(
        functools.partial(_mb_ssd_kernel, chunk=L),
        grid=(batch // nb, nc),
        in_specs=in_f + in_b + [pl.BlockSpec((1, LANES), fixed), pl.BlockSpec((1, LANES), fixed),
                                pl.BlockSpec((2, MB_HEADS, 1), fixed3c), pl.BlockSpec((2, MB_HEADS, 1), fixed3c),
                                pl.BlockSpec((2, 2 * LANES, MB_DIM), fixed3c)],
        out_specs=[out_f, out_b],
        out_shape=[y3, y3],
        scratch_shapes=[pltpu.VMEM((nb, 2, MB_STATE, MB_DIM), F32)],
        compiler_params=_cparams("parallel", "arbitrary"),
        name="mb_ssd",
    )(xs3, bc3, b_t, dt3, dt_rows, xs3, bc3, b_t, dt3, dt_rows, bias_c, alog_c, bias_r, alog_r,
      jnp.asarray(e1, BF16))
    y_fwd = y_fwd.reshape(n_rows, MB_DIM)
    y_bwd = y_bwd.reshape(n_rows, MB_DIM)

    row = lambda i: (i, 0)
    fixed2 = lambda i: (0, 0)
    return pl.pallas_call(
        _mb_post_kernel,
        grid=(n_rows // ts,),
        in_specs=[pl.BlockSpec((ts, MB_DIM), row), pl.BlockSpec((ts, MB_DIM), row),
                  pl.BlockSpec((ts, MB_DIM), row),
                  pl.BlockSpec((ts, MB_DIM), row),
                  pl.BlockSpec((1, MB_DIM), fixed2), pl.BlockSpec((1, MB_DIM), fixed2)],
        out_specs=pl.BlockSpec((ts, MB_DIM), row),
        out_shape=y_tile,
        compiler_params=_cparams("parallel"),
        name="mb_post",
    )(y_fwd, y_bwd, xs, cols, jnp.repeat(p["D"], MB_HEADDIM).reshape(1, MB_DIM), p["norm_w"].reshape(1, MB_DIM))


S5_GROUP, S5_GROUPS, S5_STATE = 16, 32, 64
S5_DIM = S5_GROUP * S5_GROUPS
S5_GPB = LANES // S5_GROUP
S5_BLOCKS = S5_GROUPS // S5_GPB
S5_HALF = S5_GPB * S5_STATE
S5_STEPS = 32


def _s5_disc_kernel(ar_ref, ai_ref, ldt_ref, bre_ref, bim_ref, abr_o, abi_o, bbr_o, bbi_o):
    dt = jnp.exp(ldt_ref[0])
    ar = jnp.minimum(ar_ref[0], -1e-4)
    ai = ai_ref[0]
    mag = jnp.exp(dt * ar)
    abr = mag * jnp.cos(dt * ai)
    abi = mag * jnp.sin(dt * ai)
    den = ar * ar + ai * ai
    fr = ((abr - 1.0) * ar + abi * ai) / den
    fi = (abi * ar - (abr - 1.0) * ai) / den
    bre = bre_ref[...]
    bim = bim_ref[...]
    abr_o[0] = abr
    abi_o[0] = abi
    bbr_o[0] = fr * bre - fi * bim
    bbi_o[0] = fr * bim + fi * bre


def _s5_scan_kernel(uf_ref, ub_ref, bw_ref, cw_ref, lr_ref, li_ref, yf_ref, yb_ref, x_s, st_s, *, batch, steps):
    @pl.when(pl.program_id(0) == 0)
    def _():
        st_s[...] = jnp.zeros_like(st_s)

    u_refs = (uf_ref, ub_ref)
    y_refs = (yf_ref, yb_ref)
    width = 2 * S5_HALF
    for dd in range(2):
        u = u_refs[dd][...].astype(BF16)
        for j in range(S5_BLOCKS):
            x_s[dd, :, j * width:(j + 1) * width] = jnp.dot(u[:, j * LANES:(j + 1) * LANES], bw_ref[dd, j],
                                                             preferred_element_type=F32)
    for dd in range(2):
        for j in range(S5_BLOCKS):
            re = slice(j * width, j * width + S5_HALF)
            im = slice(j * width + S5_HALF, (j + 1) * width)
            lam_r = jnp.broadcast_to(lr_ref[dd, :, j * S5_HALF:(j + 1) * S5_HALF], (batch, S5_HALF))
            lam_i = jnp.broadcast_to(li_ref[dd, :, j * S5_HALF:(j + 1) * S5_HALF], (batch, S5_HALF))
            xr = st_s[dd, :, re]
            xi = st_s[dd, :, im]
            for i in range(steps):
                t = i if dd == 0 else steps - 1 - i
                rows = slice(t * batch, (t + 1) * batch)
                xr, xi = (lam_r * xr - lam_i * xi + x_s[dd, rows, re],
                          lam_r * xi + lam_i * xr + x_s[dd, rows, im])
                x_s[dd, rows, re] = xr
                x_s[dd, rows, im] = xi
            st_s[dd, :, re] = xr
            st_s[dd, :, im] = xi
    for dd in range(2):
        for j in range(S5_BLOCKS):
            y_refs[dd][:, j * LANES:(j + 1) * LANES] = jnp.dot(
                x_s[dd, :, j * width:(j + 1) * width].astype(BF16), cw_ref[dd, j],
                preferred_element_type=F32).astype(y_refs[dd].dtype)


def _s5_post_kernel(yf_ref, yb_ref, u_ref, d_ref, gw_ref, gb_ref, o_ref):
    y = d_ref[...] * u_ref[...] + yf_ref[...].astype(F32) + yb_ref[...].astype(F32)
    y = 0.5 * y * (1.0 + jnp.tanh(math.sqrt(2.0 / math.pi) * (y + 0.044715 * (y * y * y))))
    o_ref[...] = (y * _sigmoid(_bdot(y, gw_ref[...]) + gb_ref[...])).astype(o_ref.dtype)


def _s5(u, p, *, batch, seq):
    n_rows = u.shape[0]
    gp = S5_GROUPS * S5_STATE
    bc = lambda t: jnp.broadcast_to(t.reshape(2, gp, 1), (2, gp, S5_GROUP))
    ldt = jnp.broadcast_to(p["log_dt"][:, :, None, None], (2, S5_GROUPS, S5_STATE, S5_GROUP)).reshape(2, gp, S5_GROUP)
    per_dir = pl.BlockSpec((1, gp, S5_GROUP), lambda d: (d, 0, 0))
    shared = pl.BlockSpec((gp, S5_GROUP), lambda d: (0, 0))
    disc = jax.ShapeDtypeStruct((2, gp, S5_GROUP), F32)
    abr, abi, bbr, bbi = pl.pallas_call(
        _s5_disc_kernel,
        grid=(2,),
        in_specs=[per_dir, per_dir, per_dir, shared, shared],
        out_specs=[per_dir] * 4,
        out_shape=[disc] * 4,
        compiler_params=_cparams("parallel"),
        name="s5_disc",
    )(bc(p["A_re"]), bc(p["A_im"]), ldt, p["B_re"].reshape(gp, S5_GROUP), p["B_im"].reshape(gp, S5_GROUP))

    eye = jnp.eye(S5_GPB, dtype=F32)
    shp = (2, S5_BLOCKS, S5_GPB, S5_STATE, S5_GROUP)
    b_blk = lambda t: jnp.einsum("djgpm,gh->djgmhp", t.reshape(shp), eye).reshape(2, S5_BLOCKS, LANES, S5_HALF)
    bw = jnp.concatenate([b_blk(bbr), b_blk(bbi)], axis=-1).astype(BF16)
    cshp = (2, S5_BLOCKS, S5_GPB, S5_GROUP, S5_STATE)
    c_blk = lambda t: jnp.einsum("djgmp,gh->djgphm", t.reshape(cshp), eye).reshape(2, S5_BLOCKS, S5_HALF, LANES)
    cw = jnp.concatenate([c_blk(p["C_re"]), c_blk(-p["C_im"])], axis=2).astype(BF16)
    lam_r = abr[:, :, 0].reshape(2, 1, gp)
    lam_i = abi[:, :, 0].reshape(2, 1, gp)

    steps = min(S5_STEPS, seq)
    nc = seq // steps
    tr = steps * batch
    width = 2 * S5_HALF
    fwd = lambda c: (c, 0)
    bwd = lambda c: (nc - 1 - c, 0)
    fixed4 = lambda c: (0, 0, 0, 0)
    fixed3 = lambda c: (0, 0, 0)
    y_tile = jax.ShapeDtypeStruct((n_rows, S5_DIM), BF16)
    y_fwd, y_bwd = pl.pallas_call(
        functools.partial(_s5_scan_kernel, batch=batch, steps=steps),
        grid=(nc,),
        in_specs=[pl.BlockSpec((tr, S5_DIM), fwd), pl.BlockSpec((tr, S5_DIM), bwd),
                  pl.BlockSpec((2, S5_BLOCKS, LANES, width), fixed4),
                  pl.BlockSpec((2, S5_BLOCKS, width, LANES), fixed4),
                  pl.BlockSpec((2, 1, gp), fixed3), pl.BlockSpec((2, 1, gp), fixed3)],
        out_specs=[pl.BlockSpec((tr, S5_DIM), fwd), pl.BlockSpec((tr, S5_DIM), bwd)],
        out_shape=[y_tile, y_tile],
        scratch_shapes=[pltpu.VMEM((2, tr, S5_BLOCKS * width), F32),
                        pltpu.VMEM((2, batch, S5_BLOCKS * width), F32)],
        compiler_params=_cparams("arbitrary"),
        name="s5_scan",
    )(u, u, bw, cw, lam_r, lam_i)

    tp = min(512, n_rows)
    row = lambda i: (i, 0)
    fixed = lambda i: (0, 0)
    return pl.pallas_call(
        _s5_post_kernel,
        grid=(n_rows // tp,),
        in_specs=[pl.BlockSpec((tp, S5_DIM), row), pl.BlockSpec((tp, S5_DIM), row), pl.BlockSpec((tp, S5_DIM), row),
                  pl.BlockSpec((1, S5_DIM), fixed), pl.BlockSpec((S5_DIM, S5_DIM), fixed),
                  pl.BlockSpec((1, S5_DIM), fixed)],
        out_specs=pl.BlockSpec((tp, S5_DIM), row),
        out_shape=jax.ShapeDtypeStruct((n_rows, S5_DIM), BF16),
        compiler_params=_cparams("parallel"),
        name="s5_post",
    )(y_fwd, y_bwd, u, p["D"].reshape(1, S5_DIM), p["glu_w"].astype(BF16), p["glu_b"].reshape(1, S5_DIM))


ML_HEADS, ML_HEAD, ML_BLOCK = 8, 128, 4
ML_DIM = ML_HEADS * ML_HEAD
ML_COLS_PAD = 2 * ML_DIM + LANES
ML_CHUNK = 128
ML_BATCH_PER_STEP = 2
ML_PROJ = 256
NEG_BIG = -1e30


def _log_sigmoid(u):
    return -_softplus(-u)


def _ml_prep_kernel(x_ref, xp_ref, xn_ref, cw_ref, cb_ref, wq_ref, wkt_ref, wv_ref,
                    xc_o, q_o, kt_o, v_o, *, tiles_per_seq):
    x = x_ref[...].astype(F32)
    prev_row, next_row = _halo_rows(xp_ref, xn_ref, tiles_per_seq)
    x_prev, x_next = _shifted(x, prev_row, next_row)
    w = cw_ref[...]
    y = w[0:1] * x_prev + w[1:2] * x + w[2:3] * x_next + cb_ref[...]
    xc = y * _sigmoid(y)
    xc_o[...] = xc.astype(BF16)
    xcb = xc.astype(BF16)
    xb = x.astype(BF16)
    for j in range(ML_DIM // ML_PROJ):
        sl = slice(j * ML_PROJ, (j + 1) * ML_PROJ)
        q_o[:, sl] = jnp.dot(xcb[:, sl], wq_ref[j], preferred_element_type=F32).astype(BF16)
        kt_o[0, sl, :] = (lax.dot_general(wkt_ref[j], xcb[:, sl], (((1,), (1,)), ((), ())),
                                          preferred_element_type=F32) * (ML_HEAD ** -0.5)).astype(BF16)
        v_o[:, sl] = jnp.dot(xb[:, sl], wv_ref[j], preferred_element_type=F32).astype(BF16)


def _ml_chunk_kernel(*refs, chunk):
    L = chunk
    ins = (refs[0:6], refs[6:12])
    bias_c_ref, ib_ref, fb_ref = refs[12:15]
    h_refs = refs[15:17]
    c_s, m_s = refs[17:19]

    @pl.when(pl.program_id(1) == 0)
    def _():
        c_s[...] = jnp.zeros_like(c_s)
        m_s[...] = jnp.zeros_like(m_s)

    lane = lax.broadcasted_iota(jnp.int32, (L, LANES), 1)
    ones_tile = jnp.ones((L, ML_HEAD), BF16)
    ri = lax.broadcasted_iota(jnp.int32, (L, L), 0)
    ci = lax.broadcasted_iota(jnp.int32, (L, L), 1)
    chains = []
    for bs, dd in [(bs, dd) for bs in range(h_refs[0].shape[0]) for dd in range(2)]:
        q_ref, kt_ref, v_ref, gc_ref, gi_ref, gf_ref = ins[dd]
        sgn = 1 - 2 * dd
        incl = (ci - ri) * sgn <= 0
        tri = jnp.where(incl, 1.0, 0.0).astype(BF16)
        tri_t = jnp.where((ri - ci) * sgn <= 0, 1.0, 0.0).astype(BF16)
        gpre = gc_ref[bs] + bias_c_ref[...]
        gcol = jnp.where(lane < 2 * ML_HEADS, gpre, _log_sigmoid(gpre))
        cs_c = _xdot_r(tri, gcol)
        li_r = gi_ref[bs] + ib_ref[dd]
        lf_r = _log_sigmoid(gf_ref[bs] + fb_ref[dd])
        b_r = _xdot_l(lf_r, tri_t)
        b_last = b_r[:, L - 1:L] if dd == 0 else b_r[:, 0:1]
        lw_r = b_last - b_r + li_r
        lw_max = jnp.max(lw_r, axis=-1, keepdims=True)
        for h in range(ML_HEADS):
            sl = slice(h * ML_HEAD, (h + 1) * ML_HEAD)
            jf = 2 * ML_HEADS + ML_HEADS * dd + h
            chains.append(dict(
                bs=bs, dd=dd, h=h, sl=sl, incl=incl, q=q_ref[bs, :, sl], kt=kt_ref[bs, sl, :],
                v_ext=jnp.concatenate([v_ref[bs, :, sl], ones_tile], axis=1),
                b_t=jnp.broadcast_to(cs_c[:, jf:jf + 1], (L, LANES)),
                b_row=b_r[h:h + 1, :], li_row=li_r[h:h + 1, :], lw_row=lw_r[h:h + 1, :],
                bl=b_last[h:h + 1, :], lw_max=lw_max[h:h + 1, :],
                m_prev=m_s[bs, dd, h:h + 1, 0:1], c_ext=c_s[bs, dd, h]))

    for c in chains:
        log_d = jnp.where(c["incl"], c["b_t"] - c["b_row"] + c["li_row"], NEG_BIG)
        inter = c["b_t"] + c["m_prev"]
        m_t = jnp.maximum(jnp.broadcast_to(jnp.max(log_d, axis=-1, keepdims=True), (L, LANES)), inter)
        c["m_t"] = m_t
        c["dmat"] = jnp.exp(log_d - m_t)
        c["w_in"] = jnp.exp(inter - m_t)
        c["qk"] = _bdot(c["q"], c["kt"])
        c["qc"] = _bdot(c["q"], c["c_ext"])
        m_new = jnp.maximum(c["bl"] + c["m_prev"], c["lw_max"])
        wkt = c["kt"].astype(F32) * jnp.exp(c["lw_row"] - m_new)
        c["c_new"] = jnp.exp(c["bl"] + c["m_prev"] - m_new) * c["c_ext"] + _bdot(wkt, c["v_ext"])
        c["m_new"] = m_new
    for c in chains:
        w_in2 = jnp.concatenate([c["w_in"], c["w_in"]], axis=1)
        nd = _bdot(c["qk"] * c["dmat"], c["v_ext"]) + w_in2 * c["qc"]
        den = nd[:, ML_HEAD:]
        h_refs[c["dd"]][c["bs"], :, c["sl"]] = (nd[:, :ML_HEAD]
                                                / jnp.maximum(jnp.abs(den), jnp.exp(-c["m_t"]))).astype(BF16)
        c_s[c["bs"], c["dd"], c["h"]] = c["c_new"]
        m_s[c["bs"], c["dd"], c["h"]:c["h"] + 1, :] = jnp.broadcast_to(c["m_new"], (1, LANES))


def _ml_post_kernel(hf_ref, hb_ref, o_ref_in, xc_ref, nw_ref, skip_ref, out_ref):
    hsum = hf_ref[...].astype(F32) + hb_ref[...].astype(F32)
    for h in range(ML_HEADS):
        sl = slice(h * ML_HEAD, (h + 1) * ML_HEAD)
        x = hsum[:, sl]
        xc = x - jnp.mean(x, axis=-1, keepdims=True)
        hn = xc * lax.rsqrt(jnp.mean(xc * xc, axis=-1, keepdims=True) + EPS) * nw_ref[:, sl]
        out_ref[:, sl] = (_sigmoid(o_ref_in[:, sl].astype(F32)) * hn
                          + skip_ref[:, sl] * xc_ref[:, sl]).astype(out_ref.dtype)


def _mlstm(cols, gate_cols, g_rows, p, *, batch, seq, ts):
    n_rows = cols.shape[0]
    dim = ML_DIM
    row = lambda i: (i, 0)
    fixed2 = lambda i: (0, 0)
    fixed3 = lambda i: (0, 0, 0)
    prev, nxt = _halo_specs(ts, dim, 0, n_rows)
    nblk = dim // ML_PROJ
    per = ML_PROJ // ML_BLOCK
    repeat = jnp.asarray(np.tile(np.eye(ML_BLOCK), (1, per)), F32)
    on_diag = jnp.asarray(np.kron(np.eye(per), np.ones((ML_BLOCK, ML_BLOCK))), F32)
    blockdiag = lambda w: (jnp.einsum("brd,dn->brn", w.reshape(nblk, ML_PROJ, ML_BLOCK), repeat,
                                      precision=lax.Precision.HIGHEST) * on_diag).astype(BF16)
    wspec = pl.BlockSpec((nblk, ML_PROJ, ML_PROJ), fixed3)
    tps = seq // ts
    tile = jax.ShapeDtypeStruct((n_rows, dim), BF16)
    xc, q, k_t, v = pl.pallas_call(
        functools.partial(_ml_prep_kernel, tiles_per_seq=tps),
        grid=(n_rows // ts,),
        in_specs=[pl.BlockSpec((ts, dim), row), prev, nxt,
                  pl.BlockSpec((3, dim), fixed2), pl.BlockSpec((1, dim), fixed2), wspec, wspec, wspec],
        out_specs=[pl.BlockSpec((ts, dim), row)] * 2
                  + [pl.BlockSpec((1, dim, ts), lambda i: (i // tps, 0, i % tps)), pl.BlockSpec((ts, dim), row)],
        out_shape=[tile] * 2 + [jax.ShapeDtypeStruct((batch, dim, seq), BF16), tile],
        compiler_params=_cparams("parallel"),
        name="ml_prep",
    )(cols, cols, cols, p["conv_w"], p["conv_b"].reshape(1, dim),
      blockdiag(p["wq"]), jnp.swapaxes(blockdiag(p["wk"]), 1, 2), blockdiag(p["wv"]))

    L = min(ML_CHUNK, seq)
    nc = seq // L
    ng = 4 * ML_HEADS
    pad = jnp.zeros((LANES - ng,), F32)
    bias_c = jnp.concatenate([p["i_b"].reshape(-1), p["f_b"].reshape(-1), pad]).reshape(1, LANES)
    ib = p["i_b"].reshape(2, ML_HEADS, 1)
    fb = p["f_b"].reshape(2, ML_HEADS, 1)
    fixed = lambda bi, c: (0, 0)
    fixed3c = lambda bi, c: (0, 0, 0)

    nb = min(ML_BATCH_PER_STEP, batch)

    def dir_specs(d):
        chunk_of = (lambda c: c) if d == 0 else (lambda c: nc - 1 - c)
        rows = lambda bi, c: (bi, chunk_of(c), 0)
        return ([pl.BlockSpec((nb, L, dim), rows),
                 pl.BlockSpec((nb, dim, L), lambda bi, c: (bi, 0, chunk_of(c))),
                 pl.BlockSpec((nb, L, dim), rows),
                 pl.BlockSpec((nb, L, LANES), rows),
                 pl.BlockSpec((nb, ML_HEADS, L), lambda bi, c: (bi, d, chunk_of(c))),
                 pl.BlockSpec((nb, ML_HEADS, L), lambda bi, c: (bi, 2 + d, chunk_of(c)))],
                pl.BlockSpec((nb, L, dim), rows))

    (in_f, out_f), (in_b, out_b) = dir_specs(0), dir_specs(1)
    h_tile = jax.ShapeDtypeStruct((n_rows, dim), BF16)
    h3 = jax.ShapeDtypeStruct((batch, seq, dim), BF16)
    q3 = q.reshape(batch, seq, dim)
    v3 = v.reshape(batch, seq, dim)
    gate3 = gate_cols.reshape(batch, seq, LANES)
    h_fwd, h_bwd = pl.pallas_call(
        functools.partial(_ml_chunk_kernel, chunk=L),
        grid=(batch // nb, nc),
        in_specs=in_f + in_b + [pl.BlockSpec((1, LANES), fixed),
                                pl.BlockSpec((2, ML_HEADS, 1), fixed3c), pl.BlockSpec((2, ML_HEADS, 1), fixed3c)],
        out_specs=[out_f, out_b],
        out_shape=[h3, h3],
        scratch_shapes=[pltpu.VMEM((nb, 2, ML_HEADS, ML_HEAD, 2 * ML_HEAD), F32),
                        pltpu.VMEM((nb, 2, ML_HEADS, LANES), F32)],
        compiler_params=_cparams("parallel", "arbitrary"),
        name="ml_chunk",
    )(q3, k_t, v3, gate3, g_rows, g_rows, q3, k_t, v3, gate3, g_rows, g_rows, bias_c, ib, fb)
    h_fwd = h_fwd.reshape(n_rows, dim)
    h_bwd = h_bwd.reshape(n_rows, dim)

    return pl.pallas_call(
        _ml_post_kernel,
        grid=(n_rows // ts,),
        in_specs=[pl.BlockSpec((ts, dim), row), pl.BlockSpec((ts, dim), row),
                  pl.BlockSpec((ts, dim), lambda i: (i, 1)),
                  pl.BlockSpec((ts, dim), row),
                  pl.BlockSpec((1, dim), fixed2), pl.BlockSpec((1, dim), fixed2)],
        out_specs=pl.BlockSpec((ts, dim), row),
        out_shape=h_tile,
        compiler_params=_cparams("parallel"),
        name="ml_post",
    )(h_fwd, h_bwd, cols, xc, p["norm_w"].reshape(1, dim), p["skip"].reshape(1, dim))


ROW_TILE = 512
MIXER_TILE = 512
FF_TILE = 2048


def _pad_cols(w, n):
    return jnp.pad(w, ((0, 0), (0, n - w.shape[1])))


def kernel(x, norm_mix, norm_mlp, norm_final, mlp_w1, mlp_w2, ab_w_in, ab_w_out, rw_mu, rw_w0, rw_w2, rw_a0, rw_a2, rw_g2, rw_k_k, rw_k_a, rw_r_k, rw_ln_w, mb_conv_w, mb_conv_b, mb_dt_bias, mb_A_log, mb_D, mb_norm_w, cd_w_in, cd_w_out, s5_A_re, s5_A_im, s5_log_dt, s5_B_re, s5_B_im, s5_C_re, s5_C_im, s5_D, s5_glu_w, s5_glu_b, ml_conv_w, ml_conv_b, ml_wq, ml_wk, ml_wv, ml_i_b, ml_f_b, ml_norm_w, ml_skip):
    batch, seq, dm = x.shape
    n_rows = batch * seq
    tm = min(ROW_TILE, seq)
    ts = min(MIXER_TILE, seq)
    tiles_per_seq = seq // tm
    h = x.reshape(n_rows, dm)
    depth = norm_mix.shape[0]
    for layer in range(depth):
        i = layer // 2
        last = layer == depth - 1
        if layer % 2 == 0:
            w_in = ab_w_in[i]
            rw_cols = _norm_matmul(h, norm_mix[layer], w_in[:, :RW_COLS].astype(BF16), tm=tm)
            mb_cols, mb_dt, mb_dt_t = _norm_matmul(h, norm_mix[layer],
                                                   _pad_cols(w_in[:, RW_COLS:], MB_COLS_PAD).astype(BF16),
                                                   tm=tm, seq=seq, tail=True)
            y1 = _rwkv7(rw_cols, dict(mu=rw_mu[i], w0=rw_w0[i], w2=rw_w2[i], a0=rw_a0[i], a2=rw_a2[i], g2=rw_g2[i],
                                      k_k=rw_k_k[i], k_a=rw_k_a[i], r_k=rw_r_k[i].reshape(-1), ln_w=rw_ln_w[i]),
                        batch=batch, seq=seq, ts=ts)
            y2 = _mamba2(mb_cols, mb_dt, mb_dt_t, dict(conv_w=mb_conv_w[i], conv_b=mb_conv_b[i], dt_bias=mb_dt_bias[i],
                                       A_log=mb_A_log[i], D=mb_D[i], norm_w=mb_norm_w[i]),
                         batch=batch, seq=seq, ts=ts)
            y1_spec = pl.BlockSpec((tm, RW_DIM), lambda r, k: (r, 0))
            w_out, k1 = ab_w_out[i], RW_DIM
        else:
            w_in = cd_w_in[i]
            tm_spec = pl.BlockSpec((tm, S5_DIM), lambda r: (r % tiles_per_seq, r // tiles_per_seq))
            s5_cols = _norm_matmul(h, norm_mix[layer], w_in[:, :S5_DIM].astype(BF16), tm=tm,
                                   out_spec=tm_spec, out_shape=(seq, batch * S5_DIM))
            ml_cols, ml_gates, ml_gates_t = _norm_matmul(h, norm_mix[layer],
                                                         _pad_cols(w_in[:, S5_DIM:], ML_COLS_PAD).astype(BF16),
                                                         tm=tm, seq=seq, tail=True)
            y1 = _s5(s5_cols.reshape(seq * batch, S5_DIM),
                     dict(A_re=s5_A_re[i], A_im=s5_A_im[i], log_dt=s5_log_dt[i], B_re=s5_B_re[i], B_im=s5_B_im[i],
                          C_re=s5_C_re[i], C_im=s5_C_im[i], D=s5_D[i], glu_w=s5_glu_w[i], glu_b=s5_glu_b[i]),
                     batch=batch, seq=seq).reshape(seq, batch * S5_DIM)
            y2 = _mlstm(ml_cols, ml_gates, ml_gates_t, dict(conv_w=ml_conv_w[i], conv_b=ml_conv_b[i], wq=ml_wq[i], wk=ml_wk[i], wv=ml_wv[i],
                                      i_b=ml_i_b[i], f_b=ml_f_b[i], norm_w=ml_norm_w[i], skip=ml_skip[i]),
                        batch=batch, seq=seq, ts=ts)
            y1_spec = pl.BlockSpec((tm, S5_DIM), lambda r, k: (r % tiles_per_seq, r // tiles_per_seq))
            w_out, k1 = cd_w_out[i], S5_DIM
        h = _mix_mlp(h, y1, y1_spec, y2, w_out[:k1].astype(BF16), w_out[k1:].astype(BF16), norm_mlp[layer],
                     mlp_w1[layer].astype(BF16), mlp_w2[layer].astype(BF16), norm_final,
                     tm=tm, tf=FF_TILE, final_norm=last)
    return h.reshape(batch, seq, dm)
```

```python
import functools
import math

import jax
import jax.numpy as jnp
import numpy as np
from jax import lax
from jax.experimental import pallas as pl
from jax.experimental.pallas import tpu as pltpu

F32 = jnp.float32
BF16 = jnp.bfloat16

EPS = 1e-5
LANES = 128
SUBLANES = 8
VMEM_LIMIT_BYTES = 48 * 1024 * 1024

D_MODEL = 1024
D_FF = 4 * D_MODEL

RW_HEADS, RW_HEAD = 8, 64
RW_DIM = RW_HEADS * RW_HEAD
RW_LR = 64 + 64 + 128
RW_COLS = 3 * RW_DIM + RW_LR
RW_GN_EPS = 64e-5
RW_CHUNK = 64
RW_BATCH_PER_STEP = 4

MB_HEADS, MB_HEADDIM, MB_GROUPS, MB_STATE = 16, 64, 2, 128
MB_HPG = MB_HEADS // MB_GROUPS
MB_DIM = MB_HEADS * MB_HEADDIM
MB_BC = 2 * MB_GROUPS * MB_STATE
MB_COLS_PAD = 2 * MB_DIM + MB_BC + LANES
MB_CHUNK = 128
MB_BATCH_PER_STEP = 4


def _cparams(*sem):
    return pltpu.CompilerParams(dimension_semantics=sem, vmem_limit_bytes=VMEM_LIMIT_BYTES)


def _bdot(a, b):
    return jnp.dot(a.astype(BF16), b.astype(BF16), preferred_element_type=F32)


def _bdot_nt(a, b):
    return lax.dot_general(a.astype(BF16), b.astype(BF16), (((1,), (1,)), ((), ())),
                           preferred_element_type=F32)


def _bdot_tn(a, b):
    return lax.dot_general(a.astype(BF16), b.astype(BF16), (((0,), (0,)), ((), ())),
                           preferred_element_type=F32)


def _split3(x):
    hi = x.astype(BF16)
    r = x - hi.astype(F32)
    mid = r.astype(BF16)
    lo = (r - mid.astype(F32)).astype(BF16)
    return hi, mid, lo


def _xdot_l(x, m):
    hi, mid, lo = _split3(x)
    dot = functools.partial(jnp.dot, preferred_element_type=F32)
    return dot(lo, m) + dot(mid, m) + dot(hi, m)


def _xdot_r(m, x):
    hi, mid, lo = _split3(x)
    dot = functools.partial(jnp.dot, preferred_element_type=F32)
    return dot(m, lo) + dot(m, mid) + dot(m, hi)


def _softplus(u):
    return jnp.maximum(u, 0.0) + jnp.log(1.0 + jnp.exp(-jnp.abs(u)))


def _sigmoid(u):
    return 1.0 / (1.0 + jnp.exp(-u))


def _rms(x, gain):
    return x * lax.rsqrt(jnp.mean(x * x, axis=-1, keepdims=True) + EPS) * gain


def _shifted(x, prev_row, next_row):
    n, w = x.shape
    rows = lax.broadcasted_iota(jnp.int32, (SUBLANES, w), 0)
    down = pltpu.roll(x, 1, 0)
    up = pltpu.roll(x, n - 1, 0)
    x_prev = jnp.concatenate([jnp.where(rows == 0, prev_row, down[:SUBLANES]), down[SUBLANES:]], axis=0)
    x_next = jnp.concatenate([up[:n - SUBLANES], jnp.where(rows == SUBLANES - 1, next_row, up[n - SUBLANES:])],
                             axis=0)
    return x_prev, x_next


HALO_ROWS = 2 * SUBLANES


def _halo_rows(xp_ref, xn_ref, tiles_per_seq):
    si = pl.program_id(0) % tiles_per_seq
    prev_row = jnp.where(si == 0, 0.0, xp_ref[...].astype(F32)[HALO_ROWS - 1:HALO_ROWS, :])
    next_row = jnp.where(si == tiles_per_seq - 1, 0.0, xn_ref[...].astype(F32)[0:1, :])
    return prev_row, next_row


def _halo_specs(ts, width, col_block, n_rows):
    per = ts // HALO_ROWS
    last = n_rows // HALO_ROWS - 1
    prev = pl.BlockSpec((HALO_ROWS, width), lambda i: (jnp.maximum(i * per - 1, 0), col_block))
    nxt = pl.BlockSpec((HALO_ROWS, width), lambda i: (jnp.minimum((i + 1) * per, last), col_block))
    return prev, nxt


def _scan_masks(n, period, sgn):
    ri = lax.broadcasted_iota(jnp.int32, (n, n), 0)
    ci = lax.broadcasted_iota(jnp.int32, (n, n), 1)
    delta = ((ci & (period - 1)) - (ri & (period - 1))) * sgn
    return delta < 0, delta <= 0, ri == ci


def _norm_mm_kernel(x_ref, g_ref, w_ref, *rest):
    xn = _rms(x_ref[...], g_ref[...]).astype(BF16)
    res = jnp.dot(xn, w_ref[...], preferred_element_type=F32)
    if len(rest) == 1:
        rest[0][...] = res.astype(rest[0].dtype)
    else:
        wt_ref, o_ref, tail_ref, tail_t_ref = rest
        o_ref[...] = res[:, :-LANES].astype(o_ref.dtype)
        tail_ref[...] = res[:, -LANES:]
        tail_t_ref[0] = lax.dot_general(wt_ref[...], xn, (((1,), (1,)), ((), ())), preferred_element_type=F32)


def _norm_matmul(h, gain, w, *, tm, seq=None, tail=False, out_spec=None, out_shape=None):
    m, dm = h.shape
    n = w.shape[1]
    n_main = n - LANES if tail else n
    if out_spec is None:
        out_spec = pl.BlockSpec((tm, n_main), lambda i: (i, 0))
        out_shape = (m, n_main)
    in_specs = [pl.BlockSpec((tm, dm), lambda i: (i, 0)),
                pl.BlockSpec((1, dm), lambda i: (0, 0)),
                pl.BlockSpec((dm, n), lambda i: (0, 0))]
    args = [h, gain.reshape(1, dm), w]
    out_specs = [out_spec]
    out_shapes = [jax.ShapeDtypeStruct(out_shape, BF16)]
    if tail:
        tps = seq // tm
        in_specs.append(pl.BlockSpec((LANES, dm), lambda i: (0, 0)))
        args.append(w[:, -LANES:].T)
        out_specs += [pl.BlockSpec((tm, LANES), lambda i: (i, 0)),
                      pl.BlockSpec((1, LANES, tm), lambda i: (i // tps, 0, i % tps))]
        out_shapes += [jax.ShapeDtypeStruct((m, LANES), F32), jax.ShapeDtypeStruct((m // seq, LANES, seq), F32)]
    res = pl.pallas_call(
        _norm_mm_kernel,
        grid=(m // tm,),
        in_specs=in_specs,
        out_specs=out_specs,
        out_shape=out_shapes,
        compiler_params=_cparams("parallel"),
        name="norm_matmul",
    )(*args)
    return res if tail else res[0]


def _mix_mlp_kernel(h_ref, y1_ref, y2_ref, wo1_ref, wo2_ref, gm_ref, w1_ref, w2_ref, gf_ref,
                    o_ref, h1_s, xn_s, acc_s, *, final_norm):
    kf = pl.program_id(1)

    @pl.when(kf == 0)
    def _():
        h1 = h_ref[...] + _bdot(y1_ref[...], wo1_ref[...]) + _bdot(y2_ref[...], wo2_ref[...])
        h1_s[...] = h1
        xn_s[...] = _rms(h1, gm_ref[...]).astype(BF16)
        acc_s[...] = jnp.zeros_like(acc_s)

    hid = jnp.dot(xn_s[...], w1_ref[...], preferred_element_type=F32)
    hid = jnp.square(jnp.maximum(hid, 0.0))
    acc_s[...] += jnp.dot(hid.astype(BF16), w2_ref[...], preferred_element_type=F32)

    @pl.when(kf == pl.num_programs(1) - 1)
    def _():
        out = h1_s[...] + acc_s[...]
        if final_norm:
            out = _rms(out, gf_ref[...])
        o_ref[...] = out


def _mix_mlp(h, y1, y1_spec, y2, wo1, wo2, g_mlp, w1, w2, g_final, *, tm, tf, final_norm):
    m, dm = h.shape
    ff = w1.shape[1]
    k2 = y2.shape[1]
    row = lambda i, k: (i, 0)
    fixed = lambda i, k: (0, 0)
    return pl.pallas_call(
        functools.partial(_mix_mlp_kernel, final_norm=final_norm),
        grid=(m // tm, ff // tf),
        in_specs=[pl.BlockSpec((tm, dm), row),
                  y1_spec,
                  pl.BlockSpec((tm, k2), row),
                  pl.BlockSpec(wo1.shape, fixed),
                  pl.BlockSpec(wo2.shape, fixed),
                  pl.BlockSpec((1, dm), fixed),
                  pl.BlockSpec((dm, tf), lambda i, k: (0, k)),
                  pl.BlockSpec((tf, dm), lambda i, k: (k, 0)),
                  pl.BlockSpec((1, dm), fixed)],
        out_specs=pl.BlockSpec((tm, dm), row),
        out_shape=jax.ShapeDtypeStruct((m, dm), F32),
        scratch_shapes=[pltpu.VMEM((tm, dm), F32), pltpu.VMEM((tm, dm), BF16), pltpu.VMEM((tm, dm), F32)],
        compiler_params=_cparams("parallel", "arbitrary"),
        name="mix_mlp",
    )(h, y1, y2, wo1, wo2, g_mlp.reshape(1, dm), w1, w2, g_final.reshape(1, dm))


def _dwconv_silu_kernel(x_ref, xp_ref, xn_ref, w_ref, b_ref, *rest, tiles_per_seq):
    x = x_ref[...].astype(F32)
    prev_row, next_row = _halo_rows(xp_ref, xn_ref, tiles_per_seq)
    x_prev, x_next = _shifted(x, prev_row, next_row)
    w = w_ref[...]
    y = w[0:1] * x_prev + w[1:2] * x + w[2:3] * x_next + b_ref[...]
    out = (y * _sigmoid(y)).astype(BF16)
    if len(rest) == 1:
        rest[0][...] = out
    else:
        eye_ref, o_ref, ot_ref = rest
        o_ref[...] = out
        n = eye_ref.shape[0]
        ot_ref[0] = lax.dot_general(eye_ref[...], out[:, :n], (((1,), (1,)), ((), ())),
                                    preferred_element_type=F32).astype(BF16)


def _dwconv_silu(cols, col_block, width, w, b, *, seq, ts, t_cols=0):
    n_rows = cols.shape[0]
    prev, nxt = _halo_specs(ts, width, col_block, n_rows)
    in_specs = [pl.BlockSpec((ts, width), lambda i: (i, col_block)), prev, nxt,
                pl.BlockSpec((3, width), lambda i: (0, 0)),
                pl.BlockSpec((1, width), lambda i: (0, 0))]
    args = [cols, cols, cols, w, b.reshape(1, width)]
    out_specs = [pl.BlockSpec((ts, width), lambda i: (i, 0))]
    out_shapes = [jax.ShapeDtypeStruct((n_rows, width), BF16)]
    if t_cols:
        tps = seq // ts
        in_specs.append(pl.BlockSpec((t_cols, t_cols), lambda i: (0, 0)))
        args.append(jnp.eye(t_cols, dtype=BF16))
        out_specs.append(pl.BlockSpec((1, t_cols, ts), lambda i: (i // tps, 0, i % tps)))
        out_shapes.append(jax.ShapeDtypeStruct((n_rows // seq, t_cols, seq), BF16))
    res = pl.pallas_call(
        functools.partial(_dwconv_silu_kernel, tiles_per_seq=seq // ts),
        grid=(n_rows // ts,),
        in_specs=in_specs,
        out_specs=out_specs,
        out_shape=out_shapes,
        compiler_params=_cparams("parallel"),
        name="dwconv_silu",
    )(*args)
    return res if t_cols else res[0]


def _rw_prep_kernel(x_ref, xp_ref, xn_ref, mu_ref, w0_ref, w2_ref, a0_ref, a2_ref, g2_ref,
                    kk_ref, ka_ref, rk_ref, ones_ref,
                    r_o, k_o, v_o, kk_o, b_o, lw_o, bonus_o, g_o, *, tiles_per_seq):
    x = x_ref[...].astype(F32)
    prev_row, next_row = _halo_rows(xp_ref, xn_ref, tiles_per_seq)
    x_prev, x_next = _shifted(x, prev_row, next_row)
    mu = mu_ref[...]
    xs = x + mu[0:1] * (x_prev - x) + mu[1:2] * (x_next - x)
    r = xs[:, 0:RW_DIM]
    k = xs[:, RW_DIM:2 * RW_DIM]
    v = xs[:, 2 * RW_DIM:3 * RW_DIM]
    lr = xs[:, 3 * RW_DIM:3 * RW_DIM + LANES]
    g_lr = xs[:, 3 * RW_DIM + LANES:3 * RW_DIM + 2 * LANES]
    th = jnp.tanh(lr)
    for d in range(2):
        z = w0_ref[d:d + 1, :] + _bdot(th, w2_ref[d])
        lw_o[d] = -jnp.exp(-_softplus(-z) - 0.5)
    a_gate = _sigmoid(a0_ref[...] + _bdot(lr, a2_ref[...]))
    g_o[...] = _bdot(_sigmoid(g_lr), g2_ref[...])
    ones = ones_ref[...]
    kk = k * kk_ref[...]
    kk = kk * lax.rsqrt(jnp.maximum(_xdot_l(kk * kk, ones), 1e-12))
    k2 = k * (1.0 + (a_gate - 1.0) * ka_ref[...])
    r_o[...] = r.astype(BF16)
    k_o[...] = k2.astype(BF16)
    v_o[...] = v.astype(BF16)
    kk_o[...] = kk.astype(BF16)
    b_o[...] = (kk * a_gate).astype(BF16)
    bonus_o[...] = _xdot_l(r * k2 * rk_ref[...], ones) * v


def _rw_chunk_kernel(*refs, chunk):
    L = chunk
    ins = (refs[0:6], refs[6:12])
    y_refs = refs[12:14]
    s_ref = refs[14]

    @pl.when(pl.program_id(1) == 0)
    def _():
        s_ref[...] = jnp.zeros_like(s_ref)

    n2 = 2 * L
    lane = lax.broadcasted_iota(jnp.int32, (L, LANES), 1)
    first = lane < RW_HEAD
    eye_l = lane == lax.broadcasted_iota(jnp.int32, (L, LANES), 0)
    zeros_l = jnp.zeros((L, LANES), BF16)
    rows2 = lax.broadcasted_iota(jnp.int32, (n2, n2), 0)
    cols2 = lax.broadcasted_iota(jnp.int32, (n2, n2), 1)

    def top(t):
        return jnp.concatenate([t.astype(BF16), zeros_l], axis=0)

    def bottom(t):
        return jnp.concatenate([zeros_l, t.astype(BF16)], axis=0)

    pairs = []
    chains = []
    for bs, dd in [(bs, dd) for bs in range(y_refs[0].shape[0]) for dd in range(2)]:
        r_ref, k_ref, v_ref, kk_ref, b_ref, lw_ref = ins[dd]
        sgn = 1 - 2 * dd
        _, incl_l, _ = _scan_masks(L, L, sgn)
        tri = jnp.where(incl_l, 1.0, 0.0).astype(BF16)
        lw = lw_ref[0, bs]
        c_incl = _xdot_r(tri, lw)
        c_tot = jnp.sum(lw, axis=0, keepdims=True)
        e_in = jnp.exp(c_incl)
        e_neg = jnp.exp(-c_incl)
        e_tot = jnp.exp(c_tot)
        e_rem = e_tot * e_neg
        kk = kk_ref[bs]
        bb = b_ref[bs]
        k = k_ref[bs]
        v = v_ref[bs]
        rt = r_ref[bs] * e_in
        at = -kk * jnp.exp(c_incl - lw)
        bt = bb * e_neg
        kt = k * e_neg
        bh = bb * e_rem
        kh = k * e_rem
        delta = ((cols2 & (L - 1)) - (rows2 & (L - 1))) * sgn
        g_mask = delta < jnp.where(rows2 < L, 0, 1)
        for p in range(RW_DIM // LANES):
            sl = slice(p * LANES, (p + 1) * LANES)
            s = s_ref[bs, dd, p]
            pair = dict(bs=bs, dd=dd, p=p, sl=sl, s=s, s_t=s.T.astype(BF16), e_tot=e_tot[:, sl], heads=[])
            pairs.append(pair)
            for hh in range(2):
                keep = first if hh == 0 else lane >= RW_HEAD
                head = lambda t: jnp.where(keep, t[:, sl], 0.0).astype(BF16)
                c = dict(pair=pair, g_mask=g_mask,
                         ar=jnp.concatenate([head(at), head(rt)], axis=0),
                         bk=jnp.concatenate([head(bt), head(kt)], axis=0),
                         bh=head(bh), kh=head(kh), v=head(v))
                pair["heads"].append(c)
                chains.append(c)

    for c in chains:
        c["g"] = jnp.where(c["g_mask"], _bdot_nt(c["ar"], c["bk"]), 0.0)
    for c in chains:
        g_a = c["g"][:L]
        c["ym"] = jnp.where(first, jnp.where(eye_l, 1.0, 0.0), pltpu.roll(g_a, L, 1))
        c["q"] = _bdot(g_a, top(c["ym"]))
        lhs = jnp.concatenate([c["ar"], jnp.where(cols2 < L, 0.0, c["g"]).astype(BF16)], axis=1)
        c["wy"] = _bdot(lhs, jnp.concatenate([c["pair"]["s_t"], zeros_l, c["v"]], axis=0))
    n_stage = int(math.log2(L))
    for i in range(1, n_stage):
        for c in chains:
            c["ym"] = jnp.where(first, c["ym"], 0.0) + c["q"]
            c["q"] = _bdot(c["q"], bottom(c["ym"]))
    for c in chains:
        c["ym"] = jnp.where(first, c["ym"], 0.0) + c["q"]
        c["u"] = _bdot(c["ym"], top(c["wy"][:L]))
    for c in chains:
        c["y"] = c["wy"][L:] + _bdot(c["g"][L:], top(c["u"]))
    for pair in pairs:
        h0, h1 = pair["heads"]
        y_refs[pair["dd"]][pair["bs"], :, pair["sl"]] = (h0["y"] + h1["y"]).astype(y_refs[pair["dd"]].dtype)
        uv = jnp.concatenate([h0["u"].astype(BF16), h1["u"].astype(BF16), h0["v"], h1["v"]], axis=0)
        bk = jnp.concatenate([h0["bh"], h1["bh"], h0["kh"], h1["kh"]], axis=0)
        s_ref[pair["bs"], pair["dd"], pair["p"]] = pair["s"] * pair["e_tot"] + _bdot_tn(uv, bk)


def _rw_post_kernel(yf_ref, yb_ref, bonus_ref, g_ref, lnw_ref, ones_ref, o_ref):
    y = yf_ref[...].astype(F32) + yb_ref[...].astype(F32)
    ones = ones_ref[...]
    yc = y - _xdot_l(y, ones) * (1.0 / RW_HEAD)
    var = _xdot_l(yc * yc, ones) * (1.0 / RW_HEAD)
    yn = yc * lax.rsqrt(var + RW_GN_EPS) * lnw_ref[...]
    o_ref[...] = ((yn + bonus_ref[...]) * g_ref[...]).astype(o_ref.dtype)


def _rwkv7(cols, p, *, batch, seq, ts):
    n_rows = cols.shape[0]
    dim = RW_DIM
    row = lambda i: (i, 0)
    fixed2 = lambda i: (0, 0)
    fixed3 = lambda i: (0, 0, 0)
    prev, nxt = _halo_specs(ts, RW_COLS, 0, n_rows)
    ones = jnp.asarray(np.kron(np.eye(RW_HEADS), np.ones((RW_HEAD, RW_HEAD))), BF16)
    zeros = jnp.zeros((64, dim), F32)
    w2 = jnp.concatenate([p["w2"], jnp.broadcast_to(zeros, (2, 64, dim))], axis=1).astype(BF16)
    a2 = jnp.concatenate([zeros, p["a2"]], axis=0).astype(BF16)
    vec = lambda t: t.reshape(1, dim)
    tile = jax.ShapeDtypeStruct((n_rows, dim), F32)
    btile = jax.ShapeDtypeStruct((n_rows, dim), BF16)
    r, k, v, kk, b, lw, bonus, g = pl.pallas_call(
        functools.partial(_rw_prep_kernel, tiles_per_seq=seq // ts),
        grid=(n_rows // ts,),
        in_specs=[pl.BlockSpec((ts, RW_COLS), row), prev, nxt,
                  pl.BlockSpec((2, RW_COLS), fixed2),
                  pl.BlockSpec((2, dim), fixed2),
                  pl.BlockSpec((2, LANES, dim), fixed3),
                  pl.BlockSpec((1, dim), fixed2),
                  pl.BlockSpec((LANES, dim), fixed2),
                  pl.BlockSpec((LANES, dim), fixed2),
                  pl.BlockSpec((1, dim), fixed2),
                  pl.BlockSpec((1, dim), fixed2),
                  pl.BlockSpec((1, dim), fixed2),
                  pl.BlockSpec((dim, dim), fixed2)],
        out_specs=[pl.BlockSpec((ts, dim), row)] * 5
                  + [pl.BlockSpec((2, ts, dim), lambda i: (0, i, 0))]
                  + [pl.BlockSpec((ts, dim), row)] * 2,
        out_shape=[btile] * 5 + [jax.ShapeDtypeStruct((2, n_rows, dim), F32)] + [tile] * 2,
        compiler_params=_cparams("parallel"),
        name="rw_prep",
    )(cols, cols, cols, p["mu"], p["w0"], w2, vec(p["a0"]), a2, p["g2"].astype(BF16),
      vec(p["k_k"]), vec(p["k_a"]), vec(p["r_k"]), ones)

    L = RW_CHUNK
    nc = seq // L
    nb = min(RW_BATCH_PER_STEP, batch)
    fwd = lambda bi, c: (bi, c, 0)
    bwd = lambda bi, c: (bi, nc - 1 - c, 0)
    by_seq = lambda t: t.reshape(batch, seq, dim)
    r3, k3, v3, kk3, b3 = by_seq(r), by_seq(k), by_seq(v), by_seq(kk), by_seq(b)
    lw4 = lw.reshape(2, batch, seq, dim)
    y3 = jax.ShapeDtypeStruct((batch, seq, dim), BF16)
    y_fwd, y_bwd = pl.pallas_call(
        functools.partial(_rw_chunk_kernel, chunk=L),
        grid=(batch // nb, nc),
        in_specs=[pl.BlockSpec((nb, L, dim), fwd)] * 5
                 + [pl.BlockSpec((1, nb, L, dim), lambda bi, c: (0, bi, c, 0))]
                 + [pl.BlockSpec((nb, L, dim), bwd)] * 5
                 + [pl.BlockSpec((1, nb, L, dim), lambda bi, c: (1, bi, nc - 1 - c, 0))],
        out_specs=[pl.BlockSpec((nb, L, dim), fwd), pl.BlockSpec((nb, L, dim), bwd)],
        out_shape=[y3, y3],
        scratch_shapes=[pltpu.VMEM((nb, 2, dim // LANES, LANES, LANES), F32)],
        compiler_params=_cparams("parallel", "arbitrary"),
        name="rw_chunk",
    )(r3, k3, v3, kk3, b3, lw4, r3, k3, v3, kk3, b3, lw4)
    y_fwd = y_fwd.reshape(n_rows, dim)
    y_bwd = y_bwd.reshape(n_rows, dim)

    return pl.pallas_call(
        _rw_post_kernel,
        grid=(n_rows // ts,),
        in_specs=[pl.BlockSpec((ts, dim), row), pl.BlockSpec((ts, dim), row),
                  pl.BlockSpec((ts, dim), row), pl.BlockSpec((ts, dim), row),
                  pl.BlockSpec((1, dim), fixed2), pl.BlockSpec((dim, dim), fixed2)],
        out_specs=pl.BlockSpec((ts, dim), row),
        out_shape=btile,
        compiler_params=_cparams("parallel"),
        name="rw_post",
    )(y_fwd, y_bwd, bonus, g, vec(p["ln_w"]), ones)


def _mb_ssd_kernel(*refs, chunk):
    L = chunk
    ins = (refs[0:5], refs[5:10])
    bias_c_ref, alog_c_ref, bias_r_ref, alog_r_ref, e1_ref = refs[10:15]
    y_refs = refs[15:17]
    st_ref = refs[17]

    @pl.when(pl.program_id(1) == 0)
    def _():
        st_ref[...] = jnp.zeros_like(st_ref)

    ri = lax.broadcasted_iota(jnp.int32, (L, L), 0)
    ci = lax.broadcasted_iota(jnp.int32, (L, L), 1)
    first = lax.broadcasted_iota(jnp.int32, (L, LANES), 1) < MB_HEADDIM
    gw = MB_HPG * MB_HEADDIM
    groups = []
    for bs, dd in [(bs, dd) for bs in range(y_refs[0].shape[0]) for dd in range(2)]:
        xs_ref, bc_ref, bt_ref, dtc_ref, dtr_ref = ins[dd]
        sgn = 1 - 2 * dd
        incl = (ci - ri) * sgn <= 0
        tri = jnp.where(incl, 1.0, 0.0).astype(BF16)
        tri_t = jnp.where((ri - ci) * sgn <= 0, 1.0, 0.0).astype(BF16)
        last = L - 1 if dd == 0 else 0
        dt_c = _softplus(dtc_ref[bs] + bias_c_ref[...])
        cs_c = _xdot_r(tri, dt_c * (-jnp.exp(alog_c_ref[...])))
        cols2 = jnp.concatenate([jnp.exp(cs_c), dt_c * jnp.exp(cs_c[last:last + 1, :] - cs_c)], axis=0)
        hi = cols2.astype(BF16)
        mid = (cols2 - hi.astype(F32)).astype(BF16)
        full2 = jnp.dot(jnp.concatenate([hi, mid], axis=1), e1_ref[dd], preferred_element_type=F32)
        ecs = full2[:L]
        xw = (xs_ref[bs] * full2[L:]).astype(BF16)
        etot = ecs[last:last + 1, :]
        xb = xs_ref[bs].astype(BF16)
        dt_r = _softplus(dtr_ref[bs] + bias_r_ref[dd])
        cs_r = _xdot_l(dt_r * (-jnp.exp(alog_r_ref[dd])), tri_t)
        for g in range(MB_GROUPS):
            gs = slice(g * gw, (g + 1) * gw)
            groups.append(dict(
                bs=bs, dd=dd, g=g, gs=gs, incl=incl, ecs=ecs[:, gs], etot=etot[:, gs], xw=xw[:, gs], xb=xb[:, gs],
                cs_c=cs_c, cs_r=cs_r, dt_r=dt_r,
                bg=bc_ref[bs, :, g * MB_STATE:(g + 1) * MB_STATE].astype(BF16),
                cg=bc_ref[bs, :, (MB_GROUPS + g) * MB_STATE:(MB_GROUPS + g + 1) * MB_STATE].astype(BF16),
                bt=bt_ref[bs, g * MB_STATE:(g + 1) * MB_STATE, :],
                st=st_ref[bs, dd, :, gs]))

    for c in groups:
        c["scores"] = _bdot_nt(c["cg"], c["bg"])
        c["y_off"] = _bdot(c["cg"], c["st"]) * c["ecs"]
        st_ref[c["bs"], c["dd"], :, c["gs"]] = c["st"] * c["etot"] + _bdot(c["bt"], c["xw"])
    for c in groups:
        pairs = []
        for j in range(MB_HPG // 2):
            xp = c["xb"][:, j * LANES:(j + 1) * LANES]
            halves = []
            for hh in range(2):
                h = c["g"] * MB_HPG + 2 * j + hh
                lane = MB_HEADS * c["dd"] + h
                b_t = jnp.broadcast_to(c["cs_c"][:, lane:lane + 1], (L, L))
                expo = jnp.minimum(b_t - c["cs_r"][h:h + 1, :], 0.0)
                m = jnp.where(c["incl"], c["scores"] * jnp.exp(expo) * c["dt_r"][h:h + 1, :], 0.0)
                halves.append(_bdot(m, xp))
            pairs.append(jnp.where(first, halves[0], halves[1]))
        y_refs[c["dd"]][c["bs"], :, c["gs"]] = (c["y_off"]
                                                + jnp.concatenate(pairs, axis=1)).astype(y_refs[c["dd"]].dtype)


def _mb_post_kernel(yf_ref, yb_ref, xs_ref, z_ref, d_ref, nw_ref, o_ref):
    y = yf_ref[...].astype(F32) + yb_ref[...].astype(F32) + d_ref[...] * xs_ref[...]
    z = z_ref[...].astype(F32)
    y = y * (z * _sigmoid(z))
    gw = MB_DIM // MB_GROUPS
    for g in range(MB_GROUPS):
        yg = y[:, g * gw:(g + 1) * gw]
        o_ref[:, g * gw:(g + 1) * gw] = (yg * lax.rsqrt(jnp.mean(yg * yg, axis=-1, keepdims=True) + EPS)
                                          * nw_ref[:, g * gw:(g + 1) * gw]).astype(o_ref.dtype)


def _mamba2(cols, dt_cols, dt_rows, p, *, batch, seq, ts):
    n_rows = cols.shape[0]
    xs = _dwconv_silu(cols, 1, MB_DIM, p["conv_w"][:, :MB_DIM], p["conv_b"][:MB_DIM], seq=seq, ts=ts)
    bc, b_t = _dwconv_silu(cols, 2 * MB_DIM // MB_BC, MB_BC, p["conv_w"][:, MB_DIM:], p["conv_b"][MB_DIM:],
                           seq=seq, ts=ts, t_cols=MB_GROUPS * MB_STATE)
    L = MB_CHUNK
    nc = seq // L
    pad = jnp.zeros((LANES - 2 * MB_HEADS,), F32)
    bias_c = jnp.concatenate([p["dt_bias"].reshape(-1), pad]).reshape(1, LANES)
    alog_c = jnp.concatenate([p["A_log"].reshape(-1), pad]).reshape(1, LANES)
    bias_r = p["dt_bias"].reshape(2, MB_HEADS, 1)
    alog_r = p["A_log"].reshape(2, MB_HEADS, 1)
    e1 = np.zeros((2, 2 * LANES, MB_DIM), np.float32)
    for d in range(2):
        for h in range(MB_HEADS):
            e1[d, MB_HEADS * d + h, h * MB_HEADDIM:(h + 1) * MB_HEADDIM] = 1.0
            e1[d, LANES + MB_HEADS * d + h, h * MB_HEADDIM:(h + 1) * MB_HEADDIM] = 1.0
    fixed = lambda bi, c: (0, 0)
    fixed3c = lambda bi, c: (0, 0, 0)

    nb = min(MB_BATCH_PER_STEP, batch)

    def dir_specs(d):
        chunk_of = (lambda c: c) if d == 0 else (lambda c: nc - 1 - c)
        rows = lambda bi, c: (bi, chunk_of(c), 0)
        return ([pl.BlockSpec((nb, L, MB_DIM), rows),
                 pl.BlockSpec((nb, L, MB_BC), rows),
                 pl.BlockSpec((nb, MB_GROUPS * MB_STATE, L), lambda bi, c: (bi, 0, chunk_of(c))),
                 pl.BlockSpec((nb, L, LANES), rows),
                 pl.BlockSpec((nb, MB_HEADS, L), lambda bi, c: (bi, d, chunk_of(c)))],
                pl.BlockSpec((nb, L, MB_DIM), rows))

    (in_f, out_f), (in_b, out_b) = dir_specs(0), dir_specs(1)
    y_tile = jax.ShapeDtypeStruct((n_rows, MB_DIM), BF16)
    y3 = jax.ShapeDtypeStruct((batch, seq, MB_DIM), BF16)
    xs3 = xs.reshape(batch, seq, MB_DIM)
    bc3 = bc.reshape(batch, seq, MB_BC)
    dt3 = dt_cols.reshape(batch, seq, LANES)
    y_fwd, y_bwd = pl.pallas_call(
        functools.partial(_mb_ssd_kernel, chunk=L),
        grid=(batch // nb, nc),
        in_specs=in_f + in_b + [pl.BlockSpec((1, LANES), fixed), pl.BlockSpec((1, LANES), fixed),
                                pl.BlockSpec((2, MB_HEADS, 1), fixed3c), pl.BlockSpec((2, MB_HEADS, 1), fixed3c),
                                pl.BlockSpec((2, 2 * LANES, MB_DIM), fixed3c)],
        out_specs=[out_f, out_b],
        out_shape=[y3, y3],
        scratch_shapes=[pltpu.VMEM((nb, 2, MB_STATE, MB_DIM), F32)],
        compiler_params=_cparams("parallel", "arbitrary"),
        name="mb_ssd",
    )(xs3, bc3, b_t, dt3, dt_rows, xs3, bc3, b_t, dt3, dt_rows, bias_c, alog_c, bias_r, alog_r,
      jnp.asarray(e1, BF16))
    y_fwd = y_fwd.reshape(n_rows, MB_DIM)
    y_bwd = y_bwd.reshape(n_rows, MB_DIM)

    row = lambda i: (i, 0)
    fixed2 = lambda i: (0, 0)
    return pl.pallas_call(
        _mb_post_kernel,
        grid=(n_rows // ts,),
        in_specs=[pl.BlockSpec((ts, MB_DIM), row), pl.BlockSpec((ts, MB_DIM), row),
                  pl.BlockSpec((ts, MB_DIM), row),
                  pl.BlockSpec((ts, MB_DIM), row),
                  pl.BlockSpec((1, MB_DIM), fixed2), pl.BlockSpec((1, MB_DIM), fixed2)],
        out_specs=pl.BlockSpec((ts, MB_DIM), row),
        out_shape=y_tile,
        compiler_params=_cparams("parallel"),
        name="mb_post",
    )(y_fwd, y_bwd, xs, cols, jnp.repeat(p["D"], MB_HEADDIM).reshape(1, MB_DIM), p["norm_w"].reshape(1, MB_DIM))


S5_GROUP, S5_GROUPS, S5_STATE = 16, 32, 64
S5_DIM = S5_GROUP * S5_GROUPS
S5_GPB = LANES // S5_GROUP
S5_BLOCKS = S5_GROUPS // S5_GPB
S5_HALF = S5_GPB * S5_STATE
S5_STEPS = 32


def _s5_disc_kernel(ar_ref, ai_ref, ldt_ref, bre_ref, bim_ref, abr_o, abi_o, bbr_o, bbi_o):
    dt = jnp.exp(ldt_ref[0])
    ar = jnp.minimum(ar_ref[0], -1e-4)
    ai = ai_ref[0]
    mag = jnp.exp(dt * ar)
    abr = mag * jnp.cos(dt * ai)
    abi = mag * jnp.sin(dt * ai)
    den = ar * ar + ai * ai
    fr = ((abr - 1.0) * ar + abi * ai) / den
    fi = (abi * ar - (abr - 1.0) * ai) / den
    bre = bre_ref[...]
    bim = bim_ref[...]
    abr_o[0] = abr
    abi_o[0] = abi
    bbr_o[0] = fr * bre - fi * bim
    bbi_o[0] = fr * bim + fi * bre


def _s5_scan_kernel(uf_ref, ub_ref, bw_ref, cw_ref, lr_ref, li_ref, yf_ref, yb_ref, x_s, st_s, *, batch, steps):
    @pl.when(pl.program_id(0) == 0)
    def _():
        st_s[...] = jnp.zeros_like(st_s)

    u_refs = (uf_ref, ub_ref)
    y_refs = (yf_ref, yb_ref)
    width = 2 * S5_HALF
    for dd in range(2):
        u = u_refs[dd][...].astype(BF16)
        for j in range(S5_BLOCKS):
            x_s[dd, :, j * width:(j + 1) * width] = jnp.dot(u[:, j * LANES:(j + 1) * LANES], bw_ref[dd, j],
                                                             preferred_element_type=F32)
    for dd in range(2):
        for j in range(S5_BLOCKS):
            re = slice(j * width, j * width + S5_HALF)
            im = slice(j * width + S5_HALF, (j + 1) * width)
            lam_r = jnp.broadcast_to(lr_ref[dd, :, j * S5_HALF:(j + 1) * S5_HALF], (batch, S5_HALF))
            lam_i = jnp.broadcast_to(li_ref[dd, :, j * S5_HALF:(j + 1) * S5_HALF], (batch, S5_HALF))
            xr = st_s[dd, :, re]
            xi = st_s[dd, :, im]
            for i in range(steps):
                t = i if dd == 0 else steps - 1 - i
                rows = slice(t * batch, (t + 1) * batch)
                xr, xi = (lam_r * xr - lam_i * xi + x_s[dd, rows, re],
                          lam_r * xi + lam_i * xr + x_s[dd, rows, im])
                x_s[dd, rows, re] = xr
                x_s[dd, rows, im] = xi
            st_s[dd, :, re] = xr
            st_s[dd, :, im] = xi
    for dd in range(2):
        for j in range(S5_BLOCKS):
            y_refs[dd][:, j * LANES:(j + 1) * LANES] = jnp.dot(
                x_s[dd, :, j * width:(j + 1) * width].astype(BF16), cw_ref[dd, j],
                preferred_element_type=F32).astype(y_refs[dd].dtype)


def _s5_post_kernel(yf_ref, yb_ref, u_ref, d_ref, gw_ref, gb_ref, o_ref):
    y = d_ref[...] * u_ref[...] + yf_ref[...].astype(F32) + yb_ref[...].astype(F32)
    y = 0.5 * y * (1.0 + jnp.tanh(math.sqrt(2.0 / math.pi) * (y + 0.044715 * (y * y * y))))
    o_ref[...] = (y * _sigmoid(_bdot(y, gw_ref[...]) + gb_ref[...])).astype(o_ref.dtype)


def _s5(u, p, *, batch, seq):
    n_rows = u.shape[0]
    gp = S5_GROUPS * S5_STATE
    bc = lambda t: jnp.broadcast_to(t.reshape(2, gp, 1), (2, gp, S5_GROUP))
    ldt = jnp.broadcast_to(p["log_dt"][:, :, None, None], (2, S5_GROUPS, S5_STATE, S5_GROUP)).reshape(2, gp, S5_GROUP)
    per_dir = pl.BlockSpec((1, gp, S5_GROUP), lambda d: (d, 0, 0))
    shared = pl.BlockSpec((gp, S5_GROUP), lambda d: (0, 0))
    disc = jax.ShapeDtypeStruct((2, gp, S5_GROUP), F32)
    abr, abi, bbr, bbi = pl.pallas_call(
        _s5_disc_kernel,
        grid=(2,),
        in_specs=[per_dir, per_dir, per_dir, shared, shared],
        out_specs=[per_dir] * 4,
        out_shape=[disc] * 4,
        compiler_params=_cparams("parallel"),
        name="s5_disc",
    )(bc(p["A_re"]), bc(p["A_im"]), ldt, p["B_re"].reshape(gp, S5_GROUP), p["B_im"].reshape(gp, S5_GROUP))

    eye = jnp.eye(S5_GPB, dtype=F32)
    shp = (2, S5_BLOCKS, S5_GPB, S5_STATE, S5_GROUP)
    b_blk = lambda t: jnp.einsum("djgpm,gh->djgmhp", t.reshape(shp), eye).reshape(2, S5_BLOCKS, LANES, S5_HALF)
    bw = jnp.concatenate([b_blk(bbr), b_blk(bbi)], axis=-1).astype(BF16)
    cshp = (2, S5_BLOCKS, S5_GPB, S5_GROUP, S5_STATE)
    c_blk = lambda t: jnp.einsum("djgmp,gh->djgphm", t.reshape(cshp), eye).reshape(2, S5_BLOCKS, S5_HALF, LANES)
    cw = jnp.concatenate([c_blk(p["C_re"]), c_blk(-p["C_im"])], axis=2).astype(BF16)
    lam_r = abr[:, :, 0].reshape(2, 1, gp)
    lam_i = abi[:, :, 0].reshape(2, 1, gp)

    steps = min(S5_STEPS, seq)
    nc = seq // steps
    tr = steps * batch
    width = 2 * S5_HALF
    fwd = lambda c: (c, 0)
    bwd = lambda c: (nc - 1 - c, 0)
    fixed4 = lambda c: (0, 0, 0, 0)
    fixed3 = lambda c: (0, 0, 0)
    y_tile = jax.ShapeDtypeStruct((n_rows, S5_DIM), BF16)
    y_fwd, y_bwd = pl.pallas_call(
        functools.partial(_s5_scan_kernel, batch=batch, steps=steps),
        grid=(nc,),
        in_specs=[pl.BlockSpec((tr, S5_DIM), fwd), pl.BlockSpec((tr, S5_DIM), bwd),
                  pl.BlockSpec((2, S5_BLOCKS, LANES, width), fixed4),
                  pl.BlockSpec((2, S5_BLOCKS, width, LANES), fixed4),
                  pl.BlockSpec((2, 1, gp), fixed3), pl.BlockSpec((2, 1, gp), fixed3)],
        out_specs=[pl.BlockSpec((tr, S5_DIM), fwd), pl.BlockSpec((tr, S5_DIM), bwd)],
        out_shape=[y_tile, y_tile],
        scratch_shapes=[pltpu.VMEM((2, tr, S5_BLOCKS * width), F32),
                        pltpu.VMEM((2, batch, S5_BLOCKS * width), F32)],
        compiler_params=_cparams("arbitrary"),
        name="s5_scan",
    )(u, u, bw, cw, lam_r, lam_i)

    tp = min(512, n_rows)
    row = lambda i: (i, 0)
    fixed = lambda i: (0, 0)
    return pl.pallas_call(
        _s5_post_kernel,
        grid=(n_rows // tp,),
        in_specs=[pl.BlockSpec((tp, S5_DIM), row), pl.BlockSpec((tp, S5_DIM), row), pl.BlockSpec((tp, S5_DIM), row),
                  pl.BlockSpec((1, S5_DIM), fixed), pl.BlockSpec((S5_DIM, S5_DIM), fixed),
                  pl.BlockSpec((1, S5_DIM), fixed)],
        out_specs=pl.BlockSpec((tp, S5_DIM), row),
        out_shape=jax.ShapeDtypeStruct((n_rows, S5_DIM), BF16),
        compiler_params=_cparams("parallel"),
        name="s5_post",
    )(y_fwd, y_bwd, u, p["D"].reshape(1, S5_DIM), p["glu_w"].astype(BF16), p["glu_b"].reshape(1, S5_DIM))


ML_HEADS, ML_HEAD, ML_BLOCK = 8, 128, 4
ML_DIM = ML_HEADS * ML_HEAD
ML_COLS_PAD = 2 * ML_DIM + LANES
ML_CHUNK = 128
ML_BATCH_PER_STEP = 2
ML_PROJ = 256
NEG_BIG = -1e30


def _log_sigmoid(u):
    return -_softplus(-u)


def _ml_prep_kernel(x_ref, xp_ref, xn_ref, cw_ref, cb_ref, wq_ref, wkt_ref, wv_ref,
                    xc_o, q_o, kt_o, v_o, *, tiles_per_seq):
    x = x_ref[...].astype(F32)
    prev_row, next_row = _halo_rows(xp_ref, xn_ref, tiles_per_seq)
    x_prev, x_next = _shifted(x, prev_row, next_row)
    w = cw_ref[...]
    y = w[0:1] * x_prev + w[1:2] * x + w[2:3] * x_next + cb_ref[...]
    xc = y * _sigmoid(y)
    xc_o[...] = xc.astype(BF16)
    xcb = xc.astype(BF16)
    xb = x.astype(BF16)
    for j in range(ML_DIM // ML_PROJ):
        sl = slice(j * ML_PROJ, (j + 1) * ML_PROJ)
        q_o[:, sl] = jnp.dot(xcb[:, sl], wq_ref[j], preferred_element_type=F32).astype(BF16)
        kt_o[0, sl, :] = (lax.dot_general(wkt_ref[j], xcb[:, sl], (((1,), (1,)), ((), ())),
                                          preferred_element_type=F32) * (ML_HEAD ** -0.5)).astype(BF16)
        v_o[:, sl] = jnp.dot(xb[:, sl], wv_ref[j], preferred_element_type=F32).astype(BF16)


def _ml_chunk_kernel(*refs, chunk):
    L = chunk
    ins = (refs[0:6], refs[6:12])
    bias_c_ref, ib_ref, fb_ref = refs[12:15]
    h_refs = refs[15:17]
    c_s, m_s = refs[17:19]

    @pl.when(pl.program_id(1) == 0)
    def _():
        c_s[...] = jnp.zeros_like(c_s)
        m_s[...] = jnp.zeros_like(m_s)

    lane = lax.broadcasted_iota(jnp.int32, (L, LANES), 1)
    ones_tile = jnp.ones((L, ML_HEAD), BF16)
    ri = lax.broadcasted_iota(jnp.int32, (L, L), 0)
    ci = lax.broadcasted_iota(jnp.int32, (L, L), 1)
    chains = []
    for bs, dd in [(bs, dd) for bs in range(h_refs[0].shape[0]) for dd in range(2)]:
        q_ref, kt_ref, v_ref, gc_ref, gi_ref, gf_ref = ins[dd]
        sgn = 1 - 2 * dd
        incl = (ci - ri) * sgn <= 0
        tri = jnp.where(incl, 1.0, 0.0).astype(BF16)
        tri_t = jnp.where((ri - ci) * sgn <= 0, 1.0, 0.0).astype(BF16)
        gpre = gc_ref[bs] + bias_c_ref[...]
        gcol = jnp.where(lane < 2 * ML_HEADS, gpre, _log_sigmoid(gpre))
        cs_c = _xdot_r(tri, gcol)
        li_r = gi_ref[bs] + ib_ref[dd]
        lf_r = _log_sigmoid(gf_ref[bs] + fb_ref[dd])
        b_r = _xdot_l(lf_r, tri_t)
        b_last = b_r[:, L - 1:L] if dd == 0 else b_r[:, 0:1]
        lw_r = b_last - b_r + li_r
        lw_max = jnp.max(lw_r, axis=-1, keepdims=True)
        for h in range(ML_HEADS):
            sl = slice(h * ML_HEAD, (h + 1) * ML_HEAD)
            jf = 2 * ML_HEADS + ML_HEADS * dd + h
            chains.append(dict(
                bs=bs, dd=dd, h=h, sl=sl, incl=incl, q=q_ref[bs, :, sl], kt=kt_ref[bs, sl, :],
                v_ext=jnp.concatenate([v_ref[bs, :, sl], ones_tile], axis=1),
                b_t=jnp.broadcast_to(cs_c[:, jf:jf + 1], (L, LANES)),
                b_row=b_r[h:h + 1, :], li_row=li_r[h:h + 1, :], lw_row=lw_r[h:h + 1, :],
                bl=b_last[h:h + 1, :], lw_max=lw_max[h:h + 1, :],
                m_prev=m_s[bs, dd, h:h + 1, 0:1], c_ext=c_s[bs, dd, h]))

    for c in chains:
        log_d = jnp.where(c["incl"], c["b_t"] - c["b_row"] + c["li_row"], NEG_BIG)
        inter = c["b_t"] + c["m_prev"]
        m_t = jnp.maximum(jnp.broadcast_to(jnp.max(log_d, axis=-1, keepdims=True), (L, LANES)), inter)
        c["m_t"] = m_t
        c["dmat"] = jnp.exp(log_d - m_t)
        c["w_in"] = jnp.exp(inter - m_t)
        c["qk"] = _bdot(c["q"], c["kt"])
        c["qc"] = _bdot(c["q"], c["c_ext"])
        m_new = jnp.maximum(c["bl"] + c["m_prev"], c["lw_max"])
        wkt = c["kt"].astype(F32) * jnp.exp(c["lw_row"] - m_new)
        c["c_new"] = jnp.exp(c["bl"] + c["m_prev"] - m_new) * c["c_ext"] + _bdot(wkt, c["v_ext"])
        c["m_new"] = m_new
    for c in chains:
        w_in2 = jnp.concatenate([c["w_in"], c["w_in"]], axis=1)
        nd = _bdot(c["qk"] * c["dmat"], c["v_ext"]) + w_in2 * c["qc"]
        den = nd[:, ML_HEAD:]
        h_refs[c["dd"]][c["bs"], :, c["sl"]] = (nd[:, :ML_HEAD]
                                                / jnp.maximum(jnp.abs(den), jnp.exp(-c["m_t"]))).astype(BF16)
        c_s[c["bs"], c["dd"], c["h"]] = c["c_new"]
        m_s[c["bs"], c["dd"], c["h"]:c["h"] + 1, :] = jnp.broadcast_to(c["m_new"], (1, LANES))


def _ml_post_kernel(hf_ref, hb_ref, o_ref_in, xc_ref, nw_ref, skip_ref, out_ref):
    hsum = hf_ref[...].astype(F32) + hb_ref[...].astype(F32)
    for h in range(ML_HEADS):
        sl = slice(h * ML_HEAD, (h + 1) * ML_HEAD)
        x = hsum[:, sl]
        xc = x - jnp.mean(x, axis=-1, keepdims=True)
        hn = xc * lax.rsqrt(jnp.mean(xc * xc, axis=-1, keepdims=True) + EPS) * nw_ref[:, sl]
        out_ref[:, sl] = (_sigmoid(o_ref_in[:, sl].astype(F32)) * hn
                          + skip_ref[:, sl] * xc_ref[:, sl]).astype(out_ref.dtype)


def _mlstm(cols, gate_cols, g_rows, p, *, batch, seq, ts):
    n_rows = cols.shape[0]
    dim = ML_DIM
    row = lambda i: (i, 0)
    fixed2 = lambda i: (0, 0)
    fixed3 = lambda i: (0, 0, 0)
    prev, nxt = _halo_specs(ts, dim, 0, n_rows)
    nblk = dim // ML_PROJ
    per = ML_PROJ // ML_BLOCK
    repeat = jnp.asarray(np.tile(np.eye(ML_BLOCK), (1, per)), F32)
    on_diag = jnp.asarray(np.kron(np.eye(per), np.ones((ML_BLOCK, ML_BLOCK))), F32)
    blockdiag = lambda w: (jnp.einsum("brd,dn->brn", w.reshape(nblk, ML_PROJ, ML_BLOCK), repeat,
                                      precision=lax.Precision.HIGHEST) * on_diag).astype(BF16)
    wspec = pl.BlockSpec((nblk, ML_PROJ, ML_PROJ), fixed3)
    tps = seq // ts
    tile = jax.ShapeDtypeStruct((n_rows, dim), BF16)
    xc, q, k_t, v = pl.pallas_call(
        functools.partial(_ml_prep_kernel, tiles_per_seq=tps),
        grid=(n_rows // ts,),
        in_specs=[pl.BlockSpec((ts, dim), row), prev, nxt,
                  pl.BlockSpec((3, dim), fixed2), pl.BlockSpec((1, dim), fixed2), wspec, wspec, wspec],
        out_specs=[pl.BlockSpec((ts, dim), row)] * 2
                  + [pl.BlockSpec((1, dim, ts), lambda i: (i // tps, 0, i % tps)), pl.BlockSpec((ts, dim), row)],
        out_shape=[tile] * 2 + [jax.ShapeDtypeStruct((batch, dim, seq), BF16), tile],
        compiler_params=_cparams("parallel"),
        name="ml_prep",
    )(cols, cols, cols, p["conv_w"], p["conv_b"].reshape(1, dim),
      blockdiag(p["wq"]), jnp.swapaxes(blockdiag(p["wk"]), 1, 2), blockdiag(p["wv"]))

    L = min(ML_CHUNK, seq)
    nc = seq // L
    ng = 4 * ML_HEADS
    pad = jnp.zeros((LANES - ng,), F32)
    bias_c = jnp.concatenate([p["i_b"].reshape(-1), p["f_b"].reshape(-1), pad]).reshape(1, LANES)
    ib = p["i_b"].reshape(2, ML_HEADS, 1)
    fb = p["f_b"].reshape(2, ML_HEADS, 1)
    fixed = lambda bi, c: (0, 0)
    fixed3c = lambda bi, c: (0, 0, 0)

    nb = min(ML_BATCH_PER_STEP, batch)

    def dir_specs(d):
        chunk_of = (lambda c: c) if d == 0 else (lambda c: nc - 1 - c)
        rows = lambda bi, c: (bi, chunk_of(c), 0)
        return ([pl.BlockSpec((nb, L, dim), rows),
                 pl.BlockSpec((nb, dim, L), lambda bi, c: (bi, 0, chunk_of(c))),
                 pl.BlockSpec((nb, L, dim), rows),
                 pl.BlockSpec((nb, L, LANES), rows),
                 pl.BlockSpec((nb, ML_HEADS, L), lambda bi, c: (bi, d, chunk_of(c))),
                 pl.BlockSpec((nb, ML_HEADS, L), lambda bi, c: (bi, 2 + d, chunk_of(c)))],
                pl.BlockSpec((nb, L, dim), rows))

    (in_f, out_f), (in_b, out_b) = dir_specs(0), dir_specs(1)
    h_tile = jax.ShapeDtypeStruct((n_rows, dim), BF16)
    h3 = jax.ShapeDtypeStruct((batch, seq, dim), BF16)
    q3 = q.reshape(batch, seq, dim)
    v3 = v.reshape(batch, seq, dim)
    gate3 = gate_cols.reshape(batch, seq, LANES)
    h_fwd, h_bwd = pl.pallas_call(
        functools.partial(_ml_chunk_kernel, chunk=L),
        grid=(batch // nb, nc),
        in_specs=in_f + in_b + [pl.BlockSpec((1, LANES), fixed),
                                pl.BlockSpec((2, ML_HEADS, 1), fixed3c), pl.BlockSpec((2, ML_HEADS, 1), fixed3c)],
        out_specs=[out_f, out_b],
        out_shape=[h3, h3],
        scratch_shapes=[pltpu.VMEM((nb, 2, ML_HEADS, ML_HEAD, 2 * ML_HEAD), F32),
                        pltpu.VMEM((nb, 2, ML_HEADS, LANES), F32)],
        compiler_params=_cparams("parallel", "arbitrary"),
        name="ml_chunk",
    )(q3, k_t, v3, gate3, g_rows, g_rows, q3, k_t, v3, gate3, g_rows, g_rows, bias_c, ib, fb)
    h_fwd = h_fwd.reshape(n_rows, dim)
    h_bwd = h_bwd.reshape(n_rows, dim)

    return pl.pallas_call(
        _ml_post_kernel,
        grid=(n_rows // ts,),
        in_specs=[pl.BlockSpec((ts, dim), row), pl.BlockSpec((ts, dim), row),
                  pl.BlockSpec((ts, dim), lambda i: (i, 1)),
                  pl.BlockSpec((ts, dim), row),
                  pl.BlockSpec((1, dim), fixed2), pl.BlockSpec((1, dim), fixed2)],
        out_specs=pl.BlockSpec((ts, dim), row),
        out_shape=h_tile,
        compiler_params=_cparams("parallel"),
        name="ml_post",
    )(h_fwd, h_bwd, cols, xc, p["norm_w"].reshape(1, dim), p["skip"].reshape(1, dim))


ROW_TILE = 512
MIXER_TILE = 512
FF_TILE = 2048


def _pad_cols(w, n):
    return jnp.pad(w, ((0, 0), (0, n - w.shape[1])))


def kernel(x, norm_mix, norm_mlp, norm_final, mlp_w1, mlp_w2, ab_w_in, ab_w_out, rw_mu, rw_w0, rw_w2, rw_a0, rw_a2, rw_g2, rw_k_k, rw_k_a, rw_r_k, rw_ln_w, mb_conv_w, mb_conv_b, mb_dt_bias, mb_A_log, mb_D, mb_norm_w, cd_w_in, cd_w_out, s5_A_re, s5_A_im, s5_log_dt, s5_B_re, s5_B_im, s5_C_re, s5_C_im, s5_D, s5_glu_w, s5_glu_b, ml_conv_w, ml_conv_b, ml_wq, ml_wk, ml_wv, ml_i_b, ml_f_b, ml_norm_w, ml_skip):
    batch, seq, dm = x.shape
    n_rows = batch * seq
    tm = min(ROW_TILE, seq)
    ts = min(MIXER_TILE, seq)
    tiles_per_seq = seq // tm
    h = x.reshape(n_rows, dm)
    depth = norm_mix.shape[0]
    for layer in range(depth):
        i = layer // 2
        last = layer == depth - 1
        if layer % 2 == 0:
            w_in = ab_w_in[i]
            rw_cols = _norm_matmul(h, norm_mix[layer], w_in[:, :RW_COLS].astype(BF16), tm=tm)
            mb_cols, mb_dt, mb_dt_t = _norm_matmul(h, norm_mix[layer],
                                                   _pad_cols(w_in[:, RW_COLS:], MB_COLS_PAD).astype(BF16),
                                                   tm=tm, seq=seq, tail=True)
            y1 = _rwkv7(rw_cols, dict(mu=rw_mu[i], w0=rw_w0[i], w2=rw_w2[i], a0=rw_a0[i], a2=rw_a2[i], g2=rw_g2[i],
                                      k_k=rw_k_k[i], k_a=rw_k_a[i], r_k=rw_r_k[i].reshape(-1), ln_w=rw_ln_w[i]),
                        batch=batch, seq=seq, ts=ts)
            y2 = _mamba2(mb_cols, mb_dt, mb_dt_t, dict(conv_w=mb_conv_w[i], conv_b=mb_conv_b[i], dt_bias=mb_dt_bias[i],
                                       A_log=mb_A_log[i], D=mb_D[i], norm_w=mb_norm_w[i]),
                         batch=batch, seq=seq, ts=ts)
            y1_spec = pl.BlockSpec((tm, RW_DIM), lambda r, k: (r, 0))
            w_out, k1 = ab_w_out[i], RW_DIM
        else:
            w_in = cd_w_in[i]
            tm_spec = pl.BlockSpec((tm, S5_DIM), lambda r: (r % tiles_per_seq, r // tiles_per_seq))
            s5_cols = _norm_matmul(h, norm_mix[layer], w_in[:, :S5_DIM].astype(BF16), tm=tm,
                                   out_spec=tm_spec, out_shape=(seq, batch * S5_DIM))
            ml_cols, ml_gates, ml_gates_t = _norm_matmul(h, norm_mix[layer],
                                                         _pad_cols(w_in[:, S5_DIM:], ML_COLS_PAD).astype(BF16),
                                                         tm=tm, seq=seq, tail=True)
            y1 = _s5(s5_cols.reshape(seq * batch, S5_DIM),
                     dict(A_re=s5_A_re[i], A_im=s5_A_im[i], log_dt=s5_log_dt[i], B_re=s5_B_re[i], B_im=s5_B_im[i],
                          C_re=s5_C_re[i], C_im=s5_C_im[i], D=s5_D[i], glu_w=s5_glu_w[i], glu_b=s5_glu_b[i]),
                     batch=batch, seq=seq).reshape(seq, batch * S5_DIM)
            y2 = _mlstm(ml_cols, ml_gates, ml_gates_t, dict(conv_w=ml_conv_w[i], conv_b=ml_conv_b[i], wq=ml_wq[i], wk=ml_wk[i], wv=ml_wv[i],
                                      i_b=ml_i_b[i], f_b=ml_f_b[i], norm_w=ml_norm_w[i], skip=ml_skip[i]),
                        batch=batch, seq=seq, ts=ts)
            y1_spec = pl.BlockSpec((tm, S5_DIM), lambda r, k: (r % tiles_per_seq, r // tiles_per_seq))
            w_out, k1 = cd_w_out[i], S5_DIM
        h = _mix_mlp(h, y1, y1_spec, y2, w_out[:k1].astype(BF16), w_out[k1:].astype(BF16), norm_mlp[layer],
                     mlp_w1[layer].astype(BF16), mlp_w2[layer].astype(BF16), norm_final,
                     tm=tm, tf=FF_TILE, final_norm=last)
    return h.reshape(batch, seq, dm)
```

```python
import functools
import math

import jax
import jax.numpy as jnp
import numpy as np
from jax import lax
from jax.experimental import pallas as pl
from jax.experimental.pallas import tpu as pltpu

F32 = jnp.float32
BF16 = jnp.bfloat16

EPS = 1e-5
LANES = 128
SUBLANES = 8
VMEM_LIMIT_BYTES = 48 * 1024 * 1024

D_MODEL = 1024
D_FF = 4 * D_MODEL

RW_HEADS, RW_HEAD = 8, 64
RW_DIM = RW_HEADS * RW_HEAD
RW_LR = 64 + 64 + 128
RW_COLS = 3 * RW_DIM + RW_LR
RW_GN_EPS = 64e-5
RW_CHUNK = 64
RW_BATCH_PER_STEP = 4

MB_HEADS, MB_HEADDIM, MB_GROUPS, MB_STATE = 16, 64, 2, 128
MB_HPG = MB_HEADS // MB_GROUPS
MB_DIM = MB_HEADS * MB_HEADDIM
MB_BC = 2 * MB_GROUPS * MB_STATE
MB_COLS_PAD = 2 * MB_DIM + MB_BC + LANES
MB_CHUNK = 128
MB_BATCH_PER_STEP = 4


def _cparams(*sem):
    return pltpu.CompilerParams(dimension_semantics=sem, vmem_limit_bytes=VMEM_LIMIT_BYTES)


def _bdot(a, b):
    return jnp.dot(a.astype(BF16), b.astype(BF16), preferred_element_type=F32)


def _bdot_nt(a, b):
    return lax.dot_general(a.astype(BF16), b.astype(BF16), (((1,), (1,)), ((), ())),
                           preferred_element_type=F32)


def _bdot_tn(a, b):
    return lax.dot_general(a.astype(BF16), b.astype(BF16), (((0,), (0,)), ((), ())),
                           preferred_element_type=F32)


def _split3(x):
    hi = x.astype(BF16)
    r = x - hi.astype(F32)
    mid = r.astype(BF16)
    lo = (r - mid.astype(F32)).astype(BF16)
    return hi, mid, lo


def _xdot_l(x, m):
    hi, mid, lo = _split3(x)
    dot = functools.partial(jnp.dot, preferred_element_type=F32)
    return dot(lo, m) + dot(mid, m) + dot(hi, m)


def _xdot_r(m, x):
    hi, mid, lo = _split3(x)
    dot = functools.partial(jnp.dot, preferred_element_type=F32)
    return dot(m, lo) + dot(m, mid) + dot(m, hi)


def _softplus(u):
    return jnp.maximum(u, 0.0) + jnp.log(1.0 + jnp.exp(-jnp.abs(u)))


def _sigmoid(u):
    return 1.0 / (1.0 + jnp.exp(-u))


def _rms(x, gain):
    return x * lax.rsqrt(jnp.mean(x * x, axis=-1, keepdims=True) + EPS) * gain


def _shifted(x, prev_row, next_row):
    n, w = x.shape
    rows = lax.broadcasted_iota(jnp.int32, (SUBLANES, w), 0)
    down = pltpu.roll(x, 1, 0)
    up = pltpu.roll(x, n - 1, 0)
    x_prev = jnp.concatenate([jnp.where(rows == 0, prev_row, down[:SUBLANES]), down[SUBLANES:]], axis=0)
    x_next = jnp.concatenate([up[:n - SUBLANES], jnp.where(rows == SUBLANES - 1, next_row, up[n - SUBLANES:])],
                             axis=0)
    return x_prev, x_next


HALO_ROWS = 2 * SUBLANES


def _halo_rows(xp_ref, xn_ref, tiles_per_seq):
    si = pl.program_id(0) % tiles_per_seq
    prev_row = jnp.where(si == 0, 0.0, xp_ref[...].astype(F32)[HALO_ROWS - 1:HALO_ROWS, :])
    next_row = jnp.where(si == tiles_per_seq - 1, 0.0, xn_ref[...].astype(F32)[0:1, :])
    return prev_row, next_row


def _halo_specs(ts, width, col_block, n_rows):
    per = ts // HALO_ROWS
    last = n_rows // HALO_ROWS - 1
    prev = pl.BlockSpec((HALO_ROWS, width), lambda i: (jnp.maximum(i * per - 1, 0), col_block))
    nxt = pl.BlockSpec((HALO_ROWS, width), lambda i: (jnp.minimum((i + 1) * per, last), col_block))
    return prev, nxt


def _scan_masks(n, period, sgn):
    ri = lax.broadcasted_iota(jnp.int32, (n, n), 0)
    ci = lax.broadcasted_iota(jnp.int32, (n, n), 1)
    delta = ((ci & (period - 1)) - (ri & (period - 1))) * sgn
    return delta < 0, delta <= 0, ri == ci


def _norm_mm_kernel(x_ref, g_ref, w_ref, *rest):
    xn = _rms(x_ref[...], g_ref[...]).astype(BF16)
    res = jnp.dot(xn, w_ref[...], preferred_element_type=F32)
    if len(rest) == 1:
        rest[0][...] = res.astype(rest[0].dtype)
    else:
        wt_ref, o_ref, tail_ref, tail_t_ref = rest
        o_ref[...] = res[:, :-LANES].astype(o_ref.dtype)
        tail_ref[...] = res[:, -LANES:]
        tail_t_ref[0] = lax.dot_general(wt_ref[...], xn, (((1,), (1,)), ((), ())), preferred_element_type=F32)


def _norm_matmul(h, gain, w, *, tm, seq=None, tail=False, out_spec=None, out_shape=None):
    m, dm = h.shape
    n = w.shape[1]
    n_main = n - LANES if tail else n
    if out_spec is None:
        out_spec = pl.BlockSpec((tm, n_main), lambda i: (i, 0))
        out_shape = (m, n_main)
    in_specs = [pl.BlockSpec((tm, dm), lambda i: (i, 0)),
                pl.BlockSpec((1, dm), lambda i: (0, 0)),
                pl.BlockSpec((dm, n), lambda i: (0, 0))]
    args = [h, gain.reshape(1, dm), w]
    out_specs = [out_spec]
    out_shapes = [jax.ShapeDtypeStruct(out_shape, BF16)]
    if tail:
        tps = seq // tm
        in_specs.append(pl.BlockSpec((LANES, dm), lambda i: (0, 0)))
        args.append(w[:, -LANES:].T)
        out_specs += [pl.BlockSpec((tm, LANES), lambda i: (i, 0)),
                      pl.BlockSpec((1, LANES, tm), lambda i: (i // tps, 0, i % tps))]
        out_shapes += [jax.ShapeDtypeStruct((m, LANES), F32), jax.ShapeDtypeStruct((m // seq, LANES, seq), F32)]
    res = pl.pallas_call(
        _norm_mm_kernel,
        grid=(m // tm,),
        in_specs=in_specs,
        out_specs=out_specs,
        out_shape=out_shapes,
        compiler_params=_cparams("parallel"),
        name="norm_matmul",
    )(*args)
    return res if tail else res[0]


def _mix_mlp_kernel(h_ref, y1_ref, y2_ref, wo1_ref, wo2_ref, gm_ref, w1_ref, w2_ref, gf_ref,
                    o_ref, h1_s, xn_s, acc_s, *, final_norm):
    kf = pl.program_id(1)

    @pl.when(kf == 0)
    def _():
        h1 = h_ref[...] + _bdot(y1_ref[...], wo1_ref[...]) + _bdot(y2_ref[...], wo2_ref[...])
        h1_s[...] = h1
        xn_s[...] = _rms(h1, gm_ref[...]).astype(BF16)
        acc_s[...] = jnp.zeros_like(acc_s)

    hid = jnp.dot(xn_s[...], w1_ref[...], preferred_element_type=F32)
    hid = jnp.square(jnp.maximum(hid, 0.0))
    acc_s[...] += jnp.dot(hid.astype(BF16), w2_ref[...], preferred_element_type=F32)

    @pl.when(kf == pl.num_programs(1) - 1)
    def _():
        out = h1_s[...] + acc_s[...]
        if final_norm:
            out = _rms(out, gf_ref[...])
        o_ref[...] = out


def _mix_mlp(h, y1, y1_spec, y2, wo1, wo2, g_mlp, w1, w2, g_final, *, tm, tf, final_norm):
    m, dm = h.shape
    ff = w1.shape[1]
    k2 = y2.shape[1]
    row = lambda i, k: (i, 0)
    fixed = lambda i, k: (0, 0)
    return pl.pallas_call(
        functools.partial(_mix_mlp_kernel, final_norm=final_norm),
        grid=(m // tm, ff // tf),
        in_specs=[pl.BlockSpec((tm, dm), row),
                  y1_spec,
                  pl.BlockSpec((tm, k2), row),
                  pl.BlockSpec(wo1.shape, fixed),
                  pl.BlockSpec(wo2.shape, fixed),
                  pl.BlockSpec((1, dm), fixed),
                  pl.BlockSpec((dm, tf), lambda i, k: (0, k)),
                  pl.BlockSpec((tf, dm), lambda i, k: (k, 0)),
                  pl.BlockSpec((1, dm), fixed)],
        out_specs=pl.BlockSpec((tm, dm), row),
        out_shape=jax.ShapeDtypeStruct((m, dm), F32),
        scratch_shapes=[pltpu.VMEM((tm, dm), F32), pltpu.VMEM((tm, dm), BF16), pltpu.VMEM((tm, dm), F32)],
        compiler_params=_cparams("parallel", "arbitrary"),
        name="mix_mlp",
    )(h, y1, y2, wo1, wo2, g_mlp.reshape(1, dm), w1, w2, g_final.reshape(1, dm))


def _dwconv_silu_kernel(x_ref, xp_ref, xn_ref, w_ref, b_ref, *rest, tiles_per_seq):
    x = x_ref[...].astype(F32)
    prev_row, next_row = _halo_rows(xp_ref, xn_ref, tiles_per_seq)
    x_prev, x_next = _shifted(x, prev_row, next_row)
    w = w_ref[...]
    y = w[0:1] * x_prev + w[1:2] * x + w[2:3] * x_next + b_ref[...]
    out = (y * _sigmoid(y)).astype(BF16)
    if len(rest) == 1:
        rest[0][...] = out
    else:
        eye_ref, o_ref, ot_ref = rest
        o_ref[...] = out
        n = eye_ref.shape[0]
        ot_ref[0] = lax.dot_general(eye_ref[...], out[:, :n], (((1,), (1,)), ((), ())),
                                    preferred_element_type=F32).astype(BF16)


def _dwconv_silu(cols, col_block, width, w, b, *, seq, ts, t_cols=0):
    n_rows = cols.shape[0]
    prev, nxt = _halo_specs(ts, width, col_block, n_rows)
    in_specs = [pl.BlockSpec((ts, width), lambda i: (i, col_block)), prev, nxt,
                pl.BlockSpec((3, width), lambda i: (0, 0)),
                pl.BlockSpec((1, width), lambda i: (0, 0))]
    args = [cols, cols, cols, w, b.reshape(1, width)]
    out_specs = [pl.BlockSpec((ts, width), lambda i: (i, 0))]
    out_shapes = [jax.ShapeDtypeStruct((n_rows, width), BF16)]
    if t_cols:
        tps = seq // ts
        in_specs.append(pl.BlockSpec((t_cols, t_cols), lambda i: (0, 0)))
        args.append(jnp.eye(t_cols, dtype=BF16))
        out_specs.append(pl.BlockSpec((1, t_cols, ts), lambda i: (i // tps, 0, i % tps)))
        out_shapes.append(jax.ShapeDtypeStruct((n_rows // seq, t_cols, seq), BF16))
    res = pl.pallas_call(
        functools.partial(_dwconv_silu_kernel, tiles_per_seq=seq // ts),
        grid=(n_rows // ts,),
        in_specs=in_specs,
        out_specs=out_specs,
        out_shape=out_shapes,
        compiler_params=_cparams("parallel"),
        name="dwconv_silu",
    )(*args)
    return res if t_cols else res[0]


def _rw_prep_kernel(x_ref, xp_ref, xn_ref, mu_ref, w0_ref, w2_ref, a0_ref, a2_ref, g2_ref,
                    kk_ref, ka_ref, rk_ref, ones_ref,
                    r_o, k_o, v_o, kk_o, b_o, lw_o, bonus_o, g_o, *, tiles_per_seq):
    x = x_ref[...].astype(F32)
    prev_row, next_row = _halo_rows(xp_ref, xn_ref, tiles_per_seq)
    x_prev, x_next = _shifted(x, prev_row, next_row)
    mu = mu_ref[...]
    xs = x + mu[0:1] * (x_prev - x) + mu[1:2] * (x_next - x)
    r = xs[:, 0:RW_DIM]
    k = xs[:, RW_DIM:2 * RW_DIM]
    v = xs[:, 2 * RW_DIM:3 * RW_DIM]
    lr = xs[:, 3 * RW_DIM:3 * RW_DIM + LANES]
    g_lr = xs[:, 3 * RW_DIM + LANES:3 * RW_DIM + 2 * LANES]
    th = jnp.tanh(lr)
    for d in range(2):
        z = w0_ref[d:d + 1, :] + _bdot(th, w2_ref[d])
        lw_o[d] = -jnp.exp(-_softplus(-z) - 0.5)
    a_gate = _sigmoid(a0_ref[...] + _bdot(lr, a2_ref[...]))
    g_o[...] = _bdot(_sigmoid(g_lr), g2_ref[...])
    ones = ones_ref[...]
    kk = k * kk_ref[...]
    kk = kk * lax.rsqrt(jnp.maximum(_xdot_l(kk * kk, ones), 1e-12))
    k2 = k * (1.0 + (a_gate - 1.0) * ka_ref[...])
    r_o[...] = r.astype(BF16)
    k_o[...] = k2.astype(BF16)
    v_o[...] = v.astype(BF16)
    kk_o[...] = kk.astype(BF16)
    b_o[...] = (kk * a_gate).astype(BF16)
    bonus_o[...] = _xdot_l(r * k2 * rk_ref[...], ones) * v


def _rw_chunk_kernel(*refs, chunk):
    L = chunk
    ins = (refs[0:6], refs[6:12])
    y_refs = refs[12:14]
    s_ref = refs[14]

    @pl.when(pl.program_id(1) == 0)
    def _():
        s_ref[...] = jnp.zeros_like(s_ref)

    n2 = 2 * L
    lane = lax.broadcasted_iota(jnp.int32, (L, LANES), 1)
    first = lane < RW_HEAD
    eye_l = lane == lax.broadcasted_iota(jnp.int32, (L, LANES), 0)
    zeros_l = jnp.zeros((L, LANES), BF16)
    rows2 = lax.broadcasted_iota(jnp.int32, (n2, n2), 0)
    cols2 = lax.broadcasted_iota(jnp.int32, (n2, n2), 1)

    def top(t):
        return jnp.concatenate([t.astype(BF16), zeros_l], axis=0)

    def bottom(t):
        return jnp.concatenate([zeros_l, t.astype(BF16)], axis=0)

    pairs = []
    chains = []
    for bs, dd in [(bs, dd) for bs in range(y_refs[0].shape[0]) for dd in range(2)]:
        r_ref, k_ref, v_ref, kk_ref, b_ref, lw_ref = ins[dd]
        sgn = 1 - 2 * dd
        _, incl_l, _ = _scan_masks(L, L, sgn)
        tri = jnp.where(incl_l, 1.0, 0.0).astype(BF16)
        lw = lw_ref[0, bs]
        c_incl = _xdot_r(tri, lw)
        c_tot = jnp.sum(lw, axis=0, keepdims=True)
        e_in = jnp.exp(c_incl)
        e_neg = jnp.exp(-c_incl)
        e_tot = jnp.exp(c_tot)
        e_rem = e_tot * e_neg
        kk = kk_ref[bs]
        bb = b_ref[bs]
        k = k_ref[bs]
        v = v_ref[bs]
        rt = r_ref[bs] * e_in
        at = -kk * jnp.exp(c_incl - lw)
        bt = bb * e_neg
        kt = k * e_neg
        bh = bb * e_rem
        kh = k * e_rem
        delta = ((cols2 & (L - 1)) - (rows2 & (L - 1))) * sgn
        g_mask = delta < jnp.where(rows2 < L, 0, 1)
        for p in range(RW_DIM // LANES):
            sl = slice(p * LANES, (p + 1) * LANES)
            s = s_ref[bs, dd, p]
            pair = dict(bs=bs, dd=dd, p=p, sl=sl, s=s, s_t=s.T.astype(BF16), e_tot=e_tot[:, sl], heads=[])
            pairs.append(pair)
            for hh in range(2):
                keep = first if hh == 0 else lane >= RW_HEAD
                head = lambda t: jnp.where(keep, t[:, sl], 0.0).astype(BF16)
                c = dict(pair=pair, g_mask=g_mask,
                         ar=jnp.concatenate([head(at), head(rt)], axis=0),
                         bk=jnp.concatenate([head(bt), head(kt)], axis=0),
                         bh=head(bh), kh=head(kh), v=head(v))
                pair["heads"].append(c)
                chains.append(c)

    for c in chains:
        c["g"] = jnp.where(c["g_mask"], _bdot_nt(c["ar"], c["bk"]), 0.0)
    for c in chains:
        g_a = c["g"][:L]
        c["ym"] = jnp.where(first, jnp.where(eye_l, 1.0, 0.0), pltpu.roll(g_a, L, 1))
        c["q"] = _bdot(g_a, top(c["ym"]))
        lhs = jnp.concatenate([c["ar"], jnp.where(cols2 < L, 0.0, c["g"]).astype(BF16)], axis=1)
        c["wy"] = _bdot(lhs, jnp.concatenate([c["pair"]["s_t"], zeros_l, c["v"]], axis=0))
    n_stage = int(math.log2(L))
    for i in range(1, n_stage):
        for c in chains:
            c["ym"] = jnp.where(first, c["ym"], 0.0) + c["q"]
            c["q"] = _bdot(c["q"], bottom(c["ym"]))
    for c in chains:
        c["ym"] = jnp.where(first, c["ym"], 0.0) + c["q"]
        c["u"] = _bdot(c["ym"], top(c["wy"][:L]))
    for c in chains:
        c["y"] = c["wy"][L:] + _bdot(c["g"][L:], top(c["u"]))
    for pair in pairs:
        h0, h1 = pair["heads"]
        y_refs[pair["dd"]][pair["bs"], :, pair["sl"]] = (h0["y"] + h1["y"]).astype(y_refs[pair["dd"]].dtype)
        uv = jnp.concatenate([h0["u"].astype(BF16), h1["u"].astype(BF16), h0["v"], h1["v"]], axis=0)
        bk = jnp.concatenate([h0["bh"], h1["bh"], h0["kh"], h1["kh"]], axis=0)
        s_ref[pair["bs"], pair["dd"], pair["p"]] = pair["s"] * pair["e_tot"] + _bdot_tn(uv, bk)


def _rw_post_kernel(yf_ref, yb_ref, bonus_ref, g_ref, lnw_ref, ones_ref, o_ref):
    y = yf_ref[...].astype(F32) + yb_ref[...].astype(F32)
    ones = ones_ref[...]
    yc = y - _xdot_l(y, ones) * (1.0 / RW_HEAD)
    var = _xdot_l(yc * yc, ones) * (1.0 / RW_HEAD)
    yn = yc * lax.rsqrt(var + RW_GN_EPS) * lnw_ref[...]
    o_ref[...] = ((yn + bonus_ref[...]) * g_ref[...]).astype(o_ref.dtype)


def _rwkv7(cols, p, *, batch, seq, ts):
    n_rows = cols.shape[0]
    dim = RW_DIM
    row = lambda i: (i, 0)
    fixed2 = lambda i: (0, 0)
    fixed3 = lambda i: (0, 0, 0)
    prev, nxt = _halo_specs(ts, RW_COLS, 0, n_rows)
    ones = jnp.asarray(np.kron(np.eye(RW_HEADS), np.ones((RW_HEAD, RW_HEAD))), BF16)
    zeros = jnp.zeros((64, dim), F32)
    w2 = jnp.concatenate([p["w2"], jnp.broadcast_to(zeros, (2, 64, dim))], axis=1).astype(BF16)
    a2 = jnp.concatenate([zeros, p["a2"]], axis=0).astype(BF16)
    vec = lambda t: t.reshape(1, dim)
    tile = jax.ShapeDtypeStruct((n_rows, dim), F32)
    btile = jax.ShapeDtypeStruct((n_rows, dim), BF16)
    r, k, v, kk, b, lw, bonus, g = pl.pallas_call(
        functools.partial(_rw_prep_kernel, tiles_per_seq=seq // ts),
        grid=(n_rows // ts,),
        in_specs=[pl.BlockSpec((ts, RW_COLS), row), prev, nxt,
                  pl.BlockSpec((2, RW_COLS), fixed2),
                  pl.BlockSpec((2, dim), fixed2),
                  pl.BlockSpec((2, LANES, dim), fixed3),
                  pl.BlockSpec((1, dim), fixed2),
                  pl.BlockSpec((LANES, dim), fixed2),
                  pl.BlockSpec((LANES, dim), fixed2),
                  pl.BlockSpec((1, dim), fixed2),
                  pl.BlockSpec((1, dim), fixed2),
                  pl.BlockSpec((1, dim), fixed2),
                  pl.BlockSpec((dim, dim), fixed2)],
        out_specs=[pl.BlockSpec((ts, dim), row)] * 5
                  + [pl.BlockSpec((2, ts, dim), lambda i: (0, i, 0))]
                  + [pl.BlockSpec((ts, dim), row)] * 2,
        out_shape=[btile] * 5 + [jax.ShapeDtypeStruct((2, n_rows, dim), F32)] + [tile] * 2,
        compiler_params=_cparams("parallel"),
        name="rw_prep",
    )(cols, cols, cols, p["mu"], p["w0"], w2, vec(p["a0"]), a2, p["g2"].astype(BF16),
      vec(p["k_k"]), vec(p["k_a"]), vec(p["r_k"]), ones)

    L = RW_CHUNK
    nc = seq // L
    nb = min(RW_BATCH_PER_STEP, batch)
    fwd = lambda bi, c: (bi, c, 0)
    bwd = lambda bi, c: (bi, nc - 1 - c, 0)
    by_seq = lambda t: t.reshape(batch, seq, dim)
    r3, k3, v3, kk3, b3 = by_seq(r), by_seq(k), by_seq(v), by_seq(kk), by_seq(b)
    lw4 = lw.reshape(2, batch, seq, dim)
    y3 = jax.ShapeDtypeStruct((batch, seq, dim), BF16)
    y_fwd, y_bwd = pl.pallas_call(
        functools.partial(_rw_chunk_kernel, chunk=L),
        grid=(batch // nb, nc),
        in_specs=[pl.BlockSpec((nb, L, dim), fwd)] * 5
                 + [pl.BlockSpec((1, nb, L, dim), lambda bi, c: (0, bi, c, 0))]
                 + [pl.BlockSpec((nb, L, dim), bwd)] * 5
                 + [pl.BlockSpec((1, nb, L, dim), lambda bi, c: (1, bi, nc - 1 - c, 0))],
        out_specs=[pl.BlockSpec((nb, L, dim), fwd), pl.BlockSpec((nb, L, dim), bwd)],
        out_shape=[y3, y3],
        scratch_shapes=[pltpu.VMEM((nb, 2, dim // LANES, LANES, LANES), F32)],
        compiler_params=_cparams("parallel", "arbitrary"),
        name="rw_chunk",
    )(r3, k3, v3, kk3, b3, lw4, r3, k3, v3, kk3, b3, lw4)
    y_fwd = y_fwd.reshape(n_rows, dim)
    y_bwd = y_bwd.reshape(n_rows, dim)

    return pl.pallas_call(
        _rw_post_kernel,
        grid=(n_rows // ts,),
        in_specs=[pl.BlockSpec((ts, dim), row), pl.BlockSpec((ts, dim), row),
                  pl.BlockSpec((ts, dim), row), pl.BlockSpec((ts, dim), row),
                  pl.BlockSpec((1, dim), fixed2), pl.BlockSpec((dim, dim), fixed2)],
        out_specs=pl.BlockSpec((ts, dim), row),
        out_shape=btile,
        compiler_params=_cparams("parallel"),
        name="rw_post",
    )(y_fwd, y_bwd, bonus, g, vec(p["ln_w"]), ones)


def _mb_ssd_kernel(*refs, chunk):
    L = chunk
    ins = (refs[0:5], refs[5:10])
    bias_c_ref, alog_c_ref, bias_r_ref, alog_r_ref, e1_ref = refs[10:15]
    y_refs = refs[15:17]
    st_ref = refs[17]

    @pl.when(pl.program_id(1) == 0)
    def _():
        st_ref[...] = jnp.zeros_like(st_ref)

    ri = lax.broadcasted_iota(jnp.int32, (L, L), 0)
    ci = lax.broadcasted_iota(jnp.int32, (L, L), 1)
    first = lax.broadcasted_iota(jnp.int32, (L, LANES), 1) < MB_HEADDIM
    gw = MB_HPG * MB_HEADDIM
    groups = []
    for bs, dd in [(bs, dd) for bs in range(y_refs[0].shape[0]) for dd in range(2)]:
        xs_ref, bc_ref, bt_ref, dtc_ref, dtr_ref = ins[dd]
        sgn = 1 - 2 * dd
        incl = (ci - ri) * sgn <= 0
        tri = jnp.where(incl, 1.0, 0.0).astype(BF16)
        tri_t = jnp.where((ri - ci) * sgn <= 0, 1.0, 0.0).astype(BF16)
        last = L - 1 if dd == 0 else 0
        dt_c = _softplus(dtc_ref[bs] + bias_c_ref[...])
        cs_c = _xdot_r(tri, dt_c * (-jnp.exp(alog_c_ref[...])))
        cols2 = jnp.concatenate([jnp.exp(cs_c), dt_c * jnp.exp(cs_c[last:last + 1, :] - cs_c)], axis=0)
        hi = cols2.astype(BF16)
        mid = (cols2 - hi.astype(F32)).astype(BF16)
        full2 = jnp.dot(jnp.concatenate([hi, mid], axis=1), e1_ref[dd], preferred_element_type=F32)
        ecs = full2[:L]
        xw = (xs_ref[bs] * full2[L:]).astype(BF16)
        etot = ecs[last:last + 1, :]
        xb = xs_ref[bs].astype(BF16)
        dt_r = _softplus(dtr_ref[bs] + bias_r_ref[dd])
        cs_r = _xdot_l(dt_r * (-jnp.exp(alog_r_ref[dd])), tri_t)
        for g in range(MB_GROUPS):
            gs = slice(g * gw, (g + 1) * gw)
            groups.append(dict(
                bs=bs, dd=dd, g=g, gs=gs, incl=incl, ecs=ecs[:, gs], etot=etot[:, gs], xw=xw[:, gs], xb=xb[:, gs],
                cs_c=cs_c, cs_r=cs_r, dt_r=dt_r,
                bg=bc_ref[bs, :, g * MB_STATE:(g + 1) * MB_STATE].astype(BF16),
                cg=bc_ref[bs, :, (MB_GROUPS + g) * MB_STATE:(MB_GROUPS + g + 1) * MB_STATE].astype(BF16),
                bt=bt_ref[bs, g * MB_STATE:(g + 1) * MB_STATE, :],
                st=st_ref[bs, dd, :, gs]))

    for c in groups:
        c["scores"] = _bdot_nt(c["cg"], c["bg"])
        c["y_off"] = _bdot(c["cg"], c["st"]) * c["ecs"]
        st_ref[c["bs"], c["dd"], :, c["gs"]] = c["st"] * c["etot"] + _bdot(c["bt"], c["xw"])
    for c in groups:
        pairs = []
        for j in range(MB_HPG // 2):
            xp = c["xb"][:, j * LANES:(j + 1) * LANES]
            halves = []
            for hh in range(2):
                h = c["g"] * MB_HPG + 2 * j + hh
                lane = MB_HEADS * c["dd"] + h
                b_t = jnp.broadcast_to(c["cs_c"][:, lane:lane + 1], (L, L))
                expo = jnp.minimum(b_t - c["cs_r"][h:h + 1, :], 0.0)
                m = jnp.where(c["incl"], c["scores"] * jnp.exp(expo) * c["dt_r"][h:h + 1, :], 0.0)
                halves.append(_bdot(m, xp))
            pairs.append(jnp.where(first, halves[0], halves[1]))
        y_refs[c["dd"]][c["bs"], :, c["gs"]] = (c["y_off"]
                                                + jnp.concatenate(pairs, axis=1)).astype(y_refs[c["dd"]].dtype)


def _mb_post_kernel(yf_ref, yb_ref, xs_ref, z_ref, d_ref, nw_ref, o_ref):
    y = yf_ref[...].astype(F32) + yb_ref[...].astype(F32) + d_ref[...] * xs_ref[...]
    z = z_ref[...].astype(F32)
    y = y * (z * _sigmoid(z))
    gw = MB_DIM // MB_GROUPS
    for g in range(MB_GROUPS):
        yg = y[:, g * gw:(g + 1) * gw]
        o_ref[:, g * gw:(g + 1) * gw] = (yg * lax.rsqrt(jnp.mean(yg * yg, axis=-1, keepdims=True) + EPS)
                                          * nw_ref[:, g * gw:(g + 1) * gw]).astype(o_ref.dtype)


def _mamba2(cols, dt_cols, dt_rows, p, *, batch, seq, ts):
    n_rows = cols.shape[0]
    xs = _dwconv_silu(cols, 1, MB_DIM, p["conv_w"][:, :MB_DIM], p["conv_b"][:MB_DIM], seq=seq, ts=ts)
    bc, b_t = _dwconv_silu(cols, 2 * MB_DIM // MB_BC, MB_BC, p["conv_w"][:, MB_DIM:], p["conv_b"][MB_DIM:],
                           seq=seq, ts=ts, t_cols=MB_GROUPS * MB_STATE)
    L = MB_CHUNK
    nc = seq // L
    pad = jnp.zeros((LANES - 2 * MB_HEADS,), F32)
    bias_c = jnp.concatenate([p["dt_bias"].reshape(-1), pad]).reshape(1, LANES)
    alog_c = jnp.concatenate([p["A_log"].reshape(-1), pad]).reshape(1, LANES)
    bias_r = p["dt_bias"].reshape(2, MB_HEADS, 1)
    alog_r = p["A_log"].reshape(2, MB_HEADS, 1)
    e1 = np.zeros((2, 2 * LANES, MB_DIM), np.float32)
    for d in range(2):
        for h in range(MB_HEADS):
            e1[d, MB_HEADS * d + h, h * MB_HEADDIM:(h + 1) * MB_HEADDIM] = 1.0
            e1[d, LANES + MB_HEADS * d + h, h * MB_HEADDIM:(h + 1) * MB_HEADDIM] = 1.0
    fixed = lambda bi, c: (0, 0)
    fixed3c = lambda bi, c: (0, 0, 0)

    nb = min(MB_BATCH_PER_STEP, batch)

    def dir_specs(d):
        chunk_of = (lambda c: c) if d == 0 else (lambda c: nc - 1 - c)
        rows = lambda bi, c: (bi, chunk_of(c), 0)
        return ([pl.BlockSpec((nb, L, MB_DIM), rows),
                 pl.BlockSpec((nb, L, MB_BC), rows),
                 pl.BlockSpec((nb, MB_GROUPS * MB_STATE, L), lambda bi, c: (bi, 0, chunk_of(c))),
                 pl.BlockSpec((nb, L, LANES), rows),
                 pl.BlockSpec((nb, MB_HEADS, L), lambda bi, c: (bi, d, chunk_of(c)))],
                pl.BlockSpec((nb, L, MB_DIM), rows))

    (in_f, out_f), (in_b, out_b) = dir_specs(0), dir_specs(1)
    y_tile = jax.ShapeDtypeStruct((n_rows, MB_DIM), BF16)
    y3 = jax.ShapeDtypeStruct((batch, seq, MB_DIM), BF16)
    xs3 = xs.reshape(batch, seq, MB_DIM)
    bc3 = bc.reshape(batch, seq, MB_BC)
    dt3 = dt_cols.reshape(batch, seq, LANES)
    y_fwd, y_bwd = pl.pallas_call(
        functools.partial(_mb_ssd_kernel, chunk=L),
        grid=(batch // nb, nc),
        in_specs=in_f + in_b + [pl.BlockSpec((1, LANES), fixed), pl.BlockSpec((1, LANES), fixed),
                                pl.BlockSpec((2, MB_HEADS, 1), fixed3c), pl.BlockSpec((2, MB_HEADS, 1), fixed3c),
                                pl.BlockSpec((2, 2 * LANES, MB_DIM), fixed3c)],
        out_specs=[out_f, out_b],
        out_shape=[y3, y3],
        scratch_shapes=[pltpu.VMEM((nb, 2, MB_STATE, MB_DIM), F32)],
        compiler_params=_cparams("parallel", "arbitrary"),
        name="mb_ssd",
    )(xs3, bc3, b_t, dt3, dt_rows, xs3, bc3, b_t, dt3, dt_rows, bias_c, alog_c, bias_r, alog_r,
      jnp.asarray(e1, BF16))
    y_fwd = y_fwd.reshape(n_rows, MB_DIM)
    y_bwd = y_bwd.reshape(n_rows, MB_DIM)

    row = lambda i: (i, 0)
    fixed2 = lambda i: (0, 0)
    return pl.pallas_call(
        _mb_post_kernel,
        grid=(n_rows // ts,),
        in_specs=[pl.BlockSpec((ts, MB_DIM), row), pl.BlockSpec((ts, MB_DIM), row),
                  pl.BlockSpec((ts, MB_DIM), row),
                  pl.BlockSpec((ts, MB_DIM), row),
                  pl.BlockSpec((1, MB_DIM), fixed2), pl.BlockSpec((1, MB_DIM), fixed2)],
        out_specs=pl.BlockSpec((ts, MB_DIM), row),
        out_shape=y_tile,
        compiler_params=_cparams("parallel"),
        name="mb_post",
    )(y_fwd, y_bwd, xs, cols, jnp.repeat(p["D"], MB_HEADDIM).reshape(1, MB_DIM), p["norm_w"].reshape(1, MB_DIM))


S5_GROUP, S5_GROUPS, S5_STATE = 16, 32, 64
S5_DIM = S5_GROUP * S5_GROUPS
S5_GPB = LANES // S5_GROUP
S5_BLOCKS = S5_GROUPS // S5_GPB
S5_HALF = S5_GPB * S5_STATE
S5_STEPS = 32


def _s5_disc_kernel(ar_ref, ai_ref, ldt_ref, bre_ref, bim_ref, abr_o, abi_o, bbr_o, bbi_o):
    dt = jnp.exp(ldt_ref[0])
    ar = jnp.minimum(ar_ref[0], -1e-4)
    ai = ai_ref[0]
    mag = jnp.exp(dt * ar)
    abr = mag * jnp.cos(dt * ai)
    abi = mag * jnp.sin(dt * ai)
    den = ar * ar + ai * ai
    fr = ((abr - 1.0) * ar + abi * ai) / den
    fi = (abi * ar - (abr - 1.0) * ai) / den
    bre = bre_ref[...]
    bim = bim_ref[...]
    abr_o[0] = abr
    abi_o[0] = abi
    bbr_o[0] = fr * bre - fi * bim
    bbi_o[0] = fr * bim + fi * bre


def _s5_scan_kernel(uf_ref, ub_ref, bw_ref, cw_ref, lr_ref, li_ref, yf_ref, yb_ref, x_s, st_s, *, batch, steps):
    @pl.when(pl.program_id(0) == 0)
    def _():
        st_s[...] = jnp.zeros_like(st_s)

    u_refs = (uf_ref, ub_ref)
    y_refs = (yf_ref, yb_ref)
    width = 2 * S5_HALF
    for dd in range(2):
        u = u_refs[dd][...].astype(BF16)
        for j in range(S5_BLOCKS):
            x_s[dd, :, j * width:(j + 1) * width] = jnp.dot(u[:, j * LANES:(j + 1) * LANES], bw_ref[dd, j],
                                                             preferred_element_type=F32)
    for dd in range(2):
        for j in range(S5_BLOCKS):
            re = slice(j * width, j * width + S5_HALF)
            im = slice(j * width + S5_HALF, (j + 1) * width)
            lam_r = jnp.broadcast_to(lr_ref[dd, :, j * S5_HALF:(j + 1) * S5_HALF], (batch, S5_HALF))
            lam_i = jnp.broadcast_to(li_ref[dd, :, j * S5_HALF:(j + 1) * S5_HALF], (batch, S5_HALF))
            xr = st_s[dd, :, re]
            xi = st_s[dd, :, im]
            for i in range(steps):
                t = i if dd == 0 else steps - 1 - i
                rows = slice(t * batch, (t + 1) * batch)
                xr, xi = (lam_r * xr - lam_i * xi + x_s[dd, rows, re],
                          lam_r * xi + lam_i * xr + x_s[dd, rows, im])
                x_s[dd, rows, re] = xr
                x_s[dd, rows, im] = xi
            st_s[dd, :, re] = xr
            st_s[dd, :, im] = xi
    for dd in range(2):
        for j in range(S5_BLOCKS):
            y_refs[dd][:, j * LANES:(j + 1) * LANES] = jnp.dot(
                x_s[dd, :, j * width:(j + 1) * width].astype(BF16), cw_ref[dd, j],
                preferred_element_type=F32).astype(y_refs[dd].dtype)


def _s5_post_kernel(yf_ref, yb_ref, u_ref, d_ref, gw_ref, gb_ref, o_ref):
    y = d_ref[...] * u_ref[...] + yf_ref[...].astype(F32) + yb_ref[...].astype(F32)
    y = 0.5 * y * (1.0 + jnp.tanh(math.sqrt(2.0 / math.pi) * (y + 0.044715 * (y * y * y))))
    o_ref[...] = (y * _sigmoid(_bdot(y, gw_ref[...]) + gb_ref[...])).astype(o_ref.dtype)


def _s5(u, p, *, batch, seq):
    n_rows = u.shape[0]
    gp = S5_GROUPS * S5_STATE
    bc = lambda t: jnp.broadcast_to(t.reshape(2, gp, 1), (2, gp, S5_GROUP))
    ldt = jnp.broadcast_to(p["log_dt"][:, :, None, None], (2, S5_GROUPS, S5_STATE, S5_GROUP)).reshape(2, gp, S5_GROUP)
    per_dir = pl.BlockSpec((1, gp, S5_GROUP), lambda d: (d, 0, 0))
    shared = pl.BlockSpec((gp, S5_GROUP), lambda d: (0, 0))
    disc = jax.ShapeDtypeStruct((2, gp, S5_GROUP), F32)
    abr, abi, bbr, bbi = pl.pallas_call(
        _s5_disc_kernel,
        grid=(2,),
        in_specs=[per_dir, per_dir, per_dir, shared, shared],
        out_specs=[per_dir] * 4,
        out_shape=[disc] * 4,
        compiler_params=_cparams("parallel"),
        name="s5_disc",
    )(bc(p["A_re"]), bc(p["A_im"]), ldt, p["B_re"].reshape(gp, S5_GROUP), p["B_im"].reshape(gp, S5_GROUP))

    eye = jnp.eye(S5_GPB, dtype=F32)
    shp = (2, S5_BLOCKS, S5_GPB, S5_STATE, S5_GROUP)
    b_blk = lambda t: jnp.einsum("djgpm,gh->djgmhp", t.reshape(shp), eye).reshape(2, S5_BLOCKS, LANES, S5_HALF)
    bw = jnp.concatenate([b_blk(bbr), b_blk(bbi)], axis=-1).astype(BF16)
    cshp = (2, S5_BLOCKS, S5_GPB, S5_GROUP, S5_STATE)
    c_blk = lambda t: jnp.einsum("djgmp,gh->djgphm", t.reshape(cshp), eye).reshape(2, S5_BLOCKS, S5_HALF, LANES)
    cw = jnp.concatenate([c_blk(p["C_re"]), c_blk(-p["C_im"])], axis=2).astype(BF16)
    lam_r = abr[:, :, 0].reshape(2, 1, gp)
    lam_i = abi[:, :, 0].reshape(2, 1, gp)

    steps = min(S5_STEPS, seq)
    nc = seq // steps
    tr = steps * batch
    width = 2 * S5_HALF
    fwd = lambda c: (c, 0)
    bwd = lambda c: (nc - 1 - c, 0)
    fixed4 = lambda c: (0, 0, 0, 0)
    fixed3 = lambda c: (0, 0, 0)
    y_tile = jax.ShapeDtypeStruct((n_rows, S5_DIM), BF16)
    y_fwd, y_bwd = pl.pallas_call(
        functools.partial(_s5_scan_kernel, batch=batch, steps=steps),
        grid=(nc,),
        in_specs=[pl.BlockSpec((tr, S5_DIM), fwd), pl.BlockSpec((tr, S5_DIM), bwd),
                  pl.BlockSpec((2, S5_BLOCKS, LANES, width), fixed4),
                  pl.BlockSpec((2, S5_BLOCKS, width, LANES), fixed4),
                  pl.BlockSpec((2, 1, gp), fixed3), pl.BlockSpec((2, 1, gp), fixed3)],
        out_specs=[pl.BlockSpec((tr, S5_DIM), fwd), pl.BlockSpec((tr, S5_DIM), bwd)],
        out_shape=[y_tile, y_tile],
        scratch_shapes=[pltpu.VMEM((2, tr, S5_BLOCKS * width), F32),
                        pltpu.VMEM((2, batch, S5_BLOCKS * width), F32)],
        compiler_params=_cparams("arbitrary"),
        name="s5_scan",
    )(u, u, bw, cw, lam_r, lam_i)

    tp = min(512, n_rows)
    row = lambda i: (i, 0)
    fixed = lambda i: (0, 0)
    return pl.pallas_call(
        _s5_post_kernel,
        grid=(n_rows // tp,),
        in_specs=[pl.BlockSpec((tp, S5_DIM), row), pl.BlockSpec((tp, S5_DIM), row), pl.BlockSpec((tp, S5_DIM), row),
                  pl.BlockSpec((1, S5_DIM), fixed), pl.BlockSpec((S5_DIM, S5_DIM), fixed),
                  pl.BlockSpec((1, S5_DIM), fixed)],
        out_specs=pl.BlockSpec((tp, S5_DIM), row),
        out_shape=jax.ShapeDtypeStruct((n_rows, S5_DIM), BF16),
        compiler_params=_cparams("parallel"),
        name="s5_post",
    )(y_fwd, y_bwd, u, p["D"].reshape(1, S5_DIM), p["glu_w"].astype(BF16), p["glu_b"].reshape(1, S5_DIM))


ML_HEADS, ML_HEAD, ML_BLOCK = 8, 128, 4
ML_DIM = ML_HEADS * ML_HEAD
ML_COLS_PAD = 2 * ML_DIM + LANES
ML_CHUNK = 128
ML_BATCH_PER_STEP = 2
ML_PROJ = 256
NEG_BIG = -1e30


def _log_sigmoid(u):
    return -_softplus(-u)


def _ml_prep_kernel(x_ref, xp_ref, xn_ref, cw_ref, cb_ref, wq_ref, wkt_ref, wv_ref,
                    xc_o, q_o, kt_o, v_o, *, tiles_per_seq):
    x = x_ref[...].astype(F32)
    prev_row, next_row = _halo_rows(xp_ref, xn_ref, tiles_per_seq)
    x_prev, x_next = _shifted(x, prev_row, next_row)
    w = cw_ref[...]
    y = w[0:1] * x_prev + w[1:2] * x + w[2:3] * x_next + cb_ref[...]
    xc = y * _sigmoid(y)
    xc_o[...] = xc.astype(BF16)
    xcb = xc.astype(BF16)
    xb = x.astype(BF16)
    for j in range(ML_DIM // ML_PROJ):
        sl = slice(j * ML_PROJ, (j + 1) * ML_PROJ)
        q_o[:, sl] = jnp.dot(xcb[:, sl], wq_ref[j], preferred_element_type=F32).astype(BF16)
        kt_o[0, sl, :] = (lax.dot_general(wkt_ref[j], xcb[:, sl], (((1,), (1,)), ((), ())),
                                          preferred_element_type=F32) * (ML_HEAD ** -0.5)).astype(BF16)
        v_o[:, sl] = jnp.dot(xb[:, sl], wv_ref[j], preferred_element_type=F32).astype(BF16)


def _ml_chunk_kernel(*refs, chunk):
    L = chunk
    ins = (refs[0:6], refs[6:12])
    bias_c_ref, ib_ref, fb_ref = refs[12:15]
    h_refs = refs[15:17]
    c_s, m_s = refs[17:19]

    @pl.when(pl.program_id(1) == 0)
    def _():
        c_s[...] = jnp.zeros_like(c_s)
        m_s[...] = jnp.zeros_like(m_s)

    lane = lax.broadcasted_iota(jnp.int32, (L, LANES), 1)
    ones_tile = jnp.ones((L, ML_HEAD), BF16)
    ri = lax.broadcasted_iota(jnp.int32, (L, L), 0)
    ci = lax.broadcasted_iota(jnp.int32, (L, L), 1)
    chains = []
    for bs, dd in [(bs, dd) for bs in range(h_refs[0].shape[0]) for dd in range(2)]:
        q_ref, kt_ref, v_ref, gc_ref, gi_ref, gf_ref = ins[dd]
        sgn = 1 - 2 * dd
        incl = (ci - ri) * sgn <= 0
        tri = jnp.where(incl, 1.0, 0.0).astype(BF16)
        tri_t = jnp.where((ri - ci) * sgn <= 0, 1.0, 0.0).astype(BF16)
        gpre = gc_ref[bs] + bias_c_ref[...]
        gcol = jnp.where(lane < 2 * ML_HEADS, gpre, _log_sigmoid(gpre))
        cs_c = _xdot_r(tri, gcol)
        li_r = gi_ref[bs] + ib_ref[dd]
        lf_r = _log_sigmoid(gf_ref[bs] + fb_ref[dd])
        b_r = _xdot_l(lf_r, tri_t)
        b_last = b_r[:, L - 1:L] if dd == 0 else b_r[:, 0:1]
        lw_r = b_last - b_r + li_r
        lw_max = jnp.max(lw_r, axis=-1, keepdims=True)
        for h in range(ML_HEADS):
            sl = slice(h * ML_HEAD, (h + 1) * ML_HEAD)
            jf = 2 * ML_HEADS + ML_HEADS * dd + h
            chains.append(dict(
                bs=bs, dd=dd, h=h, sl=sl, incl=incl, q=q_ref[bs, :, sl], kt=kt_ref[bs, sl, :],
                v_ext=jnp.concatenate([v_ref[bs, :, sl], ones_tile], axis=1),
                b_t=jnp.broadcast_to(cs_c[:, jf:jf + 1], (L, LANES)),
                b_row=b_r[h:h + 1, :], li_row=li_r[h:h + 1, :], lw_row=lw_r[h:h + 1, :],
                bl=b_last[h:h + 1, :], lw_max=lw_max[h:h + 1, :],
                m_prev=m_s[bs, dd, h:h + 1, 0:1], c_ext=c_s[bs, dd, h]))

    for c in chains:
        log_d = jnp.where(c["incl"], c["b_t"] - c["b_row"] + c["li_row"], NEG_BIG)
        inter = c["b_t"] + c["m_prev"]
        m_t = jnp.maximum(jnp.broadcast_to(jnp.max(log_d, axis=-1, keepdims=True), (L, LANES)), inter)
        c["m_t"] = m_t
        c["dmat"] = jnp.exp(log_d - m_t)
        c["w_in"] = jnp.exp(inter - m_t)
        c["qk"] = _bdot(c["q"], c["kt"])
        c["qc"] = _bdot(c["q"], c["c_ext"])
        m_new = jnp.maximum(c["bl"] + c["m_prev"], c["lw_max"])
        wkt = c["kt"].astype(F32) * jnp.exp(c["lw_row"] - m_new)
        c["c_new"] = jnp.exp(c["bl"] + c["m_prev"] - m_new) * c["c_ext"] + _bdot(wkt, c["v_ext"])
        c["m_new"] = m_new
    for c in chains:
        w_in2 = jnp.concatenate([c["w_in"], c["w_in"]], axis=1)
        nd = _bdot(c["qk"] * c["dmat"], c["v_ext"]) + w_in2 * c["qc"]
        den = nd[:, ML_HEAD:]
        h_refs[c["dd"]][c["bs"], :, c["sl"]] = (nd[:, :ML_HEAD]
                                                / jnp.maximum(jnp.abs(den), jnp.exp(-c["m_t"]))).astype(BF16)
        c_s[c["bs"], c["dd"], c["h"]] = c["c_new"]
        m_s[c["bs"], c["dd"], c["h"]:c["h"] + 1, :] = jnp.broadcast_to(c["m_new"], (1, LANES))


def _ml_post_kernel(hf_ref, hb_ref, o_ref_in, xc_ref, nw_ref, skip_ref, out_ref):
    hsum = hf_ref[...].astype(F32) + hb_ref[...].astype(F32)
    for h in range(ML_HEADS):
        sl = slice(h * ML_HEAD, (h + 1) * ML_HEAD)
        x = hsum[:, sl]
        xc = x - jnp.mean(x, axis=-1, keepdims=True)
        hn = xc * lax.rsqrt(jnp.mean(xc * xc, axis=-1, keepdims=True) + EPS) * nw_ref[:, sl]
        out_ref[:, sl] = (_sigmoid(o_ref_in[:, sl].astype(F32)) * hn
                          + skip_ref[:, sl] * xc_ref[:, sl]).astype(out_ref.dtype)


def _mlstm(cols, gate_cols, g_rows, p, *, batch, seq, ts):
    n_rows = cols.shape[0]
    dim = ML_DIM
    row = lambda i: (i, 0)
    fixed2 = lambda i: (0, 0)
    fixed3 = lambda i: (0, 0, 0)
    prev, nxt = _halo_specs(ts, dim, 0, n_rows)
    nblk = dim // ML_PROJ
    per = ML_PROJ // ML_BLOCK
    repeat = jnp.asarray(np.tile(np.eye(ML_BLOCK), (1, per)), F32)
    on_diag = jnp.asarray(np.kron(np.eye(per), np.ones((ML_BLOCK, ML_BLOCK))), F32)
    blockdiag = lambda w: (jnp.einsum("brd,dn->brn", w.reshape(nblk, ML_PROJ, ML_BLOCK), repeat,
                                      precision=lax.Precision.HIGHEST) * on_diag).astype(BF16)
    wspec = pl.BlockSpec((nblk, ML_PROJ, ML_PROJ), fixed3)
    tps = seq // ts
    tile = jax.ShapeDtypeStruct((n_rows, dim), BF16)
    xc, q, k_t, v = pl.pallas_call(
        functools.partial(_ml_prep_kernel, tiles_per_seq=tps),
        grid=(n_rows // ts,),
        in_specs=[pl.BlockSpec((ts, dim), row), prev, nxt,
                  pl.BlockSpec((3, dim), fixed2), pl.BlockSpec((1, dim), fixed2), wspec, wspec, wspec],
        out_specs=[pl.BlockSpec((ts, dim), row)] * 2
                  + [pl.BlockSpec((1, dim, ts), lambda i: (i // tps, 0, i % tps)), pl.BlockSpec((ts, dim), row)],
        out_shape=[tile] * 2 + [jax.ShapeDtypeStruct((batch, dim, seq), BF16), tile],
        compiler_params=_cparams("parallel"),
        name="ml_prep",
    )(cols, cols, cols, p["conv_w"], p["conv_b"].reshape(1, dim),
      blockdiag(p["wq"]), jnp.swapaxes(blockdiag(p["wk"]), 1, 2), blockdiag(p["wv"]))

    L = min(ML_CHUNK, seq)
    nc = seq // L
    ng = 4 * ML_HEADS
    pad = jnp.zeros((LANES - ng,), F32)
    bias_c = jnp.concatenate([p["i_b"].reshape(-1), p["f_b"].reshape(-1), pad]).reshape(1, LANES)
    ib = p["i_b"].reshape(2, ML_HEADS, 1)
    fb = p["f_b"].reshape(2, ML_HEADS, 1)
    fixed = lambda bi, c: (0, 0)
    fixed3c = lambda bi, c: (0, 0, 0)

    nb = min(ML_BATCH_PER_STEP, batch)

    def dir_specs(d):
        chunk_of = (lambda c: c) if d == 0 else (lambda c: nc - 1 - c)
        rows = lambda bi, c: (bi, chunk_of(c), 0)
        return ([pl.BlockSpec((nb, L, dim), rows),
                 pl.BlockSpec((nb, dim, L), lambda bi, c: (bi, 0, chunk_of(c))),
                 pl.BlockSpec((nb, L, dim), rows),
                 pl.BlockSpec((nb, L, LANES), rows),
                 pl.BlockSpec((nb, ML_HEADS, L), lambda bi, c: (bi, d, chunk_of(c))),
                 pl.BlockSpec((nb, ML_HEADS, L), lambda bi, c: (bi, 2 + d, chunk_of(c)))],
                pl.BlockSpec((nb, L, dim), rows))

    (in_f, out_f), (in_b, out_b) = dir_specs(0), dir_specs(1)
    h_tile = jax.ShapeDtypeStruct((n_rows, dim), BF16)
    h3 = jax.ShapeDtypeStruct((batch, seq, dim), BF16)
    q3 = q.reshape(batch, seq, dim)
    v3 = v.reshape(batch, seq, dim)
    gate3 = gate_cols.reshape(batch, seq, LANES)
    h_fwd, h_bwd = pl.pallas_call(
        functools.partial(_ml_chunk_kernel, chunk=L),
        grid=(batch // nb, nc),
        in_specs=in_f + in_b + [pl.BlockSpec((1, LANES), fixed),
                                pl.BlockSpec((2, ML_HEADS, 1), fixed3c), pl.BlockSpec((2, ML_HEADS, 1), fixed3c)],
        out_specs=[out_f, out_b],
        out_shape=[h3, h3],
        scratch_shapes=[pltpu.VMEM((nb, 2, ML_HEADS, ML_HEAD, 2 * ML_HEAD), F32),
                        pltpu.VMEM((nb, 2, ML_HEADS, LANES), F32)],
        compiler_params=_cparams("parallel", "arbitrary"),
        name="ml_chunk",
    )(q3, k_t, v3, gate3, g_rows, g_rows, q3, k_t, v3, gate3, g_rows, g_rows, bias_c, ib, fb)
    h_fwd = h_fwd.reshape(n_rows, dim)
    h_bwd = h_bwd.reshape(n_rows, dim)

    return pl.pallas_call(
        _ml_post_kernel,
        grid=(n_rows // ts,),
        in_specs=[pl.BlockSpec((ts, dim), row), pl.BlockSpec((ts, dim), row),
                  pl.BlockSpec((ts, dim), lambda i: (i, 1)),
                  pl.BlockSpec((ts, dim), row),
                  pl.BlockSpec((1, dim), fixed2), pl.BlockSpec((1, dim), fixed2)],
        out_specs=pl.BlockSpec((ts, dim), row),
        out_shape=h_tile,
        compiler_params=_cparams("parallel"),
        name="ml_post",
    )(h_fwd, h_bwd, cols, xc, p["norm_w"].reshape(1, dim), p["skip"].reshape(1, dim))


ROW_TILE = 512
PROJ_TILE = 1024
MIXER_TILE = 512
FF_TILE = 2048


def _pad_cols(w, n):
    return jnp.pad(w, ((0, 0), (0, n - w.shape[1])))


def kernel(x, norm_mix, norm_mlp, norm_final, mlp_w1, mlp_w2, ab_w_in, ab_w_out, rw_mu, rw_w0, rw_w2, rw_a0, rw_a2, rw_g2, rw_k_k, rw_k_a, rw_r_k, rw_ln_w, mb_conv_w, mb_conv_b, mb_dt_bias, mb_A_log, mb_D, mb_norm_w, cd_w_in, cd_w_out, s5_A_re, s5_A_im, s5_log_dt, s5_B_re, s5_B_im, s5_C_re, s5_C_im, s5_D, s5_glu_w, s5_glu_b, ml_conv_w, ml_conv_b, ml_wq, ml_wk, ml_wv, ml_i_b, ml_f_b, ml_norm_w, ml_skip):
    batch, seq, dm = x.shape
    n_rows = batch * seq
    tm = min(ROW_TILE, seq)
    ts = min(MIXER_TILE, seq)
    tiles_per_seq = seq // tm
    tp = min(PROJ_TILE, seq)
    proj_tiles_per_seq = seq // tp
    h = x.reshape(n_rows, dm)
    depth = norm_mix.shape[0]
    for layer in range(depth):
        i = layer // 2
        last = layer == depth - 1
        if layer % 2 == 0:
            w_in = ab_w_in[i]
            rw_cols = _norm_matmul(h, norm_mix[layer], w_in[:, :RW_COLS].astype(BF16), tm=tp)
            mb_cols, mb_dt, mb_dt_t = _norm_matmul(h, norm_mix[layer],
                                                   _pad_cols(w_in[:, RW_COLS:], MB_COLS_PAD).astype(BF16),
                                                   tm=tp, seq=seq, tail=True)
            y1 = _rwkv7(rw_cols, dict(mu=rw_mu[i], w0=rw_w0[i], w2=rw_w2[i], a0=rw_a0[i], a2=rw_a2[i], g2=rw_g2[i],
                                      k_k=rw_k_k[i], k_a=rw_k_a[i], r_k=rw_r_k[i].reshape(-1), ln_w=rw_ln_w[i]),
                        batch=batch, seq=seq, ts=ts)
            y2 = _mamba2(mb_cols, mb_dt, mb_dt_t, dict(conv_w=mb_conv_w[i], conv_b=mb_conv_b[i], dt_bias=mb_dt_bias[i],
                                       A_log=mb_A_log[i], D=mb_D[i], norm_w=mb_norm_w[i]),
                         batch=batch, seq=seq, ts=ts)
            y1_spec = pl.BlockSpec((tm, RW_DIM), lambda r, k: (r, 0))
            w_out, k1 = ab_w_out[i], RW_DIM
        else:
            w_in = cd_w_in[i]
            tm_spec = pl.BlockSpec((tp, S5_DIM), lambda r: (r % proj_tiles_per_seq, r // proj_tiles_per_seq))
            s5_cols = _norm_matmul(h, norm_mix[layer], w_in[:, :S5_DIM].astype(BF16), tm=tp,
                                   out_spec=tm_spec, out_shape=(seq, batch * S5_DIM))
            ml_cols, ml_gates, ml_gates_t = _norm_matmul(h, norm_mix[layer],
                                                         _pad_cols(w_in[:, S5_DIM:], ML_COLS_PAD).astype(BF16),
                                                         tm=tp, seq=seq, tail=True)
            y1 = _s5(s5_cols.reshape(seq * batch, S5_DIM),
                     dict(A_re=s5_A_re[i], A_im=s5_A_im[i], log_dt=s5_log_dt[i], B_re=s5_B_re[i], B_im=s5_B_im[i],
                          C_re=s5_C_re[i], C_im=s5_C_im[i], D=s5_D[i], glu_w=s5_glu_w[i], glu_b=s5_glu_b[i]),
                     batch=batch, seq=seq).reshape(seq, batch * S5_DIM)
            y2 = _mlstm(ml_cols, ml_gates, ml_gates_t, dict(conv_w=ml_conv_w[i], conv_b=ml_conv_b[i], wq=ml_wq[i], wk=ml_wk[i], wv=ml_wv[i],
                                      i_b=ml_i_b[i], f_b=ml_f_b[i], norm_w=ml_norm_w[i], skip=ml_skip[i]),
                        batch=batch, seq=seq, ts=ts)
            y1_spec = pl.BlockSpec((tm, S5_DIM), lambda r, k: (r % tiles_per_seq, r // tiles_per_seq))
            w_out, k1 = cd_w_out[i], S5_DIM
        h = _mix_mlp(h, y1, y1_spec, y2, w_out[:k1].astype(BF16), w_out[k1:].astype(BF16), norm_mlp[layer],
                     mlp_w1[layer].astype(BF16), mlp_w2[layer].astype(BF16), norm_final,
                     tm=tm, tf=FF_TILE, final_norm=last)
    return h.reshape(batch, seq, dm)
```
